```python
import math
import jax, jax.numpy as jnp
from jax import lax
import numpy as np

D_MODEL = 2048
BATCH = 1
SEQ = 16384
DEPTH = 1
DEC_BATCH = 32
DEC_SEQ = 4
PAST_LEN = 16384
PAGE_SIZE = 128

M_HEADS = 4
M_DQK = 128
M_DV = 256
M_CHUNK = 128
HEAD_EPS = 1e-6
A_HEADS = 8
A_DH = 128
PATTERNS = ((128, 1), (512, 4), (2048, 16))
WIN_MAX = 2048
BAND_BLOCK = 128
N_MEM = 256
X_HEADS = 4
X_DH = D_MODEL // X_HEADS
N_EXPERTS = 32
TOP_K = 4
D_FF = 2048
SWIGLU_LIMIT = 7.0
SWIGLU_ALPHA = 1.702
MOE_BLOCK = 128
DN_ALPHA = (2 * DEPTH) ** 0.25
DN_BETA = (8 * DEPTH) ** -0.25
LN_EPS = 1e-5

MIX_V = M_HEADS * M_DV
MIX_A = A_HEADS * A_DH
D_MIX = MIX_V + MIX_A
_COLS = (M_HEADS * M_DQK, M_HEADS * M_DQK, MIX_V, MIX_V, M_HEADS, M_HEADS, MIX_A, MIX_A, MIX_A)
_COL_BETA = (False, False, True, False, False, False, False, False, True)
D_IN = sum(_COLS)

kernel_name = 'hybrid_mlstm_dilated_swa_memxattn_moe_step'


def layer_norm(x, g, b):
    xf = x.astype(jnp.float32)
    mu = jnp.mean(xf, axis=-1, keepdims=True)
    var = jnp.mean(jnp.square(xf - mu), axis=-1, keepdims=True)
    return ((xf - mu) * lax.rsqrt(var + LN_EPS) * g + b).astype(x.dtype)


def mix_project(x, w_in, b_igate, b_fgate):
    B, T, _ = x.shape
    z = jnp.einsum('btd,dc->btc', x, w_in)
    splits = np.cumsum(_COLS)[:-1].tolist()
    zq, zk, zv, zo, zi, zf, aq, ak, av = jnp.split(z, splits, axis=-1)

    def heads(a, h):
        return a.reshape(B, T, h, -1).transpose(0, 2, 1, 3).astype(jnp.float32)

    mq = heads(zq, M_HEADS)
    mk = heads(zk, M_HEADS) * (M_DQK ** -0.5)
    mv = heads(zv, M_HEADS)
    mo = jax.nn.sigmoid(zo.astype(jnp.float32))
    ig = (zi.astype(jnp.float32) + b_igate).transpose(0, 2, 1)
    lf = jax.nn.log_sigmoid(zf.astype(jnp.float32) + b_fgate).transpose(0, 2, 1)
    rs = lambda a: a.reshape(B, T, A_HEADS, A_DH)
    return mq, mk, mv, mo, ig, lf, rs(aq), rs(ak), rs(av)


def mlstm_chunkwise(q, k, v, ig, lf, C0, n0, m0):
    B, H, T, _ = q.shape
    L = M_CHUNK if T % M_CHUNK == 0 else T
    nc = T // L

    def to_chunks(a):
        return jnp.moveaxis(a.reshape(B, H, nc, L, *a.shape[3:]), 2, 0)

    causal = jnp.tril(jnp.ones((L, L), dtype=bool))

    def step(carry, inp):
        C, n, m = carry
        qc, kc, vc, ic, fc = inp
        b = jnp.cumsum(fc, axis=-1)
        logD = jnp.where(causal, b[..., :, None] - b[..., None, :] + ic[..., None, :], -jnp.inf)
        inter = b + m[..., None]
        mt = jnp.maximum(inter, jnp.max(logD, axis=-1))
        Sd = jnp.einsum('bhtk,bhsk->bhts', qc, kc) * jnp.exp(logD - mt[..., None])
        sc = jnp.exp(inter - mt)
        num = jnp.einsum('bhts,bhsv->bhtv', Sd, vc) + sc[..., None] * jnp.einsum('bhtk,bhkv->bhtv', qc, C)
        den = jnp.sum(Sd, axis=-1) + sc * jnp.einsum('bhtk,bhk->bht', qc, n)
        h = num / jnp.maximum(jnp.abs(den), jnp.exp(-mt))[..., None]
        bL = b[..., -1]
        mL = mt[..., -1]
        w = jnp.exp(bL[..., None] - b + ic - mL[..., None])
        scL = jnp.exp(bL + m - mL)
        C_new = scL[..., None, None] * C + jnp.einsum('bhs,bhsk,bhsv->bhkv', w, kc, vc)
        n_new = scL[..., None] * n + jnp.einsum('bhs,bhsk->bhk', w, kc)
        return (C_new, n_new, mL), h

    (C, n, m), hs = lax.scan(step, (C0, n0, m0), (to_chunks(q), to_chunks(k), to_chunks(v), to_chunks(ig), to_chunks(lf)))
    h = jnp.moveaxis(hs, 0, 2).reshape(B, H, T, -1)
    return h, C, n, m


def mlstm_group(mq, mk, mv, mo, ig, lf, C0, n0, m0, g_mnorm):
    h, C, n, m = mlstm_chunkwise(mq, mk, mv, ig, lf, C0.astype(jnp.float32), n0.astype(jnp.float32), m0.astype(jnp.float32))
    h = h * lax.rsqrt(jnp.mean(h * h, axis=-1, keepdims=True) + HEAD_EPS)
    B, H, T, E = h.shape
    h = h.transpose(0, 2, 1, 3).reshape(B, T, H * E) * g_mnorm * mo
    return h, C, n, m


def _dilated_band(q, k, v, span, dil):
    B, T, H, E = q.shape
    N = T // dil
    NB = -(-N // BAND_BLOCK)
    Np = NB * BAND_BLOCK

    def deint(a):
        a = a.reshape(B, N, dil, H, E).transpose(0, 2, 1, 3, 4).reshape(B * dil, N, H, E)
        return jnp.pad(a, ((0, 0), (0, Np - N), (0, 0), (0, 0))).reshape(B * dil, NB, BAND_BLOCK, H, E)

    def with_prev(a):
        prev = jnp.pad(a[:, :-1], ((0, 0), (1, 0), (0, 0), (0, 0), (0, 0)))
        return jnp.concatenate([prev, a], axis=2)

    qb = deint(q)
    kk = with_prev(deint(k))
    vv = with_prev(deint(v)).astype(jnp.float32)
    s = jnp.einsum('bnqhe,bnkhe->bnhqk', qb, kk, preferred_element_type=jnp.float32) * (E ** -0.5)
    iq = jnp.arange(BAND_BLOCK)[:, None]
    ik = jnp.arange(2 * BAND_BLOCK)[None, :]
    dist = BAND_BLOCK + iq - ik
    key_row = (jnp.arange(NB)[:, None, None] - 1) * BAND_BLOCK + ik[None]
    mask = (dist >= 0)[None] & (dist <= span)[None] & (key_row >= 0)
    s = jnp.where(mask[None, :, None], s, -jnp.inf)
    lse = jax.nn.logsumexp(s, axis=-1)
    p = jnp.exp(s - lse[..., None])
    o = jnp.einsum('bnhqk,bnkhe->bnqhe', p, vv)
    o = o.reshape(B, dil, Np, H, E)[:, :, :N].transpose(0, 2, 1, 3, 4).reshape(B, T, H, E)
    lse = lse.transpose(0, 1, 3, 2).reshape(B, dil, Np, H)[:, :, :N].transpose(0, 2, 1, 3).reshape(B, T, H)
    return o, lse


def combine_patterns(outs, lses):
    w = jax.nn.softmax(jnp.stack(lses, axis=-1), axis=-1)
    o = jnp.einsum('bthg,gbthe->bthe', w, jnp.stack(outs, axis=0))
    B, T, H, E = o.shape
    return o.reshape(B, T, H * E)


def dilated_prompt(q, k, v):
    outs, lses = [], []
    for window, dil in PATTERNS:
        o, l = _dilated_band(q, k, v, window // dil, dil)
        outs.append(o)
        lses.append(l)
    return combine_patterns(outs, lses)


def dilated_sample(q, k, v, kbuf, vbuf):
    DB, S, H, E = q.shape
    WB = kbuf.shape[1]
    kf = jnp.concatenate([kbuf.astype(k.dtype), k], axis=1)
    vf = jnp.concatenate([vbuf.astype(v.dtype), v], axis=1)
    outs, lses = [], []
    for window, dil in PATTERNS:
        span = window // dil
        idx = WB + jnp.arange(S)[:, None] - dil * jnp.arange(span + 1)[None, :]
        valid = idx >= 0
        idx = jnp.maximum(idx, 0)
        kg = kf[:, idx]
        vg = vf[:, idx].astype(jnp.float32)
        s = jnp.einsum('bshe,bsjhe->bhsj', q, kg, preferred_element_type=jnp.float32) * (E ** -0.5)
        s = jnp.where(valid[None, None], s, -jnp.inf)
        lse = jax.nn.logsumexp(s, axis=-1)
        p = jnp.exp(s - lse[..., None])
        outs.append(jnp.einsum('bhsj,bsjhe->bshe', p, vg))
        lses.append(lse.transpose(0, 2, 1))
    return combine_patterns(outs, lses)


def memory_kv(mem, w_xk, w_xv):
    B, M, _ = mem.shape
    mk = jnp.einsum('bmd,dc->bmc', mem, w_xk).reshape(B, M, X_HEADS, X_DH)
    mv = jnp.einsum('bmd,dc->bmc', mem, w_xv).reshape(B, M, X_HEADS, X_DH)
    return mk, mv


def cross_attend(x, mk, mv, w_xq, w_xo):
    B, T, _ = x.shape
    q = jnp.einsum('btd,dc->btc', x, w_xq).reshape(B, T, X_HEADS, X_DH)
    s = jnp.einsum('bthe,bmhe->bhtm', q, mk, preferred_element_type=jnp.float32) * (X_DH ** -0.5)
    p = jax.nn.softmax(s, axis=-1)
    o = jnp.einsum('bhtm,bmhe->bthe', p, mv.astype(jnp.float32)).reshape(B, T, X_HEADS * X_DH)
    return jnp.einsum('btc,cd->btd', o.astype(x.dtype), w_xo)


def moe(x, w_router, b_router, w_gate, b_gate, w_up, b_up, w_down, b_down):
    B, T, D = x.shape
    xt = x.reshape(B * T, D)
    Ntok = B * T
    logits = jnp.einsum('nd,de->ne', xt, w_router).astype(jnp.float32) + b_router
    topv, topi = lax.top_k(logits, TOP_K)
    gates = jax.nn.softmax(topv, axis=-1)
    NK = Ntok * TOP_K
    e_flat = topi.reshape(-1)
    order = jnp.argsort(e_flat)
    e_sorted = e_flat[order]
    tok_sorted = (order // TOP_K).astype(jnp.int32)
    g_sorted = gates.reshape(-1)[order]
    counts = jnp.zeros((N_EXPERTS,), jnp.int32).at[e_flat].add(1)
    starts = jnp.cumsum(counts) - counts
    padded = (counts + MOE_BLOCK - 1) // MOE_BLOCK * MOE_BLOCK
    pends = jnp.cumsum(padded)
    pstarts = pends - padded
    dest = pstarts[e_sorted] + (jnp.arange(NK) - starts[e_sorted])
    P = -(-(NK + N_EXPERTS * (MOE_BLOCK - 1)) // MOE_BLOCK) * MOE_BLOCK
    NB = P // MOE_BLOCK
    row_tok = jnp.full((P,), Ntok, jnp.int32).at[dest].set(tok_sorted)
    row_gate = jnp.zeros((P,), jnp.float32).at[dest].set(g_sorted)
    blk_exp = jnp.minimum(jnp.searchsorted(pends, jnp.arange(NB) * MOE_BLOCK, side='right'), N_EXPERTS - 1)
    xpad = jnp.concatenate([xt, jnp.zeros((1, D), xt.dtype)], axis=0)

    def expert_block(args):
        rows, e = args
        xb = xpad[rows]
        g = jnp.minimum(xb @ w_gate[e] + b_gate[e], SWIGLU_LIMIT)
        u = jnp.clip(xb @ w_up[e] + b_up[e], -SWIGLU_LIMIT, SWIGLU_LIMIT)
        h = (u + 1.0) * (g * jax.nn.sigmoid(SWIGLU_ALPHA * g))
        return h @ w_down[e] + b_down[e]

    yb = lax.map(expert_block, (row_tok.reshape(NB, MOE_BLOCK), blk_exp))
    yb = yb.reshape(P, D).astype(jnp.float32) * row_gate[:, None]
    y = jnp.zeros((Ntok + 1, D), jnp.float32).at[row_tok].add(yb)[:Ntok]
    return y.reshape(B, T, D).astype(x.dtype)


def finish_layer(x, hm, ha, mem_k, mem_v, w_mix_out, ln1_g, ln1_b, w_xq, w_xo, ln2_g, ln2_b,
                 w_router, b_router, w_gate, b_gate, w_up, b_up, w_down, b_down, ln3_g, ln3_b):
    mix = jnp.einsum('btc,cd->btd', jnp.concatenate([hm, ha.astype(hm.dtype)], axis=-1), w_mix_out)
    x = layer_norm(DN_ALPHA * x + mix, ln1_g, ln1_b)
    x = layer_norm(DN_ALPHA * x + cross_attend(x, mem_k, mem_v, w_xq, w_xo), ln2_g, ln2_b)
    x = layer_norm(DN_ALPHA * x + moe(x, w_router, b_router, w_gate, b_gate, w_up, b_up, w_down, b_down), ln3_g, ln3_b)
    return x


def setup_inputs(seed: int = 0) -> dict:
    key = jax.random.key(seed)
    ks = iter(jax.random.split(key, 48))
    nrm = lambda shape, scale=1.0: jax.random.normal(next(ks), shape, jnp.float32) * scale
    WB = min(WIN_MAX, PAST_LEN)
    w_in = jnp.concatenate([nrm((DEPTH, D_MODEL, c), D_MODEL ** -0.5 * (DN_BETA if vb else 1.0))
                            for c, vb in zip(_COLS, _COL_BETA)], axis=-1)
    return {
        'x_prompt': nrm((BATCH, SEQ, D_MODEL)),
        'x_sample': nrm((DEC_BATCH, DEC_SEQ, D_MODEL)),
        'mem_prompt': nrm((BATCH, N_MEM, D_MODEL)),
        'state_mlstm_C': nrm((DEPTH, DEC_BATCH, M_HEADS, M_DQK, M_DV), 0.5),
        'state_mlstm_n': nrm((DEPTH, DEC_BATCH, M_HEADS, M_DQK), 0.5),
        'state_mlstm_m': 1.0 + nrm((DEPTH, DEC_BATCH, M_HEADS), 0.5),
        'cache_swa_k': nrm((DEPTH, DEC_BATCH, WB, A_HEADS, A_DH)),
        'cache_swa_v': nrm((DEPTH, DEC_BATCH, WB, A_HEADS, A_DH), DN_BETA),
        'cache_mem_k': nrm((DEPTH, DEC_BATCH, N_MEM, X_HEADS, X_DH)),
        'cache_mem_v': nrm((DEPTH, DEC_BATCH, N_MEM, X_HEADS, X_DH), DN_BETA),
        'w_in': w_in,
        'b_igate': nrm((DEPTH, M_HEADS), 0.1),
        'b_fgate': 3.0 + nrm((DEPTH, M_HEADS), 0.5),
        'g_mnorm': 1.0 + nrm((DEPTH, MIX_V), 0.02),
        'w_mix_out': nrm((DEPTH, D_MIX, D_MODEL), D_MIX ** -0.5 * DN_BETA),
        'ln1_g': 1.0 + nrm((DEPTH, D_MODEL), 0.02),
        'ln1_b': nrm((DEPTH, D_MODEL), 0.02),
        'w_xq': nrm((DEPTH, D_MODEL, X_HEADS * X_DH), D_MODEL ** -0.5),
        'w_xk': nrm((DEPTH, D_MODEL, X_HEADS * X_DH), D_MODEL ** -0.5),
        'w_xv': nrm((DEPTH, D_MODEL, X_HEADS * X_DH), D_MODEL ** -0.5 * DN_BETA),
        'w_xo': nrm((DEPTH, X_HEADS * X_DH, D_MODEL), (X_HEADS * X_DH) ** -0.5 * DN_BETA),
        'ln2_g': 1.0 + nrm((DEPTH, D_MODEL), 0.02),
        'ln2_b': nrm((DEPTH, D_MODEL), 0.02),
        'w_router': nrm((DEPTH, D_MODEL, N_EXPERTS), D_MODEL ** -0.5),
        'b_router': nrm((DEPTH, N_EXPERTS), 0.01),
        'w_gate': nrm((DEPTH, N_EXPERTS, D_MODEL, D_FF), D_MODEL ** -0.5),
        'b_gate': nrm((DEPTH, N_EXPERTS, D_FF), 0.01),
        'w_up': nrm((DEPTH, N_EXPERTS, D_MODEL, D_FF), D_MODEL ** -0.5),
        'b_up': nrm((DEPTH, N_EXPERTS, D_FF), 0.01),
        'w_down': nrm((DEPTH, N_EXPERTS, D_FF, D_MODEL), D_FF ** -0.5 * DN_BETA),
        'b_down': nrm((DEPTH, N_EXPERTS, D_MODEL), 0.01),
        'ln3_g': 1.0 + nrm((DEPTH, D_MODEL), 0.02),
        'ln3_b': nrm((DEPTH, D_MODEL), 0.02),
    }


def reference(x_prompt, x_sample, mem_prompt, state_mlstm_C, state_mlstm_n, state_mlstm_m,
              cache_swa_k, cache_swa_v, cache_mem_k, cache_mem_v,
              w_in, b_igate, b_fgate, g_mnorm, w_mix_out, ln1_g, ln1_b,
              w_xq, w_xk, w_xv, w_xo, ln2_g, ln2_b,
              w_router, b_router, w_gate, b_gate, w_up, b_up, w_down, b_down, ln3_g, ln3_b):
    hp, hs = x_prompt, x_sample
    pC, pn, pm, pk, pv, pmk, pmv = [], [], [], [], [], [], []
    sC, sn, sm, sk, sv = [], [], [], [], []
    for l in range(DEPTH):
        lw = (w_mix_out[l], ln1_g[l], ln1_b[l], w_xq[l], w_xo[l], ln2_g[l], ln2_b[l],
              w_router[l], b_router[l], w_gate[l], b_gate[l], w_up[l], b_up[l], w_down[l], b_down[l],
              ln3_g[l], ln3_b[l])
        B, T, _ = hp.shape
        mq, mk, mv, mo, ig, lf, aq, ak, av = mix_project(hp, w_in[l], b_igate[l], b_fgate[l])
        C0 = jnp.zeros((B, M_HEADS, M_DQK, M_DV), jnp.float32)
        n0 = jnp.zeros((B, M_HEADS, M_DQK), jnp.float32)
        m0 = jnp.zeros((B, M_HEADS), jnp.float32)
        hm, C, n, m = mlstm_group(mq, mk, mv, mo, ig, lf, C0, n0, m0, g_mnorm[l])
        ha = dilated_prompt(aq, ak, av)
        mem_k, mem_v = memory_kv(mem_prompt, w_xk[l], w_xv[l])
        wb = min(WIN_MAX, T)
        pC.append(C); pn.append(n); pm.append(m)
        pk.append(ak[:, T - wb:]); pv.append(av[:, T - wb:])
        pmk.append(mem_k); pmv.append(mem_v)
        hp = finish_layer(hp, hm, ha, mem_k, mem_v, *lw)
        mq, mk, mv, mo, ig, lf, aq, ak, av = mix_project(hs, w_in[l], b_igate[l], b_fgate[l])
        hm, C, n, m = mlstm_group(mq, mk, mv, mo, ig, lf, state_mlstm_C[l], state_mlstm_n[l], state_mlstm_m[l], g_mnorm[l])
        ha = dilated_sample(aq, ak, av, cache_swa_k[l], cache_swa_v[l])
        sC.append(C); sn.append(n); sm.append(m)
        sk.append(ak); sv.append(av)
        hs = finish_layer(hs, hm, ha, cache_mem_k[l], cache_mem_v[l], *lw)
    return (hp, hs,
            jnp.stack(pC), jnp.stack(pn), jnp.stack(pm), jnp.stack(pk), jnp.stack(pv), jnp.stack(pmk), jnp.stack(pmv),
            jnp.stack(sC), jnp.stack(sn), jnp.stack(sm), jnp.stack(sk), jnp.stack(sv))
```

```python
import functools

import jax
import jax.numpy as jnp
from jax import lax
from jax.experimental import pallas as pl
from jax.experimental.pallas import tpu as pltpu

F32, BF16, I32 = jnp.float32, jnp.bfloat16, jnp.int32

D_MODEL = 2048
DEPTH = 1
M_HEADS, M_DQK, M_DV = 4, 128, 256
HEAD_EPS = 1e-6
A_HEADS, A_DH = 8, 128
PATTERNS = ((128, 1), (512, 4), (2048, 16))
BAND_BLOCK = 128
X_HEADS = 4
X_DH = D_MODEL // X_HEADS
N_EXPERTS, TOP_K, D_FF = 32, 4, 2048
SWIGLU_LIMIT, SWIGLU_ALPHA = 7.0, 1.702
DN_ALPHA = (2 * DEPTH) ** 0.25
LN_EPS = 1e-5
MIX_V = M_HEADS * M_DV
MIX_A = A_HEADS * A_DH
MIX_QK = M_HEADS * M_DQK

LANES = 128
VMEM_LIMIT_BYTES = 58 * 1024 * 1024

NEG_BIG = -1e30


def _cparams(n_axes, vmem=VMEM_LIMIT_BYTES):
    return pltpu.CompilerParams(dimension_semantics=("arbitrary",) * n_axes, vmem_limit_bytes=vmem)


def _dot(a, b):
    return jnp.dot(a, b, preferred_element_type=F32)


def _dot_nt(a, b):
    return lax.dot_general(a, b, (((1,), (1,)), ((), ())), preferred_element_type=F32)


def _dot_tn(a, b):
    return lax.dot_general(a, b, (((0,), (0,)), ((), ())), preferred_element_type=F32)


def _log_sigmoid(x):
    return jnp.minimum(x, 0.0) - jnp.log(1.0 + jnp.exp(-jnp.abs(x)))


def _sigmoid(x):
    return 1.0 / (1.0 + jnp.exp(-x))


def _layer_norm(v, g, b):
    mu = jnp.mean(v, axis=-1, keepdims=True)
    d = v - mu
    var = jnp.mean(d * d, axis=-1, keepdims=True)
    return d * lax.rsqrt(var + LN_EPS) * g + b


def _resident(shape):
    nd = len(shape)
    return pl.BlockSpec(shape, lambda *_: (0,) * nd, pipeline_mode=pl.Buffered(1))


def _mm_body(x_ref, w_ref, o_ref):
    o_ref[...] = _dot(x_ref[...].astype(BF16), w_ref[...]).astype(o_ref.dtype)


def _mm(x, w, out_dtype, tm, tn, name):
    M, K = x.shape
    N = w.shape[1]
    assert M % tm == 0 and N % tn == 0
    return pl.pallas_call(
        _mm_body,
        grid=(M // tm, N // tn),
        in_specs=[pl.BlockSpec((tm, K), lambda i, j: (i, 0)), pl.BlockSpec((K, tn), lambda i, j: (0, j))],
        out_specs=pl.BlockSpec((tm, tn), lambda i, j: (i, j)),
        out_shape=jax.ShapeDtypeStruct((M, N), out_dtype),
        compiler_params=_cparams(2),
        name=name,
    )(x, w)


def _mlstm_body(q_ref, k_ref, v_ref, zo_ref, zg_ref, gb_ref, gm_ref, c0_ref, n0_ref, m0_ref,
                hm_ref, c_ref, n_ref, m_ref, *, L):
    @pl.when(pl.program_id(1) == 0)
    def _():
        c_ref[...] = c0_ref[...]
        n_ref[...] = n0_ref[...]
        m_ref[...] = m0_ref[...]

    scale = M_DQK ** -0.5
    g = zg_ref[...] + gb_ref[...]
    gt = g.T
    row = lax.broadcasted_iota(I32, (L, L), 0)
    col = lax.broadcasted_iota(I32, (L, L), 1)
    tri = row >= col
    for h in range(M_HEADS):
        i_col = g[:, h:h + 1]
        f_col = _log_sigmoid(g[:, M_HEADS + h:M_HEADS + h + 1])
        i_row = gt[h:h + 1, :]
        f_row = _log_sigmoid(gt[M_HEADS + h:M_HEADS + h + 1, :])
        q = q_ref[:, h * M_DQK:(h + 1) * M_DQK]
        k = k_ref[:, h * M_DQK:(h + 1) * M_DQK]
        v = v_ref[:, h * M_DV:(h + 1) * M_DV]
        zo = zo_ref[:, h * M_DV:(h + 1) * M_DV].astype(F32)
        C = c_ref[0, h]
        n = n_ref[0, h:h + 1, :]
        m = m_ref[0, h:h + 1, 0:1]
        b_col = jnp.sum(jnp.where(tri, f_row, 0.0), axis=1, keepdims=True)
        b_row = jnp.sum(jnp.where(row <= col, f_col, 0.0), axis=0, keepdims=True)
        logd = jnp.where(tri, b_col - b_row + i_row, -jnp.inf)
        inter = b_col + m
        mt = jnp.maximum(inter, jnp.max(logd, axis=1, keepdims=True))
        sd = _dot_nt(q, k) * scale * jnp.exp(logd - mt)
        sc = jnp.exp(inter - mt)
        num = _dot(sd.astype(BF16), v) + sc * _dot(q, C.astype(BF16))
        den = jnp.sum(sd, axis=1, keepdims=True) + sc * jnp.sum(q.astype(F32) * n, axis=1, keepdims=True)
        hh = num / jnp.maximum(jnp.abs(den), jnp.exp(-mt))
        hn = hh * lax.rsqrt(jnp.mean(hh * hh, axis=1, keepdims=True) + HEAD_EPS)
        out = hn * gm_ref[:, h * M_DV:(h + 1) * M_DV] * _sigmoid(zo)
        hm_ref[:, h * M_DV:(h + 1) * M_DV] = out.astype(hm_ref.dtype)
        bl = b_col[L - 1:L, :]
        ml = mt[L - 1:L, :]
        w_col = jnp.exp(bl - b_col + i_col - ml)
        scl = jnp.exp(bl + m - ml)
        kw = k.astype(F32) * (w_col * scale)
        c_ref[0, h] = scl * C + _dot_tn(kw.astype(BF16), v)
        n_ref[0, h:h + 1, :] = scl * n + jnp.sum(kw, axis=0, keepdims=True)
        m_ref[0, h:h + 1, :] = jnp.broadcast_to(ml, (1, LANES))


def _mlstm(zm, zg, gate_bias, g_mnorm, c0, n0, m0, B, nc, L=128):
    T = B * nc * L
    assert zm.shape == (T, 2 * MIX_QK + 2 * MIX_V)
    row_blk = lambda b, c: b * nc + c
    state_specs = [pl.BlockSpec((1, M_HEADS, M_DQK, M_DV), lambda b, c: (b, 0, 0, 0)),
                   pl.BlockSpec((1, M_HEADS, M_DQK), lambda b, c: (b, 0, 0)),
                   pl.BlockSpec((1, M_HEADS, LANES), lambda b, c: (b, 0, 0))]
    return pl.pallas_call(
        functools.partial(_mlstm_body, L=L),
        grid=(B, nc),
        in_specs=[pl.BlockSpec((L, MIX_QK), lambda b, c: (row_blk(b, c), 0)),
                  pl.BlockSpec((L, MIX_QK), lambda b, c: (row_blk(b, c), 1)),
                  pl.BlockSpec((L, MIX_V), lambda b, c: (row_blk(b, c), 1)),
                  pl.BlockSpec((L, MIX_V), lambda b, c: (row_blk(b, c), 2)),
                  pl.BlockSpec((L, LANES), lambda b, c: (row_blk(b, c), 0)),
                  pl.BlockSpec((1, LANES), lambda b, c: (0, 0)),
                  pl.BlockSpec((1, MIX_V), lambda b, c: (0, 0))] + state_specs,
        out_specs=[pl.BlockSpec((L, MIX_V), lambda b, c: (row_blk(b, c), 0))] + state_specs,
        out_shape=[jax.ShapeDtypeStruct((T, MIX_V), BF16),
                   jax.ShapeDtypeStruct((B, M_HEADS, M_DQK, M_DV), F32),
                   jax.ShapeDtypeStruct((B, M_HEADS, M_DQK), F32),
                   jax.ShapeDtypeStruct((B, M_HEADS, LANES), F32)],
        compiler_params=_cparams(2),
        name="mlstm",
    )(zm, zm, zm, zm, zg, gate_bias, g_mnorm, c0, n0, m0)


def _dil_body(q_ref, kc_ref, kp_ref, vc_ref, vp_ref, o_ref, l_ref, *, span):
    scale = A_DH ** -0.5
    n = pl.program_id(1)
    row = lax.broadcasted_iota(I32, (BAND_BLOCK, BAND_BLOCK), 0)
    col = lax.broadcasted_iota(I32, (BAND_BLOCK, BAND_BLOCK), 1)
    first = jnp.where(n > 0, 0, 2 * BAND_BLOCK)
    mask_p = (BAND_BLOCK + row - col + first) <= span
    mask_c = row >= col
    lane = lax.broadcasted_iota(I32, (BAND_BLOCK, LANES), 1)
    lse_tile = jnp.zeros((BAND_BLOCK, LANES), F32)
    for h in range(A_HEADS):
        sl = slice(h * A_DH, (h + 1) * A_DH)
        q = q_ref[:, sl]
        sp = jnp.where(mask_p, _dot_nt(q, kp_ref[:, sl]) * scale, -jnp.inf)
        sc = jnp.where(mask_c, _dot_nt(q, kc_ref[:, sl]) * scale, -jnp.inf)
        m = jnp.maximum(jnp.max(sp, axis=1, keepdims=True), jnp.max(sc, axis=1, keepdims=True))
        pp = jnp.exp(sp - m)
        pc = jnp.exp(sc - m)
        l = jnp.sum(pp, axis=1, keepdims=True) + jnp.sum(pc, axis=1, keepdims=True)
        o = (_dot(pp.astype(BF16), vp_ref[:, sl]) + _dot(pc.astype(BF16), vc_ref[:, sl])) / l
        o_ref[:, sl] = o.astype(o_ref.dtype)
        lse_tile = jnp.where(lane == h, m + jnp.log(l), lse_tile)
    l_ref[...] = lse_tile


def _dilated_pattern(za, window, dil):
    T = za.shape[0]
    span = window // dil
    assert T % (dil * BAND_BLOCK) == 0 and BAND_BLOCK - 1 <= span
    N = T // dil
    nb = N // BAND_BLOCK
    zv = za.reshape(N, dil * 3 * MIX_A)
    prev = lambda n: jnp.maximum(n - 1, 0)
    blk = (BAND_BLOCK, MIX_A)
    o, lse = pl.pallas_call(
        functools.partial(_dil_body, span=span),
        grid=(dil, nb),
        in_specs=[pl.BlockSpec(blk, lambda r, n: (n, 3 * r)),
                  pl.BlockSpec(blk, lambda r, n: (n, 3 * r + 1)),
                  pl.BlockSpec(blk, lambda r, n: (prev(n), 3 * r + 1)),
                  pl.BlockSpec(blk, lambda r, n: (n, 3 * r + 2)),
                  pl.BlockSpec(blk, lambda r, n: (prev(n), 3 * r + 2))],
        out_specs=[pl.BlockSpec(blk, lambda r, n: (n, r)),
                   pl.BlockSpec((BAND_BLOCK, LANES), lambda r, n: (n, r))],
        out_shape=[jax.ShapeDtypeStruct((N, dil * MIX_A), BF16),
                   jax.ShapeDtypeStruct((N, dil * LANES), F32)],
        compiler_params=_cparams(2),
        name=f"dilated_d{dil}",
    )(zv, zv, zv, zv, zv)
    return o.reshape(T, MIX_A), lse.reshape(T, LANES)


SWA_ROWS = 16
SWA_HEADS_PER_STEP = 4


def _swa_body(q_ref, kn_ref, vn_ref, kb_ref, vb_ref, o_ref, *, wb):
    scale = A_DH ** -0.5
    s_k = lax.broadcasted_iota(I32, (SWA_ROWS, wb), 0)
    p_k = lax.broadcasted_iota(I32, (SWA_ROWS, wb), 1)
    d_k = wb + s_k - p_k
    s_n = lax.broadcasted_iota(I32, (SWA_ROWS, SWA_ROWS), 0)
    p_n = lax.broadcasted_iota(I32, (SWA_ROWS, SWA_ROWS), 1)
    d_n = s_n - p_n
    masks = []
    for window, dil in PATTERNS:
        assert dil & (dil - 1) == 0 and window % dil == 0
        mk = jnp.where(jnp.bitwise_and(d_k, dil - 1) == 0, d_k, window + 1) <= window
        mn = jnp.where(jnp.bitwise_and(d_n, dil - 1) == 0, jnp.where(d_n >= 0, d_n, window + 1), window + 1) <= window
        masks.append((mk, mn))
    for h in range(SWA_HEADS_PER_STEP):
        sl = slice(h * A_DH, (h + 1) * A_DH)
        q = q_ref[0, :, sl]
        kb = kb_ref[0, :, sl].astype(BF16)
        vb = vb_ref[0, :, sl].astype(BF16)
        kn = kn_ref[0, :, sl]
        vn = vn_ref[0, :, sl]
        s_cache = _dot_nt(q, kb) * scale
        s_new = _dot_nt(q, kn) * scale
        ps, lses = [], []
        for mk, mn in masks:
            sk = jnp.where(mk, s_cache, -jnp.inf)
            sn = jnp.where(mn, s_new, -jnp.inf)
            m = jnp.maximum(jnp.max(sk, axis=1, keepdims=True), jnp.max(sn, axis=1, keepdims=True))
            pk = jnp.exp(sk - m)
            pn = jnp.exp(sn - m)
            l = jnp.sum(pk, axis=1, keepdims=True) + jnp.sum(pn, axis=1, keepdims=True)
            ps.append((pk, pn, l))
            lses.append(m + jnp.log(l))
        top = functools.reduce(jnp.maximum, lses)
        es = [jnp.exp(x - top) for x in lses]
        tot = functools.reduce(lambda a, b: a + b, es)
        pk_all = None
        pn_all = None
        for (pk, pn, l), e in zip(ps, es):
            coef = e / (tot * l)
            pk_all = pk * coef if pk_all is None else pk_all + pk * coef
            pn_all = pn * coef if pn_all is None else pn_all + pn * coef
        o = _dot(pk_all.astype(BF16), vb) + _dot(pn_all.astype(BF16), vn)
        o_ref[0, :, sl] = o.astype(o_ref.dtype)


def _swa_sample(za_pad, kbuf, vbuf):
    B, wb = kbuf.shape[0], kbuf.shape[1]
    hw = SWA_HEADS_PER_STEP * A_DH
    ng = MIX_A // hw
    new_blk = (1, SWA_ROWS, hw)
    return pl.pallas_call(
        functools.partial(_swa_body, wb=wb),
        grid=(B, ng),
        in_specs=[pl.BlockSpec(new_blk, lambda b, g: (b, 0, g)),
                  pl.BlockSpec(new_blk, lambda b, g: (b, 0, ng + g)),
                  pl.BlockSpec(new_blk, lambda b, g: (b, 0, 2 * ng + g)),
                  pl.BlockSpec((1, wb, hw), lambda b, g: (b, 0, g)),
                  pl.BlockSpec((1, wb, hw), lambda b, g: (b, 0, g))],
        out_specs=pl.BlockSpec(new_blk, lambda b, g: (b, 0, g)),
        out_shape=jax.ShapeDtypeStruct((B, SWA_ROWS, MIX_A), BF16),
        compiler_params=_cparams(2),
        name="swa_sample",
    )(za_pad, za_pad, za_pad, kbuf, vbuf)


def _mix_body(*refs, n_pat):
    x_ref, hm_ref = refs[0], refs[1]
    n_lse = n_pat if n_pat > 1 else 0
    o_refs = refs[2:2 + n_pat]
    l_refs = refs[2 + n_pat:2 + n_pat + n_lse]
    wmo_ref, g_ref, b_ref, wq_ref, x1_ref, q_ref = refs[2 + n_pat + n_lse:]
    if n_pat == 1:
        ha = o_refs[0][...]
    else:
        cols = []
        for h in range(A_HEADS):
            sl = slice(h * A_DH, (h + 1) * A_DH)
            ls = [l_ref[:, h:h + 1] for l_ref in l_refs]
            top = functools.reduce(jnp.maximum, ls)
            es = [jnp.exp(x - top) for x in ls]
            inv = 1.0 / functools.reduce(lambda a, b: a + b, es)
            acc = None
            for o_ref, e in zip(o_refs, es):
                term = o_ref[:, sl].astype(F32) * (e * inv)
                acc = term if acc is None else acc + term
            cols.append(acc.astype(BF16))
        ha = jnp.concatenate(cols, axis=1)
    mix = _dot(hm_ref[...], wmo_ref[0:MIX_V, :]) + _dot(ha, wmo_ref[MIX_V:MIX_V + MIX_A, :])
    x1 = _layer_norm(DN_ALPHA * x_ref[...] + mix, g_ref[...], b_ref[...])
    x1_ref[...] = x1
    q_ref[...] = _dot(x1.astype(BF16), wq_ref[...]).astype(q_ref.dtype)


def _mix_ln1_q(x, hm, outs, lses, wmo, ln_g, ln_b, wq, tm):
    M = x.shape[0]
    n_pat = len(outs)
    assert len(lses) == (n_pat if n_pat > 1 else 0)
    row = lambda w: pl.BlockSpec((tm, w), lambda i: (i, 0))
    return pl.pallas_call(
        functools.partial(_mix_body, n_pat=n_pat),
        grid=(M // tm,),
        in_specs=[row(D_MODEL), row(MIX_V)] + [row(MIX_A)] * n_pat + [row(LANES)] * len(lses)
        + [_resident(wmo.shape), _resident(ln_g.shape), _resident(ln_b.shape), _resident(wq.shape)],
        out_specs=[row(D_MODEL), row(D_MODEL)],
        out_shape=[jax.ShapeDtypeStruct((M, D_MODEL), F32), jax.ShapeDtypeStruct((M, D_MODEL), BF16)],
        compiler_params=_cparams(1),
        name="mix_ln1_q",
    )(x, hm, *outs, *lses, wmo, ln_g, ln_b, wq)


def _xattn_body(q_ref, k_ref, v_ref, o_ref):
    scale = X_DH ** -0.5
    for h in range(X_HEADS):
        sl = slice(h * X_DH, (h + 1) * X_DH)
        s = _dot_nt(q_ref[0, :, sl], k_ref[0, :, sl].astype(BF16)) * scale
        p = jnp.exp(s - jnp.max(s, axis=1, keepdims=True))
        p = p / jnp.sum(p, axis=1, keepdims=True)
        o_ref[0, :, sl] = _dot(p.astype(BF16), v_ref[0, :, sl].astype(BF16)).astype(o_ref.dtype)


def _xattn(q, mk, mv, tq):
    B, Tq, _ = q.shape
    nm = mk.shape[1]
    return pl.pallas_call(
        _xattn_body,
        grid=(B, Tq // tq),
        in_specs=[pl.BlockSpec((1, tq, D_MODEL), lambda b, i: (b, i, 0)),
                  pl.BlockSpec((1, nm, D_MODEL), lambda b, i: (b, 0, 0)),
                  pl.BlockSpec((1, nm, D_MODEL), lambda b, i: (b, 0, 0))],
        out_specs=pl.BlockSpec((1, tq, D_MODEL), lambda b, i: (b, i, 0)),
        out_shape=jax.ShapeDtypeStruct((B, Tq, D_MODEL), BF16),
        compiler_params=_cparams(2),
        name="xattn",
    )(q, mk, mv)


def _xo_body(x1_ref, o_ref, wo_ref, g_ref, b_ref, wr_ref, br_ref, x2_ref, ti_ref, tg_ref):
    y = _dot(o_ref[...], wo_ref[...])
    x2 = _layer_norm(DN_ALPHA * x1_ref[...] + y, g_ref[...], b_ref[...])
    x2_ref[...] = x2
    tm = x2.shape[0]
    lane = lax.broadcasted_iota(I32, (tm, LANES), 1)
    lanef = lane.astype(F32)
    logits = jnp.where(lane < N_EXPERTS, _dot(x2.astype(BF16), wr_ref[...]) + br_ref[...], -jnp.inf)
    vals, idxs = [], []
    cur = logits
    for _ in range(TOP_K):
        top = jnp.max(cur, axis=1, keepdims=True)
        idx = jnp.min(jnp.where(cur == top, lanef, float(LANES)), axis=1, keepdims=True)
        vals.append(top)
        idxs.append(idx)
        cur = jnp.where(lanef == idx, -jnp.inf, cur)
    es = [jnp.exp(v - vals[0]) for v in vals]
    inv = 1.0 / functools.reduce(lambda a, b: a + b, es)
    ti = jnp.zeros((tm, LANES), F32)
    tg = jnp.zeros((tm, LANES), F32)
    for k in range(TOP_K):
        ti = jnp.where(lane == k, idxs[k], ti)
        tg = jnp.where(lane == k, es[k] * inv, tg)
    ti_ref[...] = ti.astype(I32)
    tg_ref[...] = tg


def _xo_ln2_router(x1, o, wo, ln_g, ln_b, wr, br, tm):
    M = x1.shape[0]
    row = lambda w: pl.BlockSpec((tm, w), lambda i: (i, 0))
    return pl.pallas_call(
        _xo_body,
        grid=(M // tm,),
        in_specs=[row(D_MODEL), row(D_MODEL), _resident(wo.shape), _resident(ln_g.shape), _resident(ln_b.shape),
                  _resident(wr.shape), _resident(br.shape)],
        out_specs=[row(D_MODEL), row(LANES), row(LANES)],
        out_shape=[jax.ShapeDtypeStruct((M, D_MODEL), F32), jax.ShapeDtypeStruct((M, LANES), I32),
                   jax.ShapeDtypeStruct((M, LANES), F32)],
        compiler_params=_cparams(1),
        name="xo_ln2_router",
    )(x1, o, wo, ln_g, ln_b, wr, br)


GATHER_ROWS = 256
MOE_FF_CHUNK = 256


def _gather_body(valid_ref, idx_ref, x_hbm, o_hbm, sem):
    i = pl.program_id(0)

    def row_copy(src_row, dst_row):
        return pltpu.make_async_copy(x_hbm.at[pl.ds(src_row, 1), :], o_hbm.at[pl.ds(dst_row, 1), :], sem)

    @pl.when(valid_ref[i] > 0)
    def _():
        def issue(r, carry):
            row_copy(idx_ref[0, 0, r], i * GATHER_ROWS + r).start()
            return carry
        lax.fori_loop(0, GATHER_ROWS, issue, 0, unroll=8)

        def drain(r, carry):
            row_copy(0, 0).wait()
            return carry
        lax.fori_loop(0, GATHER_ROWS, drain, 0, unroll=8)


def _gather_rows(x, row_tok, blk_valid):
    P = row_tok.shape[0]
    nblk = P // GATHER_ROWS
    return pl.pallas_call(
        _gather_body,
        grid_spec=pltpu.PrefetchScalarGridSpec(
            num_scalar_prefetch=1,
            grid=(nblk,),
            in_specs=[pl.BlockSpec((1, 1, GATHER_ROWS), lambda i, v: (i, 0, 0), memory_space=pltpu.SMEM),
                      pl.BlockSpec(memory_space=pl.ANY)],
            out_specs=pl.BlockSpec(memory_space=pl.ANY),
            scratch_shapes=[pltpu.SemaphoreType.DMA(())]),
        out_shape=jax.ShapeDtypeStruct((P, x.shape[1]), x.dtype),
        compiler_params=_cparams(1),
        name="moe_dispatch",
    )(blk_valid, row_tok.reshape(nblk, 1, GATHER_ROWS), x)


def _ffn_body(te_ref, tr_ref, nu_ref, xs_ref, wg_ref, bg_ref, wu_ref, bu_ref, wd_ref, bd_ref, o_ref,
              wgb, wub, wdb, *, tm, sb):
    t = pl.program_id(0)
    c = pl.program_id(1)

    @pl.when(t < nu_ref[0])
    def _():
        wgb[...] = wg_ref[0].astype(BF16)
        wub[...] = wu_ref[0].astype(BF16)
        wdb[...] = wd_ref[0].astype(BF16)
        rows = tr_ref[t]
        for s in range(tm // sb):
            @pl.when(s * sb < rows)
            def _():
                rs = slice(s * sb, (s + 1) * sb)

                @pl.when(c == 0)
                def _():
                    o_ref[rs, :] = jnp.broadcast_to(bd_ref[0], (sb, D_MODEL))

                xb = xs_ref[rs, :].astype(BF16)
                g = jnp.minimum(_dot(xb, wgb[...]) + bg_ref[0], SWIGLU_LIMIT)
                u = jnp.clip(_dot(xb, wub[...]) + bu_ref[0], -SWIGLU_LIMIT, SWIGLU_LIMIT)
                hid = (u + 1.0) * (g * _sigmoid(SWIGLU_ALPHA * g))
                o_ref[rs, :] += _dot(hid.astype(BF16), wdb[...])


def _moe_ffn(xs, tile_e, tile_rows, n_used, wg, bg, wu, bu, wd, bd, tm, sb):
    P = xs.shape[0]
    n_tiles = P // tm
    nch = D_FF // MOE_FF_CHUNK
    fc = MOE_FF_CHUNK

    def tile(t, nu):
        return jnp.minimum(t, jnp.maximum(nu[0] - 1, 0))

    def chunk(t, c, nu):
        return jnp.where(t < nu[0], c, nch - 1)

    return pl.pallas_call(
        functools.partial(_ffn_body, tm=tm, sb=sb),
        grid_spec=pltpu.PrefetchScalarGridSpec(
            num_scalar_prefetch=3,
            grid=(n_tiles, nch),
            in_specs=[
                pl.BlockSpec((tm, D_MODEL), lambda t, c, te, tr, nu: (tile(t, nu), 0)),
                pl.BlockSpec((1, D_MODEL, fc), lambda t, c, te, tr, nu: (te[tile(t, nu)], 0, chunk(t, c, nu))),
                pl.BlockSpec((1, 1, fc), lambda t, c, te, tr, nu: (te[tile(t, nu)], 0, chunk(t, c, nu))),
                pl.BlockSpec((1, D_MODEL, fc), lambda t, c, te, tr, nu: (te[tile(t, nu)], 0, chunk(t, c, nu))),
                pl.BlockSpec((1, 1, fc), lambda t, c, te, tr, nu: (te[tile(t, nu)], 0, chunk(t, c, nu))),
                pl.BlockSpec((1, fc, D_MODEL), lambda t, c, te, tr, nu: (te[tile(t, nu)], chunk(t, c, nu), 0)),
                pl.BlockSpec((1, 1, D_MODEL), lambda t, c, te, tr, nu: (te[tile(t, nu)], 0, 0)),
            ],
            out_specs=pl.BlockSpec((tm, D_MODEL), lambda t, c, te, tr, nu: (tile(t, nu), 0)),
            scratch_shapes=[pltpu.VMEM((D_MODEL, fc), BF16), pltpu.VMEM((D_MODEL, fc), BF16),
                            pltpu.VMEM((fc, D_MODEL), BF16)]),
        out_shape=jax.ShapeDtypeStruct((P, D_MODEL), F32),
        compiler_params=_cparams(2),
        name="moe_ffn",
    )(tile_e, tile_rows, n_used, xs, wg, bg, wu, bu, wd, bd)


def _combine_body(pos_ref, x2_ref, tg_ref, ys_hbm, g_ref, b_ref, o_ref, buf, sem, *, tc):
    def row_copy(src_row, k, j):
        return pltpu.make_async_copy(ys_hbm.at[pl.ds(src_row, 1), :], buf.at[k, pl.ds(j, 1), :], sem)

    def issue(j, carry):
        for k in range(TOP_K):
            row_copy(pos_ref[0, 0, j * TOP_K + k], k, j).start()
        return carry
    lax.fori_loop(0, tc, issue, 0, unroll=4)

    def drain(j, carry):
        for k in range(TOP_K):
            row_copy(0, k, 0).wait()
        return carry
    lax.fori_loop(0, tc, drain, 0, unroll=4)

    y = None
    for k in range(TOP_K):
        term = buf[k] * tg_ref[:, k:k + 1]
        y = term if y is None else y + term
    o_ref[...] = _layer_norm(DN_ALPHA * x2_ref[...] + y, g_ref[...], b_ref[...])


def _combine_ln3(x2, tg, ys, pos, ln_g, ln_b, tc):
    M = x2.shape[0]
    nblk = M // tc
    return pl.pallas_call(
        functools.partial(_combine_body, tc=tc),
        grid=(nblk,),
        in_specs=[pl.BlockSpec((1, 1, tc * TOP_K), lambda i: (i, 0, 0), memory_space=pltpu.SMEM),
                  pl.BlockSpec((tc, D_MODEL), lambda i: (i, 0)),
                  pl.BlockSpec((tc, LANES), lambda i: (i, 0)),
                  pl.BlockSpec(memory_space=pl.ANY),
                  _resident(ln_g.shape), _resident(ln_b.shape)],
        out_specs=pl.BlockSpec((tc, D_MODEL), lambda i: (i, 0)),
        out_shape=jax.ShapeDtypeStruct((M, D_MODEL), F32),
        scratch_shapes=[pltpu.VMEM((TOP_K, tc, D_MODEL), F32), pltpu.SemaphoreType.DMA(())],
        compiler_params=_cparams(1),
        name="moe_combine",
    )(pos.reshape(nblk, 1, tc * TOP_K), x2, tg, ys, ln_g, ln_b)


def _moe(x2, topi, gates, wg, bg, wu, bu, wd, bd, ln_g, ln_b, tm, tc):
    ntok = x2.shape[0]
    nk = ntok * TOP_K
    sb = min(tm, GATHER_ROWS)
    per_blk = max(GATHER_ROWS // tm, 1)
    n_tiles = -(-(nk + N_EXPERTS * (tm - 1)) // (tm * per_blk)) * per_blk
    e_flat = topi[:, :TOP_K].reshape(nk)
    onehot = (e_flat[:, None] == jnp.arange(N_EXPERTS, dtype=I32)[None, :]).astype(I32)
    csum = jnp.cumsum(onehot, axis=0)
    counts = csum[-1]
    rank = jnp.take_along_axis(csum, e_flat[:, None], axis=1)[:, 0] - 1
    tiles_e = (counts + tm - 1) // tm
    tile_end = jnp.cumsum(tiles_e)
    tile_start = tile_end - tiles_e
    dest = tile_start[e_flat] * tm + rank
    n_used = tile_end[-1:].astype(I32)
    t_ids = jnp.arange(n_tiles, dtype=I32)
    tile_e = jnp.minimum(jnp.searchsorted(tile_end, t_ids, side="right"), N_EXPERTS - 1).astype(I32)
    tile_rows = jnp.clip(counts[tile_e] - (t_ids - tile_start[tile_e]) * tm, 0, tm)
    tile_rows = jnp.where(t_ids < n_used[0], tile_rows, 0).astype(I32)
    P = n_tiles * tm
    row_tok = jnp.zeros((P,), I32).at[dest].set(jnp.arange(nk, dtype=I32) // TOP_K)
    sub_ids = jnp.arange(P // sb, dtype=I32)
    sub_valid = ((sub_ids % (tm // sb)) * sb < tile_rows[sub_ids // (tm // sb)]).astype(I32)
    blk_valid = sub_valid.reshape(P // GATHER_ROWS, GATHER_ROWS // sb).max(axis=1)
    xs = _gather_rows(x2, row_tok, blk_valid)
    ys = _moe_ffn(xs, tile_e, tile_rows, n_used, wg, bg, wu, bu, wd, bd, tm, sb)
    return _combine_ln3(x2, gates, ys, dest, ln_g, ln_b, tc)


def _finish(x, hm, outs, lses, mem_k, mem_v, batch, w, tm_mix, tq, tm_xo, moe_tm, tc):
    M = x.shape[0]
    x1, q = _mix_ln1_q(x, hm, outs, lses, w["wmo"], w["ln1_g"], w["ln1_b"], w["wxq"], tm_mix)
    o = _xattn(q.reshape(batch, M // batch, D_MODEL), mem_k, mem_v, tq).reshape(M, D_MODEL)
    x2, topi, gates = _xo_ln2_router(x1, o, w["wxo"], w["ln2_g"], w["ln2_b"], w["wr"], w["br"], tm_xo)
    return _moe(x2, topi, gates, w["wg"], w["bg"], w["wu"], w["bu"], w["wd"], w["bd"], w["ln3_g"], w["ln3_b"],
                moe_tm, tc)


def kernel(x_prompt, x_sample, mem_prompt, state_mlstm_C, state_mlstm_n, state_mlstm_m, cache_swa_k, cache_swa_v, cache_mem_k, cache_mem_v, w_in, b_igate, b_fgate, g_mnorm, w_mix_out, ln1_g, ln1_b, w_xq, w_xk, w_xv, w_xo, ln2_g, ln2_b, w_router, b_router, w_gate, b_gate, w_up, b_up, w_down, b_down, ln3_g, ln3_b):
    assert DEPTH == 1
    B, T, _ = x_prompt.shape
    DB, S, _ = x_sample.shape
    assert B == 1
    n_mem = mem_prompt.shape[1]
    wb = cache_swa_k.shape[2]
    row2 = lambda a: a[0].reshape(1, -1)

    wi = w_in[0]
    c0 = 2 * MIX_QK + 2 * MIX_V
    w_m = wi[:, :c0].astype(BF16)
    w_g = jnp.pad(wi[:, c0:c0 + 2 * M_HEADS], ((0, 0), (0, LANES - 2 * M_HEADS))).astype(BF16)
    w_a = wi[:, c0 + 2 * M_HEADS:].astype(BF16)
    gate_bias = jnp.pad(jnp.concatenate([b_igate[0], b_fgate[0]]), (0, LANES - 2 * M_HEADS)).reshape(1, LANES)
    w = dict(
        wmo=w_mix_out[0].astype(BF16), ln1_g=row2(ln1_g), ln1_b=row2(ln1_b), wxq=w_xq[0].astype(BF16),
        wxo=w_xo[0].astype(BF16), ln2_g=row2(ln2_g), ln2_b=row2(ln2_b),
        wr=jnp.pad(w_router[0], ((0, 0), (0, LANES - N_EXPERTS))).astype(BF16),
        br=jnp.pad(b_router[0], (0, LANES - N_EXPERTS)).reshape(1, LANES),
        wg=w_gate[0], bg=b_gate[0].reshape(N_EXPERTS, 1, D_FF), wu=w_up[0], bu=b_up[0].reshape(N_EXPERTS, 1, D_FF),
        wd=w_down[0], bd=b_down[0].reshape(N_EXPERTS, 1, D_MODEL), ln3_g=row2(ln3_g), ln3_b=row2(ln3_b))
    gm = g_mnorm[0].reshape(1, MIX_V)

    xp = x_prompt.reshape(T, D_MODEL)
    zm = _mm(xp, w_m, BF16, 512, 1024, "proj_mlstm")
    zg = _mm(xp, w_g, F32, 512, LANES, "proj_gates")
    za = _mm(xp, w_a, BF16, 512, 1024, "proj_attn")
    wbp = min(wb, T)
    kv_tail = _mm(xp[T - wbp:], w_a[:, MIX_A:], F32, 512, 1024, "proj_kv_tail")
    zeros_c = jnp.zeros((1, M_HEADS, M_DQK, M_DV), F32)
    zeros_n = jnp.zeros((1, M_HEADS, M_DQK), F32)
    zeros_m = jnp.zeros((1, M_HEADS, LANES), F32)
    hm, pC, pn, pm = _mlstm(zm, zg, gate_bias, gm, zeros_c, zeros_n, zeros_m, 1, T // 128)
    pats = [_dilated_pattern(za, window, dil) for window, dil in PATTERNS]
    mp = mem_prompt.reshape(n_mem, D_MODEL)
    mem_k = _mm(mp, w_xk[0].astype(BF16), F32, n_mem, 1024, "mem_k")
    mem_v = _mm(mp, w_xv[0].astype(BF16), F32, n_mem, 1024, "mem_v")
    yp = _finish(xp, hm, [p[0] for p in pats], [p[1] for p in pats],
                 mem_k.astype(BF16).reshape(1, n_mem, D_MODEL), mem_v.astype(BF16).reshape(1, n_mem, D_MODEL),
                 1, w, tm_mix=256, tq=512, tm_xo=256, moe_tm=1024, tc=128)

    ns = DB * S
    xs_ = x_sample.reshape(ns, D_MODEL)
    zm_s = _mm(xs_, w_m, BF16, ns, 1024, "proj_mlstm_s")
    zg_s = _mm(xs_, w_g, F32, ns, LANES, "proj_gates_s")
    za_s = _mm(xs_, w_a, F32, ns, 1024, "proj_attn_s")
    pad_rows = 128 - S
    zm_pad = jnp.pad(zm_s.reshape(DB, S, -1), ((0, 0), (0, pad_rows), (0, 0))).reshape(DB * 128, -1)
    lane = jnp.arange(LANES)
    neutral = jnp.where(lane < M_HEADS, NEG_BIG, jnp.where(lane < 2 * M_HEADS, -NEG_BIG, 0.0)).astype(F32)
    zg_pad = jnp.concatenate([zg_s.reshape(DB, S, LANES), jnp.broadcast_to(neutral, (DB, pad_rows, LANES))],
                             axis=1).reshape(DB * 128, LANES)
    m0 = jnp.broadcast_to(state_mlstm_m[0][:, :, None], (DB, M_HEADS, LANES))
    hm_s, sC, sn, sm = _mlstm(zm_pad, zg_pad, gate_bias, gm, state_mlstm_C[0], state_mlstm_n[0], m0, DB, 1)
    hm_s = hm_s.reshape(DB, 128, MIX_V)[:, :S].reshape(ns, MIX_V)
    za_pad = jnp.pad(za_s.reshape(DB, S, -1), ((0, 0), (0, SWA_ROWS - S), (0, 0))).astype(BF16)
    ha_s = _swa_sample(za_pad, cache_swa_k[0].reshape(DB, wb, MIX_A), cache_swa_v[0].reshape(DB, wb, MIX_A))
    ha_s = ha_s[:, :S].reshape(ns, MIX_A)
    x1_s, q_s = _mix_ln1_q(xs_, hm_s, [ha_s], [], w["wmo"], w["ln1_g"], w["ln1_b"], w["wxq"], ns)
    q_pad = jnp.pad(q_s.reshape(DB, S, D_MODEL), ((0, 0), (0, SWA_ROWS - S), (0, 0)))
    o_s = _xattn(q_pad, cache_mem_k[0].reshape(DB, n_mem, D_MODEL), cache_mem_v[0].reshape(DB, n_mem, D_MODEL),
                 SWA_ROWS)[:, :S].reshape(ns, D_MODEL)
    x2_s, topi_s, gates_s = _xo_ln2_router(x1_s, o_s, w["wxo"], w["ln2_g"], w["ln2_b"], w["wr"], w["br"], ns)
    ys_ = _moe(x2_s, topi_s, gates_s, w["wg"], w["bg"], w["wu"], w["bu"], w["wd"], w["bd"], w["ln3_g"], w["ln3_b"],
               128, ns)

    return (yp.reshape(B, T, D_MODEL), ys_.reshape(DB, S, D_MODEL),
            pC[None], pn[None], pm[:, :, 0][None],
            kv_tail[:, :MIX_A].reshape(1, B, wbp, A_HEADS, A_DH), kv_tail[:, MIX_A:].reshape(1, B, wbp, A_HEADS, A_DH),
            mem_k.reshape(1, B, n_mem, X_HEADS, X_DH), mem_v.reshape(1, B, n_mem, X_HEADS, X_DH),
            sC[None], sn[None], sm[:, :, 0][None],
            za_s[:, MIX_A:2 * MIX_A].reshape(1, DB, S, A_HEADS, A_DH),
            za_s[:, 2 * MIX_A:].reshape(1, DB, S, A_HEADS, A_DH))
```

```python
import functools

import jax
import jax.numpy as jnp
from jax import lax
from jax.experimental import pallas as pl
from jax.experimental.pallas import tpu as pltpu

F32, BF16, I32 = jnp.float32, jnp.bfloat16, jnp.int32

D_MODEL = 2048
DEPTH = 1
M_HEADS, M_DQK, M_DV = 4, 128, 256
HEAD_EPS = 1e-6
A_HEADS, A_DH = 8, 128
PATTERNS = ((128, 1), (512, 4), (2048, 16))
BAND_BLOCK = 128
X_HEADS = 4
X_DH = D_MODEL // X_HEADS
N_EXPERTS, TOP_K, D_FF = 32, 4, 2048
SWIGLU_LIMIT, SWIGLU_ALPHA = 7.0, 1.702
DN_ALPHA = (2 * DEPTH) ** 0.25
LN_EPS = 1e-5
MIX_V = M_HEADS * M_DV
MIX_A = A_HEADS * A_DH
MIX_QK = M_HEADS * M_DQK

LANES = 128
VMEM_LIMIT_BYTES = 58 * 1024 * 1024

NEG_BIG = -1e30


def _cparams(n_axes, vmem=VMEM_LIMIT_BYTES):
    return pltpu.CompilerParams(dimension_semantics=("arbitrary",) * n_axes, vmem_limit_bytes=vmem)


def _dot(a, b):
    return jnp.dot(a, b, preferred_element_type=F32)


def _dot_nt(a, b):
    return lax.dot_general(a, b, (((1,), (1,)), ((), ())), preferred_element_type=F32)


def _dot_tn(a, b):
    return lax.dot_general(a, b, (((0,), (0,)), ((), ())), preferred_element_type=F32)


def _log_sigmoid(x):
    return jnp.minimum(x, 0.0) - jnp.log(1.0 + jnp.exp(-jnp.abs(x)))


def _sigmoid(x):
    return 1.0 / (1.0 + jnp.exp(-x))


def _layer_norm(v, g, b):
    mu = jnp.mean(v, axis=-1, keepdims=True)
    d = v - mu
    var = jnp.mean(d * d, axis=-1, keepdims=True)
    return d * lax.rsqrt(var + LN_EPS) * g + b


def _resident(shape):
    nd = len(shape)
    return pl.BlockSpec(shape, lambda *_: (0,) * nd, pipeline_mode=pl.Buffered(1))


def _mm_body(x_ref, w_ref, o_ref):
    o_ref[...] = _dot(x_ref[...].astype(BF16), w_ref[...]).astype(o_ref.dtype)


def _mm(x, w, out_dtype, tm, tn, name):
    M, K = x.shape
    N = w.shape[1]
    assert M % tm == 0 and N % tn == 0
    return pl.pallas_call(
        _mm_body,
        grid=(M // tm, N // tn),
        in_specs=[pl.BlockSpec((tm, K), lambda i, j: (i, 0)), pl.BlockSpec((K, tn), lambda i, j: (0, j))],
        out_specs=pl.BlockSpec((tm, tn), lambda i, j: (i, j)),
        out_shape=jax.ShapeDtypeStruct((M, N), out_dtype),
        compiler_params=_cparams(2),
        name=name,
    )(x, w)


def _mlstm_body(q_ref, k_ref, v_ref, zo_ref, zg_ref, gb_ref, gm_ref, c0_ref, n0_ref, m0_ref,
                hm_ref, c_ref, n_ref, m_ref, *, L):
    @pl.when(pl.program_id(1) == 0)
    def _():
        c_ref[...] = c0_ref[...]
        n_ref[...] = n0_ref[...]
        m_ref[...] = m0_ref[...]

    scale = M_DQK ** -0.5
    g = zg_ref[...] + gb_ref[...]
    gt = g.T
    row = lax.broadcasted_iota(I32, (L, L), 0)
    col = lax.broadcasted_iota(I32, (L, L), 1)
    tri = row >= col
    for h in range(M_HEADS):
        i_col = g[:, h:h + 1]
        f_col = _log_sigmoid(g[:, M_HEADS + h:M_HEADS + h + 1])
        i_row = gt[h:h + 1, :]
        f_row = _log_sigmoid(gt[M_HEADS + h:M_HEADS + h + 1, :])
        q = q_ref[:, h * M_DQK:(h + 1) * M_DQK]
        k = k_ref[:, h * M_DQK:(h + 1) * M_DQK]
        v = v_ref[:, h * M_DV:(h + 1) * M_DV]
        zo = zo_ref[:, h * M_DV:(h + 1) * M_DV].astype(F32)
        C = c_ref[0, h]
        n = n_ref[0, h:h + 1, :]
        m = m_ref[0, h:h + 1, 0:1]
        b_col = jnp.sum(jnp.where(tri, f_row, 0.0), axis=1, keepdims=True)
        b_row = jnp.sum(jnp.where(row <= col, f_col, 0.0), axis=0, keepdims=True)
        logd = jnp.where(tri, b_col - b_row + i_row, -jnp.inf)
        inter = b_col + m
        mt = jnp.maximum(inter, jnp.max(logd, axis=1, keepdims=True))
        sd = _dot_nt(q, k) * scale * jnp.exp(logd - mt)
        sc = jnp.exp(inter - mt)
        num = _dot(sd.astype(BF16), v) + sc * _dot(q, C.astype(BF16))
        den = jnp.sum(sd, axis=1, keepdims=True) + sc * jnp.sum(q.astype(F32) * n, axis=1, keepdims=True)
        hh = num / jnp.maximum(jnp.abs(den), jnp.exp(-mt))
        hn = hh * lax.rsqrt(jnp.mean(hh * hh, axis=1, keepdims=True) + HEAD_EPS)
        out = hn * gm_ref[:, h * M_DV:(h + 1) * M_DV] * _sigmoid(zo)
        hm_ref[:, h * M_DV:(h + 1) * M_DV] = out.astype(hm_ref.dtype)
        bl = b_col[L - 1:L, :]
        ml = mt[L - 1:L, :]
        w_col = jnp.exp(bl - b_col + i_col - ml)
        scl = jnp.exp(bl + m - ml)
        kw = k.astype(F32) * (w_col * scale)
        c_ref[0, h] = scl * C + _dot_tn(kw.astype(BF16), v)
        n_ref[0, h:h + 1, :] = scl * n + jnp.sum(kw, axis=0, keepdims=True)
        m_ref[0, h:h + 1, :] = jnp.broadcast_to(ml, (1, LANES))


def _mlstm(zm, zg, gate_bias, g_mnorm, c0, n0, m0, B, nc, L=128):
    T = B * nc * L
    assert zm.shape == (T, 2 * MIX_QK + 2 * MIX_V)
    row_blk = lambda b, c: b * nc + c
    state_specs = [pl.BlockSpec((1, M_HEADS, M_DQK, M_DV), lambda b, c: (b, 0, 0, 0)),
                   pl.BlockSpec((1, M_HEADS, M_DQK), lambda b, c: (b, 0, 0)),
                   pl.BlockSpec((1, M_HEADS, LANES), lambda b, c: (b, 0, 0))]
    return pl.pallas_call(
        functools.partial(_mlstm_body, L=L),
        grid=(B, nc),
        in_specs=[pl.BlockSpec((L, MIX_QK), lambda b, c: (row_blk(b, c), 0)),
                  pl.BlockSpec((L, MIX_QK), lambda b, c: (row_blk(b, c), 1)),
                  pl.BlockSpec((L, MIX_V), lambda b, c: (row_blk(b, c), 1)),
                  pl.BlockSpec((L, MIX_V), lambda b, c: (row_blk(b, c), 2)),
                  pl.BlockSpec((L, LANES), lambda b, c: (row_blk(b, c), 0)),
                  pl.BlockSpec((1, LANES), lambda b, c: (0, 0)),
                  pl.BlockSpec((1, MIX_V), lambda b, c: (0, 0))] + state_specs,
        out_specs=[pl.BlockSpec((L, MIX_V), lambda b, c: (row_blk(b, c), 0))] + state_specs,
        out_shape=[jax.ShapeDtypeStruct((T, MIX_V), BF16),
                   jax.ShapeDtypeStruct((B, M_HEADS, M_DQK, M_DV), F32),
                   jax.ShapeDtypeStruct((B, M_HEADS, M_DQK), F32),
                   jax.ShapeDtypeStruct((B, M_HEADS, LANES), F32)],
        compiler_params=_cparams(2),
        name="mlstm",
    )(zm, zm, zm, zm, zg, gate_bias, g_mnorm, c0, n0, m0)


def _dil_body(q_ref, kc_ref, kp_ref, vc_ref, vp_ref, o_ref, l_ref, *, span):
    scale = A_DH ** -0.5
    n = pl.program_id(1)
    row = lax.broadcasted_iota(I32, (BAND_BLOCK, BAND_BLOCK), 0)
    col = lax.broadcasted_iota(I32, (BAND_BLOCK, BAND_BLOCK), 1)
    first = jnp.where(n > 0, 0, 2 * BAND_BLOCK)
    mask_p = (BAND_BLOCK + row - col + first) <= span
    mask_c = row >= col
    lane = lax.broadcasted_iota(I32, (BAND_BLOCK, LANES), 1)
    lse_tile = jnp.zeros((BAND_BLOCK, LANES), F32)
    for h in range(A_HEADS):
        sl = slice(h * A_DH, (h + 1) * A_DH)
        q = q_ref[:, sl]
        sp = jnp.where(mask_p, _dot_nt(q, kp_ref[:, sl]) * scale, -jnp.inf)
        sc = jnp.where(mask_c, _dot_nt(q, kc_ref[:, sl]) * scale, -jnp.inf)
        m = jnp.maximum(jnp.max(sp, axis=1, keepdims=True), jnp.max(sc, axis=1, keepdims=True))
        pp = jnp.exp(sp - m)
        pc = jnp.exp(sc - m)
        l = jnp.sum(pp, axis=1, keepdims=True) + jnp.sum(pc, axis=1, keepdims=True)
        o = (_dot(pp.astype(BF16), vp_ref[:, sl]) + _dot(pc.astype(BF16), vc_ref[:, sl])) / l
        o_ref[:, sl] = o.astype(o_ref.dtype)
        lse_tile = jnp.where(lane == h, m + jnp.log(l), lse_tile)
    l_ref[...] = lse_tile


def _dilated_pattern(za, window, dil):
    T = za.shape[0]
    span = window // dil
    assert T % (dil * BAND_BLOCK) == 0 and BAND_BLOCK - 1 <= span
    N = T // dil
    nb = N // BAND_BLOCK
    zv = za.reshape(N, dil * 3 * MIX_A)
    prev = lambda n: jnp.maximum(n - 1, 0)
    blk = (BAND_BLOCK, MIX_A)
    o, lse = pl.pallas_call(
        functools.partial(_dil_body, span=span),
        grid=(dil, nb),
        in_specs=[pl.BlockSpec(blk, lambda r, n: (n, 3 * r)),
                  pl.BlockSpec(blk, lambda r, n: (n, 3 * r + 1)),
                  pl.BlockSpec(blk, lambda r, n: (prev(n), 3 * r + 1)),
                  pl.BlockSpec(blk, lambda r, n: (n, 3 * r + 2)),
                  pl.BlockSpec(blk, lambda r, n: (prev(n), 3 * r + 2))],
        out_specs=[pl.BlockSpec(blk, lambda r, n: (n, r)),
                   pl.BlockSpec((BAND_BLOCK, LANES), lambda r, n: (n, r))],
        out_shape=[jax.ShapeDtypeStruct((N, dil * MIX_A), BF16),
                   jax.ShapeDtypeStruct((N, dil * LANES), F32)],
        compiler_params=_cparams(2),
        name=f"dilated_d{dil}",
    )(zv, zv, zv, zv, zv)
    return o.reshape(T, MIX_A), lse.reshape(T, LANES)


SWA_ROWS = 16


def _swa_body(q_ref, kn_ref, vn_ref, kb_ref, vb_ref, o_ref, *, wb):
    scale = A_DH ** -0.5
    s_k = lax.broadcasted_iota(I32, (SWA_ROWS, wb), 0)
    p_k = lax.broadcasted_iota(I32, (SWA_ROWS, wb), 1)
    d_k = wb + s_k - p_k
    s_n = lax.broadcasted_iota(I32, (SWA_ROWS, SWA_ROWS), 0)
    p_n = lax.broadcasted_iota(I32, (SWA_ROWS, SWA_ROWS), 1)
    d_n = s_n - p_n
    masks = []
    for window, dil in PATTERNS:
        assert dil & (dil - 1) == 0 and window % dil == 0
        mk = jnp.where(jnp.bitwise_and(d_k, dil - 1) == 0, d_k, window + 1) <= window
        mn = jnp.where(jnp.bitwise_and(d_n, dil - 1) == 0, jnp.where(d_n >= 0, d_n, window + 1), window + 1) <= window
        masks.append((mk, mn))
    for h in range(A_HEADS):
        sl = slice(h * A_DH, (h + 1) * A_DH)
        q = q_ref[0, :, sl]
        kb = kb_ref[:, h, :].astype(BF16)
        vb = vb_ref[:, h, :].astype(BF16)
        kn = kn_ref[0, :, sl]
        vn = vn_ref[0, :, sl]
        s_cache = _dot_nt(q, kb) * scale
        s_new = _dot_nt(q, kn) * scale
        ps, lses = [], []
        for mk, mn in masks:
            sk = jnp.where(mk, s_cache, -jnp.inf)
            sn = jnp.where(mn, s_new, -jnp.inf)
            m = jnp.maximum(jnp.max(sk, axis=1, keepdims=True), jnp.max(sn, axis=1, keepdims=True))
            pk = jnp.exp(sk - m)
            pn = jnp.exp(sn - m)
            l = jnp.sum(pk, axis=1, keepdims=True) + jnp.sum(pn, axis=1, keepdims=True)
            ps.append((pk, pn, l))
            lses.append(m + jnp.log(l))
        top = functools.reduce(jnp.maximum, lses)
        es = [jnp.exp(x - top) for x in lses]
        tot = functools.reduce(lambda a, b: a + b, es)
        pk_all = None
        pn_all = None
        for (pk, pn, l), e in zip(ps, es):
            coef = e / (tot * l)
            pk_all = pk * coef if pk_all is None else pk_all + pk * coef
            pn_all = pn * coef if pn_all is None else pn_all + pn * coef
        o = _dot(pk_all.astype(BF16), vb) + _dot(pn_all.astype(BF16), vn)
        o_ref[0, :, sl] = o.astype(o_ref.dtype)


def _swa_sample(za_pad, kbuf, vbuf):
    _, B, wb = kbuf.shape[:3]
    new_blk = (1, SWA_ROWS, MIX_A)
    cache_blk = (None, None, wb, A_HEADS, A_DH)
    return pl.pallas_call(
        functools.partial(_swa_body, wb=wb),
        grid=(B,),
        in_specs=[pl.BlockSpec(new_blk, lambda b: (b, 0, 0)),
                  pl.BlockSpec(new_blk, lambda b: (b, 0, 1)),
                  pl.BlockSpec(new_blk, lambda b: (b, 0, 2)),
                  pl.BlockSpec(cache_blk, lambda b: (0, b, 0, 0, 0)),
                  pl.BlockSpec(cache_blk, lambda b: (0, b, 0, 0, 0))],
        out_specs=pl.BlockSpec(new_blk, lambda b: (b, 0, 0)),
        out_shape=jax.ShapeDtypeStruct((B, SWA_ROWS, MIX_A), BF16),
        compiler_params=_cparams(1),
        name="swa_sample",
    )(za_pad, za_pad, za_pad, kbuf, vbuf)


def _mix_body(*refs, n_pat):
    x_ref, hm_ref = refs[0], refs[1]
    n_lse = n_pat if n_pat > 1 else 0
    o_refs = refs[2:2 + n_pat]
    l_refs = refs[2 + n_pat:2 + n_pat + n_lse]
    wmo_ref, g_ref, b_ref, wq_ref, x1_ref, q_ref = refs[2 + n_pat + n_lse:]
    if n_pat == 1:
        ha = o_refs[0][...]
    else:
        cols = []
        for h in range(A_HEADS):
            sl = slice(h * A_DH, (h + 1) * A_DH)
            ls = [l_ref[:, h:h + 1] for l_ref in l_refs]
            top = functools.reduce(jnp.maximum, ls)
            es = [jnp.exp(x - top) for x in ls]
            inv = 1.0 / functools.reduce(lambda a, b: a + b, es)
            acc = None
            for o_ref, e in zip(o_refs, es):
                term = o_ref[:, sl].astype(F32) * (e * inv)
                acc = term if acc is None else acc + term
            cols.append(acc.astype(BF16))
        ha = jnp.concatenate(cols, axis=1)
    mix = _dot(hm_ref[...], wmo_ref[0:MIX_V, :]) + _dot(ha, wmo_ref[MIX_V:MIX_V + MIX_A, :])
    x1 = _layer_norm(DN_ALPHA * x_ref[...] + mix, g_ref[...], b_ref[...])
    x1_ref[...] = x1
    q_ref[...] = _dot(x1.astype(BF16), wq_ref[...]).astype(q_ref.dtype)


def _mix_ln1_q(x, hm, outs, lses, wmo, ln_g, ln_b, wq, tm):
    M = x.shape[0]
    n_pat = len(outs)
    assert len(lses) == (n_pat if n_pat > 1 else 0)
    row = lambda w: pl.BlockSpec((tm, w), lambda i: (i, 0))
    return pl.pallas_call(
        functools.partial(_mix_body, n_pat=n_pat),
        grid=(M // tm,),
        in_specs=[row(D_MODEL), row(MIX_V)] + [row(MIX_A)] * n_pat + [row(LANES)] * len(lses)
        + [_resident(wmo.shape), _resident(ln_g.shape), _resident(ln_b.shape), _resident(wq.shape)],
        out_specs=[row(D_MODEL), row(D_MODEL)],
        out_shape=[jax.ShapeDtypeStruct((M, D_MODEL), F32), jax.ShapeDtypeStruct((M, D_MODEL), BF16)],
        compiler_params=_cparams(1),
        name="mix_ln1_q",
    )(x, hm, *outs, *lses, wmo, ln_g, ln_b, wq)


def _xattn_body(q_ref, k_ref, v_ref, o_ref):
    scale = X_DH ** -0.5
    for h in range(X_HEADS):
        sl = slice(h * X_DH, (h + 1) * X_DH)
        s = _dot_nt(q_ref[0, :, sl], k_ref[:, h, :].astype(BF16)) * scale
        p = jnp.exp(s - jnp.max(s, axis=1, keepdims=True))
        p = p / jnp.sum(p, axis=1, keepdims=True)
        o_ref[0, :, sl] = _dot(p.astype(BF16), v_ref[:, h, :].astype(BF16)).astype(o_ref.dtype)


def _xattn(q, mk, mv, tq):
    B, Tq, _ = q.shape
    nm = mk.shape[2]
    mem_blk = (None, None, nm, X_HEADS, X_DH)
    return pl.pallas_call(
        _xattn_body,
        grid=(B, Tq // tq),
        in_specs=[pl.BlockSpec((1, tq, D_MODEL), lambda b, i: (b, i, 0)),
                  pl.BlockSpec(mem_blk, lambda b, i: (0, b, 0, 0, 0)),
                  pl.BlockSpec(mem_blk, lambda b, i: (0, b, 0, 0, 0))],
        out_specs=pl.BlockSpec((1, tq, D_MODEL), lambda b, i: (b, i, 0)),
        out_shape=jax.ShapeDtypeStruct((B, Tq, D_MODEL), BF16),
        compiler_params=_cparams(2),
        name="xattn",
    )(q, mk, mv)


def _xo_body(x1_ref, o_ref, wo_ref, g_ref, b_ref, wr_ref, br_ref, x2_ref, ti_ref, tg_ref):
    y = _dot(o_ref[...], wo_ref[...])
    x2 = _layer_norm(DN_ALPHA * x1_ref[...] + y, g_ref[...], b_ref[...])
    x2_ref[...] = x2
    tm = x2.shape[0]
    lane = lax.broadcasted_iota(I32, (tm, LANES), 1)
    lanef = lane.astype(F32)
    logits = jnp.where(lane < N_EXPERTS, _dot(x2.astype(BF16), wr_ref[...]) + br_ref[...], -jnp.inf)
    vals, idxs = [], []
    cur = logits
    for _ in range(TOP_K):
        top = jnp.max(cur, axis=1, keepdims=True)
        idx = jnp.min(jnp.where(cur == top, lanef, float(LANES)), axis=1, keepdims=True)
        vals.append(top)
        idxs.append(idx)
        cur = jnp.where(lanef == idx, -jnp.inf, cur)
    es = [jnp.exp(v - vals[0]) for v in vals]
    inv = 1.0 / functools.reduce(lambda a, b: a + b, es)
    ti = jnp.zeros((tm, LANES), F32)
    tg = jnp.zeros((tm, LANES), F32)
    for k in range(TOP_K):
        ti = jnp.where(lane == k, idxs[k], ti)
        tg = jnp.where(lane == k, es[k] * inv, tg)
    ti_ref[...] = ti.astype(I32)
    tg_ref[...] = tg


def _xo_ln2_router(x1, o, wo, ln_g, ln_b, wr, br, tm):
    M = x1.shape[0]
    row = lambda w: pl.BlockSpec((tm, w), lambda i: (i, 0))
    return pl.pallas_call(
        _xo_body,
        grid=(M // tm,),
        in_specs=[row(D_MODEL), row(D_MODEL), _resident(wo.shape), _resident(ln_g.shape), _resident(ln_b.shape),
                  _resident(wr.shape), _resident(br.shape)],
        out_specs=[row(D_MODEL), row(LANES), row(LANES)],
        out_shape=[jax.ShapeDtypeStruct((M, D_MODEL), F32), jax.ShapeDtypeStruct((M, LANES), I32),
                   jax.ShapeDtypeStruct((M, LANES), F32)],
        compiler_params=_cparams(1),
        name="xo_ln2_router",
    )(x1, o, wo, ln_g, ln_b, wr, br)


MOE_SUB_ROWS = 256
MOE_FF_CHUNK = 256


def _ffn_body(te_ref, tr_ref, nu_ref, idc_ref, idn_ref, x_hbm, wg_ref, bg_ref, wu_ref, bu_ref, wd_ref, bd_ref,
              o_ref, xbuf, sem, wgb, wub, wdb, *, tm, sb, nch):
    t = pl.program_id(0)
    c = pl.program_id(1)
    n_used = nu_ref[0]
    slot = lax.rem(t, 2)
    per_step = tm // nch

    def row_copy(tok, sl, r):
        return pltpu.make_async_copy(x_hbm.at[pl.ds(tok, 1), :], xbuf.at[sl, pl.ds(r, 1), :], sem.at[sl])

    @pl.when(t >= n_used)
    def _():
        @pl.when(c == 0)
        def _():
            o_ref[...] = jnp.zeros((tm, D_MODEL), F32)

    @pl.when(t < n_used)
    def _():
        @pl.when((t == 0) & (c == 0))
        def _():
            def issue(r, carry):
                row_copy(idc_ref[0, 0, r], 0, r).start()
                return carry
            lax.fori_loop(0, tm, issue, 0, unroll=8)

        @pl.when(c == 0)
        def _():
            pltpu.make_async_copy(x_hbm.at[pl.ds(0, tm), :], xbuf.at[slot], sem.at[slot]).wait()

        @pl.when(t + 1 < n_used)
        def _():
            def issue(r, carry):
                rr = c * per_step + r
                row_copy(idn_ref[0, 0, rr], 1 - slot, rr).start()
                return carry
            lax.fori_loop(0, per_step, issue, 0, unroll=8)

        wgb[...] = wg_ref[0, 0].astype(BF16)
        wub[...] = wu_ref[0, 0].astype(BF16)
        wdb[...] = wd_ref[0, 0].astype(BF16)
        rows = tr_ref[t]
        for s in range(tm // sb):
            rs = slice(s * sb, (s + 1) * sb)

            @pl.when((s * sb >= rows) & (c == 0))
            def _():
                o_ref[rs, :] = jnp.zeros((sb, D_MODEL), F32)

            @pl.when(s * sb < rows)
            def _():
                @pl.when(c == 0)
                def _():
                    o_ref[rs, :] = jnp.broadcast_to(bd_ref[0], (sb, D_MODEL))

                xb = xbuf[slot, rs, :].astype(BF16)
                g = jnp.minimum(_dot(xb, wgb[...]) + bg_ref[0], SWIGLU_LIMIT)
                u = jnp.clip(_dot(xb, wub[...]) + bu_ref[0], -SWIGLU_LIMIT, SWIGLU_LIMIT)
                hid = (u + 1.0) * (g * _sigmoid(SWIGLU_ALPHA * g))
                o_ref[rs, :] += _dot(hid.astype(BF16), wdb[...])


def _moe_ffn(x, row_tok, tile_e, tile_rows, n_used, wg, bg, wu, bu, wd, bd, tm, sb):
    P = row_tok.shape[0]
    n_tiles = P // tm
    nch = D_FF // MOE_FF_CHUNK
    fc = MOE_FF_CHUNK
    assert tm % nch == 0 and (tm // nch) % 8 == 0

    def tile(t, nu):
        return jnp.minimum(t, jnp.maximum(nu[0] - 1, 0))

    def chunk(t, c, nu):
        return jnp.where(t < nu[0], c, nch - 1)

    idx_blk = (1, 1, tm)
    return pl.pallas_call(
        functools.partial(_ffn_body, tm=tm, sb=sb, nch=nch),
        grid_spec=pltpu.PrefetchScalarGridSpec(
            num_scalar_prefetch=3,
            grid=(n_tiles, nch),
            in_specs=[
                pl.BlockSpec(idx_blk, lambda t, c, te, tr, nu: (tile(t, nu), 0, 0), memory_space=pltpu.SMEM),
                pl.BlockSpec(idx_blk, lambda t, c, te, tr, nu: (tile(t + 1, nu), 0, 0), memory_space=pltpu.SMEM),
                pl.BlockSpec(memory_space=pl.ANY),
                pl.BlockSpec((1, 1, D_MODEL, fc), lambda t, c, te, tr, nu: (0, te[tile(t, nu)], 0, chunk(t, c, nu))),
                pl.BlockSpec((1, 1, fc), lambda t, c, te, tr, nu: (te[tile(t, nu)], 0, chunk(t, c, nu))),
                pl.BlockSpec((1, 1, D_MODEL, fc), lambda t, c, te, tr, nu: (0, te[tile(t, nu)], 0, chunk(t, c, nu))),
                pl.BlockSpec((1, 1, fc), lambda t, c, te, tr, nu: (te[tile(t, nu)], 0, chunk(t, c, nu))),
                pl.BlockSpec((1, 1, fc, D_MODEL), lambda t, c, te, tr, nu: (0, te[tile(t, nu)], chunk(t, c, nu), 0)),
                pl.BlockSpec((1, 1, D_MODEL), lambda t, c, te, tr, nu: (te[tile(t, nu)], 0, 0)),
            ],
            out_specs=pl.BlockSpec((tm, D_MODEL), lambda t, c, te, tr, nu: (t, 0)),
            scratch_shapes=[pltpu.VMEM((2, tm, D_MODEL), F32), pltpu.SemaphoreType.DMA((2,)),
                            pltpu.VMEM((D_MODEL, fc), BF16), pltpu.VMEM((D_MODEL, fc), BF16),
                            pltpu.VMEM((fc, D_MODEL), BF16)]),
        out_shape=jax.ShapeDtypeStruct((P, D_MODEL), F32),
        compiler_params=_cparams(2),
        name="moe_ffn",
    )(tile_e, tile_rows, n_used, row_tok.reshape(n_tiles, 1, tm), row_tok.reshape(n_tiles, 1, tm), x,
      wg, bg, wu, bu, wd, bd)


def _combine_body(pos_ref, x2_ref, tg_ref, ys_hbm, g_ref, b_ref, o_ref, buf, sem, *, tc):
    def row_copy(src_row, k, j):
        return pltpu.make_async_copy(ys_hbm.at[pl.ds(src_row, 1), :], buf.at[k, pl.ds(j, 1), :], sem)

    def issue(j, carry):
        for k in range(TOP_K):
            row_copy(pos_ref[0, 0, j * TOP_K + k], k, j).start()
        return carry
    lax.fori_loop(0, tc, issue, 0, unroll=4)

    def drain(j, carry):
        for k in range(TOP_K):
            row_copy(0, k, 0).wait()
        return carry
    lax.fori_loop(0, tc, drain, 0, unroll=4)

    y = None
    for k in range(TOP_K):
        term = buf[k] * tg_ref[:, k:k + 1]
        y = term if y is None else y + term
    o_ref[...] = _layer_norm(DN_ALPHA * x2_ref[...] + y, g_ref[...], b_ref[...])


def _combine_ln3(x2, tg, ys, pos, ln_g, ln_b, tc):
    M = x2.shape[0]
    nblk = M // tc
    return pl.pallas_call(
        functools.partial(_combine_body, tc=tc),
        grid=(nblk,),
        in_specs=[pl.BlockSpec((1, 1, tc * TOP_K), lambda i: (i, 0, 0), memory_space=pltpu.SMEM),
                  pl.BlockSpec((tc, D_MODEL), lambda i: (i, 0)),
                  pl.BlockSpec((tc, LANES), lambda i: (i, 0)),
                  pl.BlockSpec(memory_space=pl.ANY),
                  _resident(ln_g.shape), _resident(ln_b.shape)],
        out_specs=pl.BlockSpec((tc, D_MODEL), lambda i: (i, 0)),
        out_shape=jax.ShapeDtypeStruct((M, D_MODEL), F32),
        scratch_shapes=[pltpu.VMEM((TOP_K, tc, D_MODEL), F32), pltpu.SemaphoreType.DMA(())],
        compiler_params=_cparams(1),
        name="moe_combine",
    )(pos.reshape(nblk, 1, tc * TOP_K), x2, tg, ys, ln_g, ln_b)


def _moe(x2, topi, gates, wg, bg, wu, bu, wd, bd, ln_g, ln_b, tm, tc):
    ntok = x2.shape[0]
    nk = ntok * TOP_K
    sb = min(tm, MOE_SUB_ROWS)
    n_tiles = -(-(nk + N_EXPERTS * (tm - 1)) // tm)
    e_flat = topi[:, :TOP_K].reshape(nk)
    onehot = (e_flat[:, None] == jnp.arange(N_EXPERTS, dtype=I32)[None, :]).astype(I32)
    csum = jnp.cumsum(onehot, axis=0)
    counts = csum[-1]
    rank = jnp.take_along_axis(csum, e_flat[:, None], axis=1)[:, 0] - 1
    tiles_e = (counts + tm - 1) // tm
    tile_end = jnp.cumsum(tiles_e)
    tile_start = tile_end - tiles_e
    dest = tile_start[e_flat] * tm + rank
    n_used = tile_end[-1:].astype(I32)
    t_ids = jnp.arange(n_tiles, dtype=I32)
    tile_e = jnp.minimum(jnp.searchsorted(tile_end, t_ids, side="right"), N_EXPERTS - 1).astype(I32)
    tile_rows = jnp.clip(counts[tile_e] - (t_ids - tile_start[tile_e]) * tm, 0, tm)
    tile_rows = jnp.where(t_ids < n_used[0], tile_rows, 0).astype(I32)
    P = n_tiles * tm
    row_tok = jnp.zeros((P,), I32).at[dest].set(jnp.arange(nk, dtype=I32) // TOP_K, unique_indices=True)
    ys = _moe_ffn(x2, row_tok, tile_e, tile_rows, n_used, wg, bg, wu, bu, wd, bd, tm, sb)
    return _combine_ln3(x2, gates, ys, dest, ln_g, ln_b, tc)


def _finish(x, hm, outs, lses, mem_k, mem_v, batch, w, tm_mix, tq, tm_xo, moe_tm, tc):
    M = x.shape[0]
    x1, q = _mix_ln1_q(x, hm, outs, lses, w["wmo"], w["ln1_g"], w["ln1_b"], w["wxq"], tm_mix)
    o = _xattn(q.reshape(batch, M // batch, D_MODEL), mem_k, mem_v, tq).reshape(M, D_MODEL)
    x2, topi, gates = _xo_ln2_router(x1, o, w["wxo"], w["ln2_g"], w["ln2_b"], w["wr"], w["br"], tm_xo)
    return _moe(x2, topi, gates, w["wg"], w["bg"], w["wu"], w["bu"], w["wd"], w["bd"], w["ln3_g"], w["ln3_b"],
                moe_tm, tc)


def kernel(x_prompt, x_sample, mem_prompt, state_mlstm_C, state_mlstm_n, state_mlstm_m, cache_swa_k, cache_swa_v, cache_mem_k, cache_mem_v, w_in, b_igate, b_fgate, g_mnorm, w_mix_out, ln1_g, ln1_b, w_xq, w_xk, w_xv, w_xo, ln2_g, ln2_b, w_router, b_router, w_gate, b_gate, w_up, b_up, w_down, b_down, ln3_g, ln3_b):
    assert DEPTH == 1
    B, T, _ = x_prompt.shape
    DB, S, _ = x_sample.shape
    assert B == 1
    n_mem = mem_prompt.shape[1]
    wb = cache_swa_k.shape[2]
    row2 = lambda a: a[0].reshape(1, -1)

    wi = w_in[0]
    c0 = 2 * MIX_QK + 2 * MIX_V
    w_m = wi[:, :c0].astype(BF16)
    w_g = jnp.pad(wi[:, c0:c0 + 2 * M_HEADS], ((0, 0), (0, LANES - 2 * M_HEADS))).astype(BF16)
    w_a = wi[:, c0 + 2 * M_HEADS:].astype(BF16)
    gate_bias = jnp.pad(jnp.concatenate([b_igate[0], b_fgate[0]]), (0, LANES - 2 * M_HEADS)).reshape(1, LANES)
    w = dict(
        wmo=w_mix_out[0].astype(BF16), ln1_g=row2(ln1_g), ln1_b=row2(ln1_b), wxq=w_xq[0].astype(BF16),
        wxo=w_xo[0].astype(BF16), ln2_g=row2(ln2_g), ln2_b=row2(ln2_b),
        wr=jnp.pad(w_router[0], ((0, 0), (0, LANES - N_EXPERTS))).astype(BF16),
        br=jnp.pad(b_router[0], (0, LANES - N_EXPERTS)).reshape(1, LANES),
        wg=w_gate, bg=b_gate[0].reshape(N_EXPERTS, 1, D_FF), wu=w_up, bu=b_up[0].reshape(N_EXPERTS, 1, D_FF),
        wd=w_down, bd=b_down[0].reshape(N_EXPERTS, 1, D_MODEL), ln3_g=row2(ln3_g), ln3_b=row2(ln3_b))
    gm = g_mnorm[0].reshape(1, MIX_V)

    xp = x_prompt.reshape(T, D_MODEL)
    zm = _mm(xp, w_m, BF16, 512, 1024, "proj_mlstm")
    zg = _mm(xp, w_g, F32, 512, LANES, "proj_gates")
    za = _mm(xp, w_a, BF16, 512, 1024, "proj_attn")
    wbp = min(wb, T)
    kv_tail = _mm(xp[T - wbp:], w_a[:, MIX_A:], F32, 512, 1024, "proj_kv_tail")
    zeros_c = jnp.zeros((1, M_HEADS, M_DQK, M_DV), F32)
    zeros_n = jnp.zeros((1, M_HEADS, M_DQK), F32)
    zeros_m = jnp.zeros((1, M_HEADS, LANES), F32)
    hm, pC, pn, pm = _mlstm(zm, zg, gate_bias, gm, zeros_c, zeros_n, zeros_m, 1, T // 128)
    pats = [_dilated_pattern(za, window, dil) for window, dil in PATTERNS]
    mp = mem_prompt.reshape(n_mem, D_MODEL)
    mem_k = _mm(mp, w_xk[0].astype(BF16), F32, n_mem, 1024, "mem_k")
    mem_v = _mm(mp, w_xv[0].astype(BF16), F32, n_mem, 1024, "mem_v")
    mem_k5 = mem_k.reshape(1, B, n_mem, X_HEADS, X_DH)
    mem_v5 = mem_v.reshape(1, B, n_mem, X_HEADS, X_DH)
    yp = _finish(xp, hm, [p[0] for p in pats], [p[1] for p in pats], mem_k5.astype(BF16), mem_v5.astype(BF16),
                 1, w, tm_mix=256, tq=512, tm_xo=256, moe_tm=1024, tc=128)

    ns = DB * S
    xs_ = x_sample.reshape(ns, D_MODEL)
    zm_s = _mm(xs_, w_m, BF16, ns, 1024, "proj_mlstm_s")
    zg_s = _mm(xs_, w_g, F32, ns, LANES, "proj_gates_s")
    za_s = _mm(xs_, w_a, F32, ns, 1024, "proj_attn_s")
    pad_rows = 128 - S
    zm_pad = jnp.pad(zm_s.reshape(DB, S, -1), ((0, 0), (0, pad_rows), (0, 0))).reshape(DB * 128, -1)
    lane = jnp.arange(LANES)
    neutral = jnp.where(lane < M_HEADS, NEG_BIG, jnp.where(lane < 2 * M_HEADS, -NEG_BIG, 0.0)).astype(F32)
    zg_pad = jnp.concatenate([zg_s.reshape(DB, S, LANES), jnp.broadcast_to(neutral, (DB, pad_rows, LANES))],
                             axis=1).reshape(DB * 128, LANES)
    m0 = jnp.broadcast_to(state_mlstm_m[0][:, :, None], (DB, M_HEADS, LANES))
    hm_s, sC, sn, sm = _mlstm(zm_pad, zg_pad, gate_bias, gm, state_mlstm_C[0], state_mlstm_n[0], m0, DB, 1)
    hm_s = hm_s.reshape(DB, 128, MIX_V)[:, :S].reshape(ns, MIX_V)
    za_pad = jnp.pad(za_s.reshape(DB, S, -1), ((0, 0), (0, SWA_ROWS - S), (0, 0))).astype(BF16)
    ha_s = _swa_sample(za_pad, cache_swa_k, cache_swa_v)[:, :S].reshape(ns, MIX_A)
    x1_s, q_s = _mix_ln1_q(xs_, hm_s, [ha_s], [], w["wmo"], w["ln1_g"], w["ln1_b"], w["wxq"], ns)
    q_pad = jnp.pad(q_s.reshape(DB, S, D_MODEL), ((0, 0), (0, SWA_ROWS - S), (0, 0)))
    o_s = _xattn(q_pad, cache_mem_k, cache_mem_v, SWA_ROWS)[:, :S].reshape(ns, D_MODEL)
    x2_s, topi_s, gates_s = _xo_ln2_router(x1_s, o_s, w["wxo"], w["ln2_g"], w["ln2_b"], w["wr"], w["br"], ns)
    ys_ = _moe(x2_s, topi_s, gates_s, w["wg"], w["bg"], w["wu"], w["bu"], w["wd"], w["bd"], w["ln3_g"], w["ln3_b"],
               128, ns)

    return (yp.reshape(B, T, D_MODEL), ys_.reshape(DB, S, D_MODEL),
            pC[None], pn[None], pm[:, :, 0][None],
            kv_tail[:, :MIX_A].reshape(1, B, wbp, A_HEADS, A_DH), kv_tail[:, MIX_A:].reshape(1, B, wbp, A_HEADS, A_DH),
            mem_k5, mem_v5,
            sC[None], sn[None], sm[:, :, 0][None],
            za_s[:, MIX_A:2 * MIX_A].reshape(1, DB, S, A_HEADS, A_DH),
            za_s[:, 2 * MIX_A:].reshape(1, DB, S, A_HEADS, A_DH))
```

```python
import functools

import jax
import jax.numpy as jnp
from jax import lax
from jax.experimental import pallas as pl
from jax.experimental.pallas import tpu as pltpu

F32, BF16, I32 = jnp.float32, jnp.bfloat16, jnp.int32

D_MODEL = 2048
DEPTH = 1
M_HEADS, M_DQK, M_DV = 4, 128, 256
HEAD_EPS = 1e-6
A_HEADS, A_DH = 8, 128
PATTERNS = ((128, 1), (512, 4), (2048, 16))
BAND_BLOCK = 128
X_HEADS = 4
X_DH = D_MODEL // X_HEADS
N_EXPERTS, TOP_K, D_FF = 32, 4, 2048
SWIGLU_LIMIT, SWIGLU_ALPHA = 7.0, 1.702
DN_ALPHA = (2 * DEPTH) ** 0.25
LN_EPS = 1e-5
MIX_V = M_HEADS * M_DV
MIX_A = A_HEADS * A_DH
MIX_QK = M_HEADS * M_DQK

LANES = 128
VMEM_LIMIT_BYTES = 58 * 1024 * 1024

NEG_BIG = -1e30


def _cparams(n_axes, vmem=VMEM_LIMIT_BYTES):
    return pltpu.CompilerParams(dimension_semantics=("arbitrary",) * n_axes, vmem_limit_bytes=vmem)


def _dot(a, b):
    return jnp.dot(a, b, preferred_element_type=F32)


def _dot_nt(a, b):
    return lax.dot_general(a, b, (((1,), (1,)), ((), ())), preferred_element_type=F32)


def _dot_tn(a, b):
    return lax.dot_general(a, b, (((0,), (0,)), ((), ())), preferred_element_type=F32)


def _log_sigmoid(x):
    return jnp.minimum(x, 0.0) - jnp.log(1.0 + jnp.exp(-jnp.abs(x)))


def _sigmoid(x):
    return 1.0 / (1.0 + jnp.exp(-x))


def _layer_norm(v, g, b):
    mu = jnp.mean(v, axis=-1, keepdims=True)
    d = v - mu
    var = jnp.mean(d * d, axis=-1, keepdims=True)
    return d * lax.rsqrt(var + LN_EPS) * g + b


def _resident(shape):
    nd = len(shape)
    return pl.BlockSpec(shape, lambda *_: (0,) * nd, pipeline_mode=pl.Buffered(1))


def _mm_body(x_ref, w_ref, o_ref):
    o_ref[...] = _dot(x_ref[...].astype(BF16), w_ref[...]).astype(o_ref.dtype)


def _mm(x, w, out_dtype, tm, tn, name):
    M, K = x.shape
    N = w.shape[1]
    assert M % tm == 0 and N % tn == 0
    return pl.pallas_call(
        _mm_body,
        grid=(M // tm, N // tn),
        in_specs=[pl.BlockSpec((tm, K), lambda i, j: (i, 0)), pl.BlockSpec((K, tn), lambda i, j: (0, j))],
        out_specs=pl.BlockSpec((tm, tn), lambda i, j: (i, j)),
        out_shape=jax.ShapeDtypeStruct((M, N), out_dtype),
        compiler_params=_cparams(2),
        name=name,
    )(x, w)


def _mlstm_body(q_ref, k_ref, v_ref, zo_ref, zg_ref, gb_ref, gm_ref, c0_ref, n0_ref, m0_ref,
                hm_ref, c_ref, n_ref, m_ref, *, L):
    @pl.when(pl.program_id(1) == 0)
    def _():
        c_ref[...] = c0_ref[...]
        n_ref[...] = n0_ref[...]
        m_ref[...] = m0_ref[...]

    scale = M_DQK ** -0.5
    g = zg_ref[...] + gb_ref[...]
    gt = g.T
    row = lax.broadcasted_iota(I32, (L, L), 0)
    col = lax.broadcasted_iota(I32, (L, L), 1)
    tri = row >= col
    for h in range(M_HEADS):
        i_col = g[:, h:h + 1]
        f_col = _log_sigmoid(g[:, M_HEADS + h:M_HEADS + h + 1])
        i_row = gt[h:h + 1, :]
        f_row = _log_sigmoid(gt[M_HEADS + h:M_HEADS + h + 1, :])
        q = q_ref[:, h * M_DQK:(h + 1) * M_DQK]
        k = k_ref[:, h * M_DQK:(h + 1) * M_DQK]
        v = v_ref[:, h * M_DV:(h + 1) * M_DV]
        zo = zo_ref[:, h * M_DV:(h + 1) * M_DV].astype(F32)
        C = c_ref[0, h]
        n = n_ref[0, h:h + 1, :]
        m = m_ref[0, h:h + 1, 0:1]
        b_col = jnp.sum(jnp.where(tri, f_row, 0.0), axis=1, keepdims=True)
        b_row = jnp.sum(jnp.where(row <= col, f_col, 0.0), axis=0, keepdims=True)
        logd = jnp.where(tri, b_col - b_row + i_row, -jnp.inf)
        inter = b_col + m
        mt = jnp.maximum(inter, jnp.max(logd, axis=1, keepdims=True))
        sd = _dot_nt(q, k) * scale * jnp.exp(logd - mt)
        sc = jnp.exp(inter - mt)
        num = _dot(sd.astype(BF16), v) + sc * _dot(q, C.astype(BF16))
        den = jnp.sum(sd, axis=1, keepdims=True) + sc * jnp.sum(q.astype(F32) * n, axis=1, keepdims=True)
        hh = num / jnp.maximum(jnp.abs(den), jnp.exp(-mt))
        hn = hh * lax.rsqrt(jnp.mean(hh * hh, axis=1, keepdims=True) + HEAD_EPS)
        out = hn * gm_ref[:, h * M_DV:(h + 1) * M_DV] * _sigmoid(zo)
        hm_ref[:, h * M_DV:(h + 1) * M_DV] = out.astype(hm_ref.dtype)
        bl = b_col[L - 1:L, :]
        ml = mt[L - 1:L, :]
        w_col = jnp.exp(bl - b_col + i_col - ml)
        scl = jnp.exp(bl + m - ml)
        kw = k.astype(F32) * (w_col * scale)
        c_ref[0, h] = scl * C + _dot_tn(kw.astype(BF16), v)
        n_ref[0, h:h + 1, :] = scl * n + jnp.sum(kw, axis=0, keepdims=True)
        m_ref[0, h:h + 1, :] = jnp.broadcast_to(ml, (1, LANES))


def _mlstm(zm, zg, gate_bias, g_mnorm, c0, n0, m0, B, nc, L=128):
    T = B * nc * L
    assert zm.shape == (T, 2 * MIX_QK + 2 * MIX_V)
    row_blk = lambda b, c: b * nc + c
    state_specs = [pl.BlockSpec((1, M_HEADS, M_DQK, M_DV), lambda b, c: (b, 0, 0, 0)),
                   pl.BlockSpec((1, M_HEADS, M_DQK), lambda b, c: (b, 0, 0)),
                   pl.BlockSpec((1, M_HEADS, LANES), lambda b, c: (b, 0, 0))]
    return pl.pallas_call(
        functools.partial(_mlstm_body, L=L),
        grid=(B, nc),
        in_specs=[pl.BlockSpec((L, MIX_QK), lambda b, c: (row_blk(b, c), 0)),
                  pl.BlockSpec((L, MIX_QK), lambda b, c: (row_blk(b, c), 1)),
                  pl.BlockSpec((L, MIX_V), lambda b, c: (row_blk(b, c), 1)),
                  pl.BlockSpec((L, MIX_V), lambda b, c: (row_blk(b, c), 2)),
                  pl.BlockSpec((L, LANES), lambda b, c: (row_blk(b, c), 0)),
                  pl.BlockSpec((1, LANES), lambda b, c: (0, 0)),
                  pl.BlockSpec((1, MIX_V), lambda b, c: (0, 0))] + state_specs,
        out_specs=[pl.BlockSpec((L, MIX_V), lambda b, c: (row_blk(b, c), 0))] + state_specs,
        out_shape=[jax.ShapeDtypeStruct((T, MIX_V), BF16),
                   jax.ShapeDtypeStruct((B, M_HEADS, M_DQK, M_DV), F32),
                   jax.ShapeDtypeStruct((B, M_HEADS, M_DQK), F32),
                   jax.ShapeDtypeStruct((B, M_HEADS, LANES), F32)],
        compiler_params=_cparams(2),
        name="mlstm",
    )(zm, zm, zm, zm, zg, gate_bias, g_mnorm, c0, n0, m0)


def _dil_body(q_ref, kc_ref, kp_ref, vc_ref, vp_ref, o_ref, l_ref, *, span):
    scale = A_DH ** -0.5
    n = pl.program_id(1)
    row = lax.broadcasted_iota(I32, (BAND_BLOCK, BAND_BLOCK), 0)
    col = lax.broadcasted_iota(I32, (BAND_BLOCK, BAND_BLOCK), 1)
    first = jnp.where(n > 0, 0, 2 * BAND_BLOCK)
    mask_p = (BAND_BLOCK + row - col + first) <= span
    mask_c = row >= col
    lane = lax.broadcasted_iota(I32, (BAND_BLOCK, LANES), 1)
    lse_tile = jnp.zeros((BAND_BLOCK, LANES), F32)
    for h in range(A_HEADS):
        sl = slice(h * A_DH, (h + 1) * A_DH)
        q = q_ref[:, sl]
        sp = jnp.where(mask_p, _dot_nt(q, kp_ref[:, sl]) * scale, -jnp.inf)
        sc = jnp.where(mask_c, _dot_nt(q, kc_ref[:, sl]) * scale, -jnp.inf)
        m = jnp.maximum(jnp.max(sp, axis=1, keepdims=True), jnp.max(sc, axis=1, keepdims=True))
        pp = jnp.exp(sp - m)
        pc = jnp.exp(sc - m)
        l = jnp.sum(pp, axis=1, keepdims=True) + jnp.sum(pc, axis=1, keepdims=True)
        o = (_dot(pp.astype(BF16), vp_ref[:, sl]) + _dot(pc.astype(BF16), vc_ref[:, sl])) / l
        o_ref[:, sl] = o.astype(o_ref.dtype)
        lse_tile = jnp.where(lane == h, m + jnp.log(l), lse_tile)
    l_ref[...] = lse_tile


def _dilated_pattern(za, window, dil):
    T = za.shape[0]
    span = window // dil
    assert T % (dil * BAND_BLOCK) == 0 and BAND_BLOCK - 1 <= span
    N = T // dil
    nb = N // BAND_BLOCK
    zv = za.reshape(N, dil * 3 * MIX_A)
    prev = lambda n: jnp.maximum(n - 1, 0)
    blk = (BAND_BLOCK, MIX_A)
    o, lse = pl.pallas_call(
        functools.partial(_dil_body, span=span),
        grid=(dil, nb),
        in_specs=[pl.BlockSpec(blk, lambda r, n: (n, 3 * r)),
                  pl.BlockSpec(blk, lambda r, n: (n, 3 * r + 1)),
                  pl.BlockSpec(blk, lambda r, n: (prev(n), 3 * r + 1)),
                  pl.BlockSpec(blk, lambda r, n: (n, 3 * r + 2)),
                  pl.BlockSpec(blk, lambda r, n: (prev(n), 3 * r + 2))],
        out_specs=[pl.BlockSpec(blk, lambda r, n: (n, r)),
                   pl.BlockSpec((BAND_BLOCK, LANES), lambda r, n: (n, r))],
        out_shape=[jax.ShapeDtypeStruct((N, dil * MIX_A), BF16),
                   jax.ShapeDtypeStruct((N, dil * LANES), F32)],
        compiler_params=_cparams(2),
        name=f"dilated_d{dil}",
    )(zv, zv, zv, zv, zv)
    return o.reshape(T, MIX_A), lse.reshape(T, LANES)


SWA_ROWS = 16


def _swa_body(q_ref, kn_ref, vn_ref, kb_ref, vb_ref, o_ref, *, wb):
    scale = A_DH ** -0.5
    s_k = lax.broadcasted_iota(I32, (SWA_ROWS, wb), 0)
    p_k = lax.broadcasted_iota(I32, (SWA_ROWS, wb), 1)
    d_k = wb + s_k - p_k
    s_n = lax.broadcasted_iota(I32, (SWA_ROWS, SWA_ROWS), 0)
    p_n = lax.broadcasted_iota(I32, (SWA_ROWS, SWA_ROWS), 1)
    d_n = s_n - p_n
    masks = []
    for window, dil in PATTERNS:
        assert dil & (dil - 1) == 0 and window % dil == 0
        mk = jnp.where(jnp.bitwise_and(d_k, dil - 1) == 0, d_k, window + 1) <= window
        mn = jnp.where(jnp.bitwise_and(d_n, dil - 1) == 0, jnp.where(d_n >= 0, d_n, window + 1), window + 1) <= window
        masks.append((mk, mn))
    for h in range(A_HEADS):
        sl = slice(h * A_DH, (h + 1) * A_DH)
        q = q_ref[0, :, sl]
        kb = kb_ref[0, pl.ds(h, wb, stride=A_HEADS), :].astype(BF16)
        vb = vb_ref[0, pl.ds(h, wb, stride=A_HEADS), :].astype(BF16)
        kn = kn_ref[0, :, sl]
        vn = vn_ref[0, :, sl]
        s_cache = _dot_nt(q, kb) * scale
        s_new = _dot_nt(q, kn) * scale
        ps, lses = [], []
        for mk, mn in masks:
            sk = jnp.where(mk, s_cache, -jnp.inf)
            sn = jnp.where(mn, s_new, -jnp.inf)
            m = jnp.maximum(jnp.max(sk, axis=1, keepdims=True), jnp.max(sn, axis=1, keepdims=True))
            pk = jnp.exp(sk - m)
            pn = jnp.exp(sn - m)
            l = jnp.sum(pk, axis=1, keepdims=True) + jnp.sum(pn, axis=1, keepdims=True)
            ps.append((pk, pn, l))
            lses.append(m + jnp.log(l))
        top = functools.reduce(jnp.maximum, lses)
        es = [jnp.exp(x - top) for x in lses]
        tot = functools.reduce(lambda a, b: a + b, es)
        pk_all = None
        pn_all = None
        for (pk, pn, l), e in zip(ps, es):
            coef = e / (tot * l)
            pk_all = pk * coef if pk_all is None else pk_all + pk * coef
            pn_all = pn * coef if pn_all is None else pn_all + pn * coef
        o = _dot(pk_all.astype(BF16), vb) + _dot(pn_all.astype(BF16), vn)
        o_ref[0, :, sl] = o.astype(o_ref.dtype)


def _swa_sample(za_pad, kbuf, vbuf):
    B, wb = kbuf.shape[0], kbuf.shape[1] // A_HEADS
    new_blk = (1, SWA_ROWS, MIX_A)
    cache_blk = (1, wb * A_HEADS, A_DH)
    return pl.pallas_call(
        functools.partial(_swa_body, wb=wb),
        grid=(B,),
        in_specs=[pl.BlockSpec(new_blk, lambda b: (b, 0, 0)),
                  pl.BlockSpec(new_blk, lambda b: (b, 0, 1)),
                  pl.BlockSpec(new_blk, lambda b: (b, 0, 2)),
                  pl.BlockSpec(cache_blk, lambda b: (b, 0, 0)),
                  pl.BlockSpec(cache_blk, lambda b: (b, 0, 0))],
        out_specs=pl.BlockSpec(new_blk, lambda b: (b, 0, 0)),
        out_shape=jax.ShapeDtypeStruct((B, SWA_ROWS, MIX_A), BF16),
        compiler_params=_cparams(1),
        name="swa_sample",
    )(za_pad, za_pad, za_pad, kbuf, vbuf)


def _mix_body(*refs, n_pat):
    x_ref, hm_ref = refs[0], refs[1]
    n_lse = n_pat if n_pat > 1 else 0
    o_refs = refs[2:2 + n_pat]
    l_refs = refs[2 + n_pat:2 + n_pat + n_lse]
    wmo_ref, g_ref, b_ref, wq_ref, x1_ref, q_ref = refs[2 + n_pat + n_lse:]
    if n_pat == 1:
        ha = o_refs[0][...]
    else:
        cols = []
        for h in range(A_HEADS):
            sl = slice(h * A_DH, (h + 1) * A_DH)
            ls = [l_ref[:, h:h + 1] for l_ref in l_refs]
            top = functools.reduce(jnp.maximum, ls)
            es = [jnp.exp(x - top) for x in ls]
            inv = 1.0 / functools.reduce(lambda a, b: a + b, es)
            acc = None
            for o_ref, e in zip(o_refs, es):
                term = o_ref[:, sl].astype(F32) * (e * inv)
                acc = term if acc is None else acc + term
            cols.append(acc.astype(BF16))
        ha = jnp.concatenate(cols, axis=1)
    mix = _dot(hm_ref[...], wmo_ref[0:MIX_V, :]) + _dot(ha, wmo_ref[MIX_V:MIX_V + MIX_A, :])
    x1 = _layer_norm(DN_ALPHA * x_ref[...] + mix, g_ref[...], b_ref[...])
    x1_ref[...] = x1
    q_ref[...] = _dot(x1.astype(BF16), wq_ref[...]).astype(q_ref.dtype)


def _mix_ln1_q(x, hm, outs, lses, wmo, ln_g, ln_b, wq, tm):
    M = x.shape[0]
    n_pat = len(outs)
    assert len(lses) == (n_pat if n_pat > 1 else 0)
    row = lambda w: pl.BlockSpec((tm, w), lambda i: (i, 0))
    return pl.pallas_call(
        functools.partial(_mix_body, n_pat=n_pat),
        grid=(M // tm,),
        in_specs=[row(D_MODEL), row(MIX_V)] + [row(MIX_A)] * n_pat + [row(LANES)] * len(lses)
        + [_resident(wmo.shape), _resident(ln_g.shape), _resident(ln_b.shape), _resident(wq.shape)],
        out_specs=[row(D_MODEL), row(D_MODEL)],
        out_shape=[jax.ShapeDtypeStruct((M, D_MODEL), F32), jax.ShapeDtypeStruct((M, D_MODEL), BF16)],
        compiler_params=_cparams(1),
        name="mix_ln1_q",
    )(x, hm, *outs, *lses, wmo, ln_g, ln_b, wq)


def _xattn_body(q_ref, k_ref, v_ref, o_ref):
    for h in range(X_HEADS):
        sl = slice(h * X_DH, (h + 1) * X_DH)
        o_ref[0, :, sl] = _xattn_head(q_ref[0, :, sl], k_ref[0, :, sl], v_ref[0, :, sl]).astype(o_ref.dtype)


def _xattn_head(q, k, v):
    s = _dot_nt(q, k.astype(BF16)) * (X_DH ** -0.5)
    p = jnp.exp(s - jnp.max(s, axis=1, keepdims=True))
    p = p / jnp.sum(p, axis=1, keepdims=True)
    return _dot(p.astype(BF16), v.astype(BF16))


def _xattn_cache_body(q_ref, k_hbm, v_hbm, o_ref, kbuf, vbuf, sem, *, nb):
    b = pl.program_id(0)
    slot = lax.rem(b, 2)

    def head_copies(bi, sl):
        cps = []
        for h in range(X_HEADS):
            cps.append(pltpu.make_async_copy(k_hbm.at[0, bi, :, h, :], kbuf.at[sl, h], sem.at[sl]))
            cps.append(pltpu.make_async_copy(v_hbm.at[0, bi, :, h, :], vbuf.at[sl, h], sem.at[sl]))
        return cps

    @pl.when(b == 0)
    def _():
        for cp in head_copies(0, 0):
            cp.start()

    for cp in head_copies(b, slot):
        cp.wait()
    nxt = jnp.minimum(b + 1, nb - 1)
    for cp in head_copies(nxt, 1 - slot):
        cp.start()
    for h in range(X_HEADS):
        sl = slice(h * X_DH, (h + 1) * X_DH)
        o_ref[0, :, sl] = _xattn_head(q_ref[0, :, sl], kbuf[slot, h], vbuf[slot, h]).astype(o_ref.dtype)

    @pl.when(b == nb - 1)
    def _():
        for cp in head_copies(nxt, 1 - slot):
            cp.wait()


def _xattn_cache(q, ck, cv):
    B, Tq, _ = q.shape
    nm = ck.shape[2]
    return pl.pallas_call(
        functools.partial(_xattn_cache_body, nb=B),
        grid=(B,),
        in_specs=[pl.BlockSpec((1, Tq, D_MODEL), lambda b: (b, 0, 0)),
                  pl.BlockSpec(memory_space=pl.ANY), pl.BlockSpec(memory_space=pl.ANY)],
        out_specs=pl.BlockSpec((1, Tq, D_MODEL), lambda b: (b, 0, 0)),
        out_shape=jax.ShapeDtypeStruct((B, Tq, D_MODEL), BF16),
        scratch_shapes=[pltpu.VMEM((2, X_HEADS, nm, X_DH), F32), pltpu.VMEM((2, X_HEADS, nm, X_DH), F32),
                        pltpu.SemaphoreType.DMA((2,))],
        compiler_params=_cparams(1),
        name="xattn_cache",
    )(q, ck, cv)


def _xattn(q, mk, mv, tq):
    B, Tq, _ = q.shape
    nm = mk.shape[1]
    mem_blk = (1, nm, D_MODEL)
    return pl.pallas_call(
        _xattn_body,
        grid=(B, Tq // tq),
        in_specs=[pl.BlockSpec((1, tq, D_MODEL), lambda b, i: (b, i, 0)),
                  pl.BlockSpec(mem_blk, lambda b, i: (b, 0, 0)),
                  pl.BlockSpec(mem_blk, lambda b, i: (b, 0, 0))],
        out_specs=pl.BlockSpec((1, tq, D_MODEL), lambda b, i: (b, i, 0)),
        out_shape=jax.ShapeDtypeStruct((B, Tq, D_MODEL), BF16),
        compiler_params=_cparams(2),
        name="xattn",
    )(q, mk, mv)


def _xo_body(x1_ref, o_ref, wo_ref, g_ref, b_ref, wr_ref, br_ref, x2_ref, ti_ref, tg_ref):
    y = _dot(o_ref[...], wo_ref[...])
    x2 = _layer_norm(DN_ALPHA * x1_ref[...] + y, g_ref[...], b_ref[...])
    x2_ref[...] = x2
    tm = x2.shape[0]
    lane = lax.broadcasted_iota(I32, (tm, LANES), 1)
    lanef = lane.astype(F32)
    logits = jnp.where(lane < N_EXPERTS, _dot(x2.astype(BF16), wr_ref[...]) + br_ref[...], -jnp.inf)
    vals, idxs = [], []
    cur = logits
    for _ in range(TOP_K):
        top = jnp.max(cur, axis=1, keepdims=True)
        idx = jnp.min(jnp.where(cur == top, lanef, float(LANES)), axis=1, keepdims=True)
        vals.append(top)
        idxs.append(idx)
        cur = jnp.where(lanef == idx, -jnp.inf, cur)
    es = [jnp.exp(v - vals[0]) for v in vals]
    inv = 1.0 / functools.reduce(lambda a, b: a + b, es)
    ti = jnp.zeros((tm, LANES), F32)
    tg = jnp.zeros((tm, LANES), F32)
    for k in range(TOP_K):
        ti = jnp.where(lane == k, idxs[k], ti)
        tg = jnp.where(lane == k, es[k] * inv, tg)
    ti_ref[...] = ti.astype(I32)
    tg_ref[...] = tg


def _xo_ln2_router(x1, o, wo, ln_g, ln_b, wr, br, tm):
    M = x1.shape[0]
    row = lambda w: pl.BlockSpec((tm, w), lambda i: (i, 0))
    return pl.pallas_call(
        _xo_body,
        grid=(M // tm,),
        in_specs=[row(D_MODEL), row(D_MODEL), _resident(wo.shape), _resident(ln_g.shape), _resident(ln_b.shape),
                  _resident(wr.shape), _resident(br.shape)],
        out_specs=[row(D_MODEL), row(LANES), row(LANES)],
        out_shape=[jax.ShapeDtypeStruct((M, D_MODEL), F32), jax.ShapeDtypeStruct((M, LANES), I32),
                   jax.ShapeDtypeStruct((M, LANES), F32)],
        compiler_params=_cparams(1),
        name="xo_ln2_router",
    )(x1, o, wo, ln_g, ln_b, wr, br)


MOE_SUB_ROWS = 256
MOE_FF_CHUNK = 256


def _ffn_body(te_ref, tr_ref, nu_ref, idc_ref, idn_ref, x_hbm, wg_ref, bg_ref, wu_ref, bu_ref, wd_ref, bd_ref,
              o_ref, stg, sem, xbb, wgb, wub, wdb, *, tm, sb, nch):
    t = pl.program_id(0)
    c = pl.program_id(1)
    n_used = nu_ref[0]
    ps = tm // nch
    cur = lax.rem(t, 2)
    ring = lax.rem(c, 2)

    def row_copy(tok, sl, r):
        return pltpu.make_async_copy(x_hbm.at[pl.ds(tok, 1), :], stg.at[sl, pl.ds(r, 1), :], sem.at[sl])

    def group_wait(sl):
        pltpu.make_async_copy(x_hbm.at[pl.ds(0, ps), :], stg.at[sl], sem.at[sl]).wait()

    def ffn_rows(rs, wg, wu, wd):
        xb = xbb[cur, rs, :]
        g = jnp.minimum(_dot(xb, wg) + bg_ref[0], SWIGLU_LIMIT)
        u = jnp.clip(_dot(xb, wu) + bu_ref[0], -SWIGLU_LIMIT, SWIGLU_LIMIT)
        hid = (u + 1.0) * (g * _sigmoid(SWIGLU_ALPHA * g))
        o_ref[rs, :] += _dot(hid.astype(BF16), wd)

    @pl.when(t >= n_used)
    def _():
        @pl.when(c == 0)
        def _():
            o_ref[...] = jnp.zeros((tm, D_MODEL), F32)

    @pl.when(t < n_used)
    def _():
        @pl.when((t == 0) & (c == 0))
        def _():
            for j in range(nch):
                def issue(r, carry):
                    row_copy(idc_ref[0, 0, j * ps + r], 0, r).start()
                    return carry
                lax.fori_loop(0, ps, issue, 0, unroll=8)
                group_wait(0)
                xbb[0, j * ps:(j + 1) * ps, :] = stg[0].astype(BF16)

        def gather_next_group():
            for r in range(ps):
                row_copy(idn_ref[0, 0, c * ps + r], ring, r).start()

        @pl.when((t > 0) | (c > 0))
        def _():
            prev_c = lax.rem(c + nch - 1, nch)
            dst = jnp.where(c == 0, cur, 1 - cur)
            group_wait(1 - ring)
            xbb[dst, pl.ds(pl.multiple_of(prev_c * ps, ps), ps), :] = stg[1 - ring].astype(BF16)

        rows = tr_ref[t]

        @pl.when(rows > tm - sb)
        def _():
            @pl.when(c == 0)
            def _():
                o_ref[...] = jnp.broadcast_to(bd_ref[0], (tm, D_MODEL))
            gather_next_group()
            ffn_rows(slice(0, tm), wg_ref[0, 0].astype(BF16), wu_ref[0, 0].astype(BF16), wd_ref[0, 0].astype(BF16))

        @pl.when(rows <= tm - sb)
        def _():
            gather_next_group()
            wgb[...] = wg_ref[0, 0].astype(BF16)
            wub[...] = wu_ref[0, 0].astype(BF16)
            wdb[...] = wd_ref[0, 0].astype(BF16)
            for s in range(tm // sb):
                rs = slice(s * sb, (s + 1) * sb)

                @pl.when((s * sb >= rows) & (c == 0))
                def _():
                    o_ref[rs, :] = jnp.zeros((sb, D_MODEL), F32)

                @pl.when(s * sb < rows)
                def _():
                    @pl.when(c == 0)
                    def _():
                        o_ref[rs, :] = jnp.broadcast_to(bd_ref[0], (sb, D_MODEL))
                    ffn_rows(rs, wgb[...], wub[...], wdb[...])

        @pl.when((t == n_used - 1) & (c == nch - 1))
        def _():
            group_wait(ring)


def _moe_ffn(x, row_tok, tile_e, tile_rows, n_used, wg, bg, wu, bu, wd, bd, tm, sb):
    P = row_tok.shape[0]
    n_tiles = P // tm
    nch = D_FF // MOE_FF_CHUNK
    fc = MOE_FF_CHUNK
    assert tm % nch == 0 and (tm // nch) % 16 == 0 and nch % 2 == 0

    def tile(t, nu):
        return jnp.minimum(t, jnp.maximum(nu[0] - 1, 0))

    def chunk(t, c, nu):
        return jnp.where(t < nu[0], c, nch - 1)

    idx_blk = (1, 1, tm)
    return pl.pallas_call(
        functools.partial(_ffn_body, tm=tm, sb=sb, nch=nch),
        grid_spec=pltpu.PrefetchScalarGridSpec(
            num_scalar_prefetch=3,
            grid=(n_tiles, nch),
            in_specs=[
                pl.BlockSpec(idx_blk, lambda t, c, te, tr, nu: (tile(t, nu), 0, 0), memory_space=pltpu.SMEM),
                pl.BlockSpec(idx_blk, lambda t, c, te, tr, nu: (tile(t + 1, nu), 0, 0), memory_space=pltpu.SMEM),
                pl.BlockSpec(memory_space=pl.ANY),
                pl.BlockSpec((1, 1, D_MODEL, fc), lambda t, c, te, tr, nu: (0, te[tile(t, nu)], 0, chunk(t, c, nu))),
                pl.BlockSpec((1, 1, fc), lambda t, c, te, tr, nu: (te[tile(t, nu)], 0, chunk(t, c, nu))),
                pl.BlockSpec((1, 1, D_MODEL, fc), lambda t, c, te, tr, nu: (0, te[tile(t, nu)], 0, chunk(t, c, nu))),
                pl.BlockSpec((1, 1, fc), lambda t, c, te, tr, nu: (te[tile(t, nu)], 0, chunk(t, c, nu))),
                pl.BlockSpec((1, 1, fc, D_MODEL), lambda t, c, te, tr, nu: (0, te[tile(t, nu)], chunk(t, c, nu), 0)),
                pl.BlockSpec((1, 1, D_MODEL), lambda t, c, te, tr, nu: (te[tile(t, nu)], 0, 0)),
            ],
            out_specs=pl.BlockSpec((tm, D_MODEL), lambda t, c, te, tr, nu: (t, 0)),
            scratch_shapes=[pltpu.VMEM((2, tm // nch, D_MODEL), F32), pltpu.SemaphoreType.DMA((2,)),
                            pltpu.VMEM((2, tm, D_MODEL), BF16),
                            pltpu.VMEM((D_MODEL, fc), BF16), pltpu.VMEM((D_MODEL, fc), BF16),
                            pltpu.VMEM((fc, D_MODEL), BF16)]),
        out_shape=jax.ShapeDtypeStruct((P, D_MODEL), F32),
        compiler_params=_cparams(2),
        name="moe_ffn",
    )(tile_e, tile_rows, n_used, row_tok.reshape(n_tiles, 1, tm), row_tok.reshape(n_tiles, 1, tm), x,
      wg, bg, wu, bu, wd, bd)


def _combine_body(posc_ref, posn_ref, x2_ref, tg_ref, ys_hbm, g_ref, b_ref, o_ref, buf, sem, *, tc, nblk):
    i = pl.program_id(0)
    slot = lax.rem(i, 2)

    def row_copy(src_row, sl, k, j):
        return pltpu.make_async_copy(ys_hbm.at[pl.ds(src_row, 1), :], buf.at[sl, k, pl.ds(j, 1), :], sem.at[sl])

    def slot_wait(sl):
        pltpu.make_async_copy(buf.at[sl], buf.at[sl], sem.at[sl]).wait()

    @pl.when(i == 0)
    def _():
        def issue(j, carry):
            for k in range(TOP_K):
                row_copy(posc_ref[0, 0, j * TOP_K + k], 0, k, j).start()
            return carry
        lax.fori_loop(0, tc, issue, 0, unroll=4)

    slot_wait(slot)
    for j in range(tc):
        for k in range(TOP_K):
            row_copy(posn_ref[0, 0, j * TOP_K + k], 1 - slot, k, j).start()
    y = None
    for k in range(TOP_K):
        term = buf[slot, k] * tg_ref[:, k:k + 1]
        y = term if y is None else y + term
    o_ref[...] = _layer_norm(DN_ALPHA * x2_ref[...] + y, g_ref[...], b_ref[...])

    @pl.when(i == nblk - 1)
    def _():
        slot_wait(1 - slot)


def _combine_ln3(x2, tg, ys, pos, ln_g, ln_b, tc):
    M = x2.shape[0]
    nblk = M // tc
    pos3 = pos.reshape(nblk, 1, tc * TOP_K)
    return pl.pallas_call(
        functools.partial(_combine_body, tc=tc, nblk=nblk),
        grid=(nblk,),
        in_specs=[pl.BlockSpec((1, 1, tc * TOP_K), lambda i: (i, 0, 0), memory_space=pltpu.SMEM),
                  pl.BlockSpec((1, 1, tc * TOP_K), lambda i: (jnp.minimum(i + 1, nblk - 1), 0, 0),
                               memory_space=pltpu.SMEM),
                  pl.BlockSpec((tc, D_MODEL), lambda i: (i, 0)),
                  pl.BlockSpec((tc, LANES), lambda i: (i, 0)),
                  pl.BlockSpec(memory_space=pl.ANY),
                  _resident(ln_g.shape), _resident(ln_b.shape)],
        out_specs=pl.BlockSpec((tc, D_MODEL), lambda i: (i, 0)),
        out_shape=jax.ShapeDtypeStruct((M, D_MODEL), F32),
        scratch_shapes=[pltpu.VMEM((2, TOP_K, tc, D_MODEL), F32), pltpu.SemaphoreType.DMA((2,))],
        compiler_params=_cparams(1),
        name="moe_combine",
    )(pos3, pos3, x2, tg, ys, ln_g, ln_b)


def _moe(x2, topi, gates, wg, bg, wu, bu, wd, bd, ln_g, ln_b, tm, tc):
    ntok = x2.shape[0]
    nk = ntok * TOP_K
    sb = min(tm, MOE_SUB_ROWS)
    n_tiles = -(-(nk + N_EXPERTS * (tm - 1)) // tm)
    e_flat = topi[:, :TOP_K].reshape(nk)
    onehot = (e_flat[:, None] == jnp.arange(N_EXPERTS, dtype=I32)[None, :]).astype(I32)
    csum = jnp.cumsum(onehot, axis=0)
    counts = csum[-1]
    rank = jnp.take_along_axis(csum, e_flat[:, None], axis=1)[:, 0] - 1
    tiles_e = (counts + tm - 1) // tm
    tile_end = jnp.cumsum(tiles_e)
    tile_start = tile_end - tiles_e
    dest = tile_start[e_flat] * tm + rank
    n_used = tile_end[-1:].astype(I32)
    t_ids = jnp.arange(n_tiles, dtype=I32)
    tile_e = jnp.minimum(jnp.searchsorted(tile_end, t_ids, side="right"), N_EXPERTS - 1).astype(I32)
    tile_rows = jnp.clip(counts[tile_e] - (t_ids - tile_start[tile_e]) * tm, 0, tm)
    tile_rows = jnp.where(t_ids < n_used[0], tile_rows, 0).astype(I32)
    P = n_tiles * tm
    row_tok = jnp.zeros((P,), I32).at[dest].set(jnp.arange(nk, dtype=I32) // TOP_K, unique_indices=True)
    ys = _moe_ffn(x2, row_tok, tile_e, tile_rows, n_used, wg, bg, wu, bu, wd, bd, tm, sb)
    return _combine_ln3(x2, gates, ys, dest, ln_g, ln_b, tc)


def _finish(x, hm, outs, lses, mem_k, mem_v, batch, w, tm_mix, tq, tm_xo, moe_tm, tc):
    M = x.shape[0]
    x1, q = _mix_ln1_q(x, hm, outs, lses, w["wmo"], w["ln1_g"], w["ln1_b"], w["wxq"], tm_mix)
    o = _xattn(q.reshape(batch, M // batch, D_MODEL), mem_k, mem_v, tq).reshape(M, D_MODEL)
    x2, topi, gates = _xo_ln2_router(x1, o, w["wxo"], w["ln2_g"], w["ln2_b"], w["wr"], w["br"], tm_xo)
    return _moe(x2, topi, gates, w["wg"], w["bg"], w["wu"], w["bu"], w["wd"], w["bd"], w["ln3_g"], w["ln3_b"],
                moe_tm, tc)


def kernel(x_prompt, x_sample, mem_prompt, state_mlstm_C, state_mlstm_n, state_mlstm_m, cache_swa_k, cache_swa_v, cache_mem_k, cache_mem_v, w_in, b_igate, b_fgate, g_mnorm, w_mix_out, ln1_g, ln1_b, w_xq, w_xk, w_xv, w_xo, ln2_g, ln2_b, w_router, b_router, w_gate, b_gate, w_up, b_up, w_down, b_down, ln3_g, ln3_b):
    assert DEPTH == 1
    B, T, _ = x_prompt.shape
    DB, S, _ = x_sample.shape
    assert B == 1
    n_mem = mem_prompt.shape[1]
    wb = cache_swa_k.shape[2]
    row2 = lambda a: a[0].reshape(1, -1)

    wi = w_in[0]
    c0 = 2 * MIX_QK + 2 * MIX_V
    w_m = wi[:, :c0].astype(BF16)
    w_g = jnp.pad(wi[:, c0:c0 + 2 * M_HEADS], ((0, 0), (0, LANES - 2 * M_HEADS))).astype(BF16)
    w_a = wi[:, c0 + 2 * M_HEADS:].astype(BF16)
    gate_bias = jnp.pad(jnp.concatenate([b_igate[0], b_fgate[0]]), (0, LANES - 2 * M_HEADS)).reshape(1, LANES)
    w = dict(
        wmo=w_mix_out[0].astype(BF16), ln1_g=row2(ln1_g), ln1_b=row2(ln1_b), wxq=w_xq[0].astype(BF16),
        wxo=w_xo[0].astype(BF16), ln2_g=row2(ln2_g), ln2_b=row2(ln2_b),
        wr=jnp.pad(w_router[0], ((0, 0), (0, LANES - N_EXPERTS))).astype(BF16),
        br=jnp.pad(b_router[0], (0, LANES - N_EXPERTS)).reshape(1, LANES),
        wg=w_gate, bg=b_gate[0].reshape(N_EXPERTS, 1, D_FF), wu=w_up, bu=b_up[0].reshape(N_EXPERTS, 1, D_FF),
        wd=w_down, bd=b_down[0].reshape(N_EXPERTS, 1, D_MODEL), ln3_g=row2(ln3_g), ln3_b=row2(ln3_b))
    gm = g_mnorm[0].reshape(1, MIX_V)

    xp = x_prompt.reshape(T, D_MODEL)
    zm = _mm(xp, w_m, BF16, 512, 1024, "proj_mlstm")
    zg = _mm(xp, w_g, F32, 512, LANES, "proj_gates")
    za = _mm(xp, w_a, BF16, 512, 1024, "proj_attn")
    wbp = min(wb, T)
    kv_tail = _mm(xp[T - wbp:], w_a[:, MIX_A:], F32, 512, 1024, "proj_kv_tail")
    zeros_c = jnp.zeros((1, M_HEADS, M_DQK, M_DV), F32)
    zeros_n = jnp.zeros((1, M_HEADS, M_DQK), F32)
    zeros_m = jnp.zeros((1, M_HEADS, LANES), F32)
    hm, pC, pn, pm = _mlstm(zm, zg, gate_bias, gm, zeros_c, zeros_n, zeros_m, 1, T // 128)
    pats = [_dilated_pattern(za, window, dil) for window, dil in PATTERNS]
    mp = mem_prompt.reshape(n_mem, D_MODEL)
    mem_k = _mm(mp, w_xk[0].astype(BF16), F32, n_mem, 1024, "mem_k")
    mem_v = _mm(mp, w_xv[0].astype(BF16), F32, n_mem, 1024, "mem_v")
    mem_k5 = mem_k.reshape(1, B, n_mem, X_HEADS, X_DH)
    mem_v5 = mem_v.reshape(1, B, n_mem, X_HEADS, X_DH)
    yp = _finish(xp, hm, [p[0] for p in pats], [p[1] for p in pats],
                 mem_k.astype(BF16).reshape(1, n_mem, D_MODEL), mem_v.astype(BF16).reshape(1, n_mem, D_MODEL),
                 1, w, tm_mix=256, tq=512, tm_xo=256, moe_tm=1024, tc=128)

    ns = DB * S
    xs_ = x_sample.reshape(ns, D_MODEL)
    zm_s = _mm(xs_, w_m, BF16, ns, 1024, "proj_mlstm_s")
    zg_s = _mm(xs_, w_g, F32, ns, LANES, "proj_gates_s")
    za_s = _mm(xs_, w_a, F32, ns, 1024, "proj_attn_s")
    pad_rows = 128 - S
    zm_pad = jnp.pad(zm_s.reshape(DB, S, -1), ((0, 0), (0, pad_rows), (0, 0))).reshape(DB * 128, -1)
    lane = jnp.arange(LANES)
    neutral = jnp.where(lane < M_HEADS, NEG_BIG, jnp.where(lane < 2 * M_HEADS, -NEG_BIG, 0.0)).astype(F32)
    zg_pad = jnp.concatenate([zg_s.reshape(DB, S, LANES), jnp.broadcast_to(neutral, (DB, pad_rows, LANES))],
                             axis=1).reshape(DB * 128, LANES)
    m0 = jnp.broadcast_to(state_mlstm_m[0][:, :, None], (DB, M_HEADS, LANES))
    hm_s, sC, sn, sm = _mlstm(zm_pad, zg_pad, gate_bias, gm, state_mlstm_C[0], state_mlstm_n[0], m0, DB, 1)
    hm_s = hm_s.reshape(DB, 128, MIX_V)[:, :S].reshape(ns, MIX_V)
    za_pad = jnp.pad(za_s.reshape(DB, S, -1), ((0, 0), (0, SWA_ROWS - S), (0, 0))).astype(BF16)
    ha_s = _swa_sample(za_pad, cache_swa_k.reshape(DB, wb * A_HEADS, A_DH),
                       cache_swa_v.reshape(DB, wb * A_HEADS, A_DH))[:, :S].reshape(ns, MIX_A)
    x1_s, q_s = _mix_ln1_q(xs_, hm_s, [ha_s], [], w["wmo"], w["ln1_g"], w["ln1_b"], w["wxq"], ns)
    q_pad = jnp.pad(q_s.reshape(DB, S, D_MODEL), ((0, 0), (0, SWA_ROWS - S), (0, 0)))
    o_s = _xattn_cache(q_pad, cache_mem_k, cache_mem_v)[:, :S].reshape(ns, D_MODEL)
    x2_s, topi_s, gates_s = _xo_ln2_router(x1_s, o_s, w["wxo"], w["ln2_g"], w["ln2_b"], w["wr"], w["br"], ns)
    ys_ = _moe(x2_s, topi_s, gates_s, w["wg"], w["bg"], w["wu"], w["bu"], w["wd"], w["bd"], w["ln3_g"], w["ln3_b"],
               128, ns)

    return (yp.reshape(B, T, D_MODEL), ys_.reshape(DB, S, D_MODEL),
            pC[None], pn[None], pm[:, :, 0][None],
            kv_tail[:, :MIX_A].reshape(1, B, wbp, A_HEADS, A_DH), kv_tail[:, MIX_A:].reshape(1, B, wbp, A_HEADS, A_DH),
            mem_k5, mem_v5,
            sC[None], sn[None], sm[:, :, 0][None],
            za_s[:, MIX_A:2 * MIX_A].reshape(1, DB, S, A_HEADS, A_DH),
            za_s[:, 2 * MIX_A:].reshape(1, DB, S, A_HEADS, A_DH))
```

```python
import functools

import jax
import jax.numpy as jnp
from jax import lax
from jax.experimental import pallas as pl
from jax.experimental.pallas import tpu as pltpu

F32, BF16, I32 = jnp.float32, jnp.bfloat16, jnp.int32

D_MODEL = 2048
DEPTH = 1
M_HEADS, M_DQK, M_DV = 4, 128, 256
HEAD_EPS = 1e-6
A_HEADS, A_DH = 8, 128
PATTERNS = ((128, 1), (512, 4), (2048, 16))
BAND_BLOCK = 128
X_HEADS = 4
X_DH = D_MODEL // X_HEADS
N_EXPERTS, TOP_K, D_FF = 32, 4, 2048
SWIGLU_LIMIT, SWIGLU_ALPHA = 7.0, 1.702
DN_ALPHA = (2 * DEPTH) ** 0.25
LN_EPS = 1e-5
MIX_V = M_HEADS * M_DV
MIX_A = A_HEADS * A_DH
MIX_QK = M_HEADS * M_DQK

LANES = 128
VMEM_LIMIT_BYTES = 58 * 1024 * 1024

NEG_BIG = -1e30


def _cparams(n_axes, vmem=VMEM_LIMIT_BYTES):
    return pltpu.CompilerParams(dimension_semantics=("arbitrary",) * n_axes, vmem_limit_bytes=vmem)


def _dot(a, b):
    return jnp.dot(a, b, preferred_element_type=F32)


def _dot_nt(a, b):
    return lax.dot_general(a, b, (((1,), (1,)), ((), ())), preferred_element_type=F32)


def _dot_tn(a, b):
    return lax.dot_general(a, b, (((0,), (0,)), ((), ())), preferred_element_type=F32)


def _log_sigmoid(x):
    return jnp.minimum(x, 0.0) - jnp.log(1.0 + jnp.exp(-jnp.abs(x)))


def _sigmoid(x):
    return 1.0 / (1.0 + jnp.exp(-x))


def _layer_norm(v, g, b):
    mu = jnp.mean(v, axis=-1, keepdims=True)
    d = v - mu
    var = jnp.mean(d * d, axis=-1, keepdims=True)
    return d * lax.rsqrt(var + LN_EPS) * g + b


def _resident(shape):
    nd = len(shape)
    return pl.BlockSpec(shape, lambda *_: (0,) * nd, pipeline_mode=pl.Buffered(1))


def _mm_body(x_ref, w_ref, o_ref):
    o_ref[...] = _dot(x_ref[...].astype(BF16), w_ref[...]).astype(o_ref.dtype)


def _mm(x, w, out_dtype, tm, tn, name):
    M, K = x.shape
    N = w.shape[1]
    assert M % tm == 0 and N % tn == 0
    return pl.pallas_call(
        _mm_body,
        grid=(M // tm, N // tn),
        in_specs=[pl.BlockSpec((tm, K), lambda i, j: (i, 0)), pl.BlockSpec((K, tn), lambda i, j: (0, j))],
        out_specs=pl.BlockSpec((tm, tn), lambda i, j: (i, j)),
        out_shape=jax.ShapeDtypeStruct((M, N), out_dtype),
        compiler_params=_cparams(2),
        name=name,
    )(x, w)


def _mlstm_body(q_ref, k_ref, v_ref, zo_ref, zg_ref, gb_ref, gm_ref, c0_ref, n0_ref, m0_ref,
                hm_ref, c_ref, n_ref, m_ref, *, L):
    @pl.when(pl.program_id(1) == 0)
    def _():
        c_ref[...] = c0_ref[...]
        n_ref[...] = n0_ref[...]
        m_ref[...] = m0_ref[...]

    scale = M_DQK ** -0.5
    g = zg_ref[...] + gb_ref[...]
    gt = g.T
    row = lax.broadcasted_iota(I32, (L, L), 0)
    col = lax.broadcasted_iota(I32, (L, L), 1)
    tri = row >= col
    for h in range(M_HEADS):
        i_col = g[:, h:h + 1]
        f_col = _log_sigmoid(g[:, M_HEADS + h:M_HEADS + h + 1])
        i_row = gt[h:h + 1, :]
        f_row = _log_sigmoid(gt[M_HEADS + h:M_HEADS + h + 1, :])
        q = q_ref[:, h * M_DQK:(h + 1) * M_DQK]
        k = k_ref[:, h * M_DQK:(h + 1) * M_DQK]
        v = v_ref[:, h * M_DV:(h + 1) * M_DV]
        zo = zo_ref[:, h * M_DV:(h + 1) * M_DV].astype(F32)
        C = c_ref[0, h]
        n = n_ref[0, h:h + 1, :]
        m = m_ref[0, h:h + 1, 0:1]
        b_col = jnp.sum(jnp.where(tri, f_row, 0.0), axis=1, keepdims=True)
        b_row = jnp.sum(jnp.where(row <= col, f_col, 0.0), axis=0, keepdims=True)
        logd = jnp.where(tri, b_col - b_row + i_row, -jnp.inf)
        inter = b_col + m
        mt = jnp.maximum(inter, jnp.max(logd, axis=1, keepdims=True))
        sd = _dot_nt(q, k) * scale * jnp.exp(logd - mt)
        sc = jnp.exp(inter - mt)
        num = _dot(sd.astype(BF16), v) + sc * _dot(q, C.astype(BF16))
        den = jnp.sum(sd, axis=1, keepdims=True) + sc * jnp.sum(q.astype(F32) * n, axis=1, keepdims=True)
        hh = num / jnp.maximum(jnp.abs(den), jnp.exp(-mt))
        hn = hh * lax.rsqrt(jnp.mean(hh * hh, axis=1, keepdims=True) + HEAD_EPS)
        out = hn * gm_ref[:, h * M_DV:(h + 1) * M_DV] * _sigmoid(zo)
        hm_ref[:, h * M_DV:(h + 1) * M_DV] = out.astype(hm_ref.dtype)
        bl = b_col[L - 1:L, :]
        ml = mt[L - 1:L, :]
        w_col = jnp.exp(bl - b_col + i_col - ml)
        scl = jnp.exp(bl + m - ml)
        kw = k.astype(F32) * (w_col * scale)
        c_ref[0, h] = scl * C + _dot_tn(kw.astype(BF16), v)
        n_ref[0, h:h + 1, :] = scl * n + jnp.sum(kw, axis=0, keepdims=True)
        m_ref[0, h:h + 1, :] = jnp.broadcast_to(ml, (1, LANES))


def _mlstm(zm, zg, gate_bias, g_mnorm, c0, n0, m0, B, nc, L=128):
    T = B * nc * L
    assert zm.shape == (T, 2 * MIX_QK + 2 * MIX_V)
    row_blk = lambda b, c: b * nc + c
    state_specs = [pl.BlockSpec((1, M_HEADS, M_DQK, M_DV), lambda b, c: (b, 0, 0, 0)),
                   pl.BlockSpec((1, M_HEADS, M_DQK), lambda b, c: (b, 0, 0)),
                   pl.BlockSpec((1, M_HEADS, LANES), lambda b, c: (b, 0, 0))]
    return pl.pallas_call(
        functools.partial(_mlstm_body, L=L),
        grid=(B, nc),
        in_specs=[pl.BlockSpec((L, MIX_QK), lambda b, c: (row_blk(b, c), 0)),
                  pl.BlockSpec((L, MIX_QK), lambda b, c: (row_blk(b, c), 1)),
                  pl.BlockSpec((L, MIX_V), lambda b, c: (row_blk(b, c), 1)),
                  pl.BlockSpec((L, MIX_V), lambda b, c: (row_blk(b, c), 2)),
                  pl.BlockSpec((L, LANES), lambda b, c: (row_blk(b, c), 0)),
                  pl.BlockSpec((1, LANES), lambda b, c: (0, 0)),
                  pl.BlockSpec((1, MIX_V), lambda b, c: (0, 0))] + state_specs,
        out_specs=[pl.BlockSpec((L, MIX_V), lambda b, c: (row_blk(b, c), 0))] + state_specs,
        out_shape=[jax.ShapeDtypeStruct((T, MIX_V), BF16),
                   jax.ShapeDtypeStruct((B, M_HEADS, M_DQK, M_DV), F32),
                   jax.ShapeDtypeStruct((B, M_HEADS, M_DQK), F32),
                   jax.ShapeDtypeStruct((B, M_HEADS, LANES), F32)],
        compiler_params=_cparams(2),
        name="mlstm",
    )(zm, zm, zm, zm, zg, gate_bias, g_mnorm, c0, n0, m0)


def _dil_body(q_ref, kc_ref, kp_ref, vc_ref, vp_ref, o_ref, l_ref, *, span):
    scale = A_DH ** -0.5
    n = pl.program_id(1)
    row = lax.broadcasted_iota(I32, (BAND_BLOCK, BAND_BLOCK), 0)
    col = lax.broadcasted_iota(I32, (BAND_BLOCK, BAND_BLOCK), 1)
    first = jnp.where(n > 0, 0, 2 * BAND_BLOCK)
    mask_p = (BAND_BLOCK + row - col + first) <= span
    mask_c = row >= col
    lane = lax.broadcasted_iota(I32, (BAND_BLOCK, LANES), 1)
    lse_tile = jnp.zeros((BAND_BLOCK, LANES), F32)
    for h in range(A_HEADS):
        sl = slice(h * A_DH, (h + 1) * A_DH)
        q = q_ref[:, sl]
        sp = jnp.where(mask_p, _dot_nt(q, kp_ref[:, sl]) * scale, -jnp.inf)
        sc = jnp.where(mask_c, _dot_nt(q, kc_ref[:, sl]) * scale, -jnp.inf)
        m = jnp.maximum(jnp.max(sp, axis=1, keepdims=True), jnp.max(sc, axis=1, keepdims=True))
        pp = jnp.exp(sp - m)
        pc = jnp.exp(sc - m)
        l = jnp.sum(pp, axis=1, keepdims=True) + jnp.sum(pc, axis=1, keepdims=True)
        o = (_dot(pp.astype(BF16), vp_ref[:, sl]) + _dot(pc.astype(BF16), vc_ref[:, sl])) / l
        o_ref[:, sl] = o.astype(o_ref.dtype)
        lse_tile = jnp.where(lane == h, m + jnp.log(l), lse_tile)
    l_ref[...] = lse_tile


def _dilated_pattern(za, window, dil):
    T = za.shape[0]
    span = window // dil
    assert T % (dil * BAND_BLOCK) == 0 and BAND_BLOCK - 1 <= span
    N = T // dil
    nb = N // BAND_BLOCK
    zv = za.reshape(N, dil * 3 * MIX_A)
    prev = lambda n: jnp.maximum(n - 1, 0)
    blk = (BAND_BLOCK, MIX_A)
    o, lse = pl.pallas_call(
        functools.partial(_dil_body, span=span),
        grid=(dil, nb),
        in_specs=[pl.BlockSpec(blk, lambda r, n: (n, 3 * r)),
                  pl.BlockSpec(blk, lambda r, n: (n, 3 * r + 1)),
                  pl.BlockSpec(blk, lambda r, n: (prev(n), 3 * r + 1)),
                  pl.BlockSpec(blk, lambda r, n: (n, 3 * r + 2)),
                  pl.BlockSpec(blk, lambda r, n: (prev(n), 3 * r + 2))],
        out_specs=[pl.BlockSpec(blk, lambda r, n: (n, r)),
                   pl.BlockSpec((BAND_BLOCK, LANES), lambda r, n: (n, r))],
        out_shape=[jax.ShapeDtypeStruct((N, dil * MIX_A), BF16),
                   jax.ShapeDtypeStruct((N, dil * LANES), F32)],
        compiler_params=_cparams(2),
        name=f"dilated_d{dil}",
    )(zv, zv, zv, zv, zv)
    return o.reshape(T, MIX_A), lse.reshape(T, LANES)


SWA_ROWS = 16


def _swa_body(q_ref, kn_ref, vn_ref, kb_ref, vb_ref, o_ref, *, wb):
    scale = A_DH ** -0.5
    s_k = lax.broadcasted_iota(I32, (SWA_ROWS, wb), 0)
    p_k = lax.broadcasted_iota(I32, (SWA_ROWS, wb), 1)
    d_k = wb + s_k - p_k
    s_n = lax.broadcasted_iota(I32, (SWA_ROWS, SWA_ROWS), 0)
    p_n = lax.broadcasted_iota(I32, (SWA_ROWS, SWA_ROWS), 1)
    d_n = s_n - p_n
    masks = []
    for window, dil in PATTERNS:
        assert dil & (dil - 1) == 0 and window % dil == 0
        mk = jnp.where(jnp.bitwise_and(d_k, dil - 1) == 0, d_k, window + 1) <= window
        mn = jnp.where(jnp.bitwise_and(d_n, dil - 1) == 0, jnp.where(d_n >= 0, d_n, window + 1), window + 1) <= window
        masks.append((mk, mn))
    for h in range(A_HEADS):
        sl = slice(h * A_DH, (h + 1) * A_DH)
        q = q_ref[0, :, sl]
        kb = kb_ref[0, pl.ds(h, wb, stride=A_HEADS), :].astype(BF16)
        vb = vb_ref[0, pl.ds(h, wb, stride=A_HEADS), :].astype(BF16)
        kn = kn_ref[0, :, sl]
        vn = vn_ref[0, :, sl]
        s_cache = _dot_nt(q, kb) * scale
        s_new = _dot_nt(q, kn) * scale
        ps, lses = [], []
        for mk, mn in masks:
            sk = jnp.where(mk, s_cache, -jnp.inf)
            sn = jnp.where(mn, s_new, -jnp.inf)
            m = jnp.maximum(jnp.max(sk, axis=1, keepdims=True), jnp.max(sn, axis=1, keepdims=True))
            pk = jnp.exp(sk - m)
            pn = jnp.exp(sn - m)
            l = jnp.sum(pk, axis=1, keepdims=True) + jnp.sum(pn, axis=1, keepdims=True)
            ps.append((pk, pn, l))
            lses.append(m + jnp.log(l))
        top = functools.reduce(jnp.maximum, lses)
        es = [jnp.exp(x - top) for x in lses]
        tot = functools.reduce(lambda a, b: a + b, es)
        pk_all = None
        pn_all = None
        for (pk, pn, l), e in zip(ps, es):
            coef = e / (tot * l)
            pk_all = pk * coef if pk_all is None else pk_all + pk * coef
            pn_all = pn * coef if pn_all is None else pn_all + pn * coef
        o = _dot(pk_all.astype(BF16), vb) + _dot(pn_all.astype(BF16), vn)
        o_ref[0, :, sl] = o.astype(o_ref.dtype)


def _swa_sample(za_pad, kbuf, vbuf):
    B, wb = kbuf.shape[0], kbuf.shape[1] // A_HEADS
    new_blk = (1, SWA_ROWS, MIX_A)
    cache_blk = (1, wb * A_HEADS, A_DH)
    return pl.pallas_call(
        functools.partial(_swa_body, wb=wb),
        grid=(B,),
        in_specs=[pl.BlockSpec(new_blk, lambda b: (b, 0, 0)),
                  pl.BlockSpec(new_blk, lambda b: (b, 0, 1)),
                  pl.BlockSpec(new_blk, lambda b: (b, 0, 2)),
                  pl.BlockSpec(cache_blk, lambda b: (b, 0, 0)),
                  pl.BlockSpec(cache_blk, lambda b: (b, 0, 0))],
        out_specs=pl.BlockSpec(new_blk, lambda b: (b, 0, 0)),
        out_shape=jax.ShapeDtypeStruct((B, SWA_ROWS, MIX_A), BF16),
        compiler_params=_cparams(1),
        name="swa_sample",
    )(za_pad, za_pad, za_pad, kbuf, vbuf)


def _mix_body(*refs, n_pat):
    x_ref, hm_ref = refs[0], refs[1]
    n_lse = n_pat if n_pat > 1 else 0
    o_refs = refs[2:2 + n_pat]
    l_refs = refs[2 + n_pat:2 + n_pat + n_lse]
    wmo_ref, g_ref, b_ref, wq_ref, x1_ref, q_ref = refs[2 + n_pat + n_lse:]
    if n_pat == 1:
        ha = o_refs[0][...]
    else:
        cols = []
        for h in range(A_HEADS):
            sl = slice(h * A_DH, (h + 1) * A_DH)
            ls = [l_ref[:, h:h + 1] for l_ref in l_refs]
            top = functools.reduce(jnp.maximum, ls)
            es = [jnp.exp(x - top) for x in ls]
            inv = 1.0 / functools.reduce(lambda a, b: a + b, es)
            acc = None
            for o_ref, e in zip(o_refs, es):
                term = o_ref[:, sl].astype(F32) * (e * inv)
                acc = term if acc is None else acc + term
            cols.append(acc.astype(BF16))
        ha = jnp.concatenate(cols, axis=1)
    mix = _dot(hm_ref[...], wmo_ref[0:MIX_V, :]) + _dot(ha, wmo_ref[MIX_V:MIX_V + MIX_A, :])
    x1 = _layer_norm(DN_ALPHA * x_ref[...] + mix, g_ref[...], b_ref[...])
    x1_ref[...] = x1
    q_ref[...] = _dot(x1.astype(BF16), wq_ref[...]).astype(q_ref.dtype)


def _mix_ln1_q(x, hm, outs, lses, wmo, ln_g, ln_b, wq, tm):
    M = x.shape[0]
    n_pat = len(outs)
    assert len(lses) == (n_pat if n_pat > 1 else 0)
    row = lambda w: pl.BlockSpec((tm, w), lambda i: (i, 0))
    return pl.pallas_call(
        functools.partial(_mix_body, n_pat=n_pat),
        grid=(M // tm,),
        in_specs=[row(D_MODEL), row(MIX_V)] + [row(MIX_A)] * n_pat + [row(LANES)] * len(lses)
        + [_resident(wmo.shape), _resident(ln_g.shape), _resident(ln_b.shape), _resident(wq.shape)],
        out_specs=[row(D_MODEL), row(D_MODEL)],
        out_shape=[jax.ShapeDtypeStruct((M, D_MODEL), F32), jax.ShapeDtypeStruct((M, D_MODEL), BF16)],
        compiler_params=_cparams(1),
        name="mix_ln1_q",
    )(x, hm, *outs, *lses, wmo, ln_g, ln_b, wq)


def _xattn_body(q_ref, k_ref, v_ref, o_ref):
    for h in range(X_HEADS):
        sl = slice(h * X_DH, (h + 1) * X_DH)
        o_ref[0, :, sl] = _xattn_head(q_ref[0, :, sl], k_ref[0, :, sl], v_ref[0, :, sl]).astype(o_ref.dtype)


def _xattn_head(q, k, v):
    s = _dot_nt(q, k.astype(BF16)) * (X_DH ** -0.5)
    p = jnp.exp(s - jnp.max(s, axis=1, keepdims=True))
    p = p / jnp.sum(p, axis=1, keepdims=True)
    return _dot(p.astype(BF16), v.astype(BF16))


def _xattn_cache_body(q_ref, k_hbm, v_hbm, o_ref, kbuf, vbuf, sem, *, nb):
    b = pl.program_id(0)
    slot = lax.rem(b, 2)

    def head_copies(bi, sl):
        cps = []
        for h in range(X_HEADS):
            cps.append(pltpu.make_async_copy(k_hbm.at[0, bi, :, h, :], kbuf.at[sl, h], sem.at[sl]))
            cps.append(pltpu.make_async_copy(v_hbm.at[0, bi, :, h, :], vbuf.at[sl, h], sem.at[sl]))
        return cps

    @pl.when(b == 0)
    def _():
        for cp in head_copies(0, 0):
            cp.start()

    for cp in head_copies(b, slot):
        cp.wait()
    nxt = jnp.minimum(b + 1, nb - 1)
    for cp in head_copies(nxt, 1 - slot):
        cp.start()
    for h in range(X_HEADS):
        sl = slice(h * X_DH, (h + 1) * X_DH)
        o_ref[0, :, sl] = _xattn_head(q_ref[0, :, sl], kbuf[slot, h], vbuf[slot, h]).astype(o_ref.dtype)

    @pl.when(b == nb - 1)
    def _():
        for cp in head_copies(nxt, 1 - slot):
            cp.wait()


def _xattn_cache(q, ck, cv):
    B, Tq, _ = q.shape
    nm = ck.shape[2]
    return pl.pallas_call(
        functools.partial(_xattn_cache_body, nb=B),
        grid=(B,),
        in_specs=[pl.BlockSpec((1, Tq, D_MODEL), lambda b: (b, 0, 0)),
                  pl.BlockSpec(memory_space=pl.ANY), pl.BlockSpec(memory_space=pl.ANY)],
        out_specs=pl.BlockSpec((1, Tq, D_MODEL), lambda b: (b, 0, 0)),
        out_shape=jax.ShapeDtypeStruct((B, Tq, D_MODEL), BF16),
        scratch_shapes=[pltpu.VMEM((2, X_HEADS, nm, X_DH), F32), pltpu.VMEM((2, X_HEADS, nm, X_DH), F32),
                        pltpu.SemaphoreType.DMA((2,))],
        compiler_params=_cparams(1),
        name="xattn_cache",
    )(q, ck, cv)


def _xattn(q, mk, mv, tq):
    B, Tq, _ = q.shape
    nm = mk.shape[1]
    mem_blk = (1, nm, D_MODEL)
    return pl.pallas_call(
        _xattn_body,
        grid=(B, Tq // tq),
        in_specs=[pl.BlockSpec((1, tq, D_MODEL), lambda b, i: (b, i, 0)),
                  pl.BlockSpec(mem_blk, lambda b, i: (b, 0, 0)),
                  pl.BlockSpec(mem_blk, lambda b, i: (b, 0, 0))],
        out_specs=pl.BlockSpec((1, tq, D_MODEL), lambda b, i: (b, i, 0)),
        out_shape=jax.ShapeDtypeStruct((B, Tq, D_MODEL), BF16),
        compiler_params=_cparams(2),
        name="xattn",
    )(q, mk, mv)


def _xo_body(x1_ref, o_ref, wo_ref, g_ref, b_ref, wr_ref, br_ref, x2_ref, ti_ref, tg_ref):
    y = _dot(o_ref[...], wo_ref[...])
    x2 = _layer_norm(DN_ALPHA * x1_ref[...] + y, g_ref[...], b_ref[...])
    x2_ref[...] = x2
    tm = x2.shape[0]
    lane = lax.broadcasted_iota(I32, (tm, LANES), 1)
    lanef = lane.astype(F32)
    logits = jnp.where(lane < N_EXPERTS, _dot(x2.astype(BF16), wr_ref[...]) + br_ref[...], -jnp.inf)
    vals, idxs = [], []
    cur = logits
    for _ in range(TOP_K):
        top = jnp.max(cur, axis=1, keepdims=True)
        idx = jnp.min(jnp.where(cur == top, lanef, float(LANES)), axis=1, keepdims=True)
        vals.append(top)
        idxs.append(idx)
        cur = jnp.where(lanef == idx, -jnp.inf, cur)
    es = [jnp.exp(v - vals[0]) for v in vals]
    inv = 1.0 / functools.reduce(lambda a, b: a + b, es)
    ti = jnp.zeros((tm, LANES), F32)
    tg = jnp.zeros((tm, LANES), F32)
    for k in range(TOP_K):
        ti = jnp.where(lane == k, idxs[k], ti)
        tg = jnp.where(lane == k, es[k] * inv, tg)
    ti_ref[...] = ti.astype(I32)
    tg_ref[...] = tg


def _xo_ln2_router(x1, o, wo, ln_g, ln_b, wr, br, tm):
    M = x1.shape[0]
    row = lambda w: pl.BlockSpec((tm, w), lambda i: (i, 0))
    return pl.pallas_call(
        _xo_body,
        grid=(M // tm,),
        in_specs=[row(D_MODEL), row(D_MODEL), _resident(wo.shape), _resident(ln_g.shape), _resident(ln_b.shape),
                  _resident(wr.shape), _resident(br.shape)],
        out_specs=[row(D_MODEL), row(LANES), row(LANES)],
        out_shape=[jax.ShapeDtypeStruct((M, D_MODEL), F32), jax.ShapeDtypeStruct((M, LANES), I32),
                   jax.ShapeDtypeStruct((M, LANES), F32)],
        compiler_params=_cparams(1),
        name="xo_ln2_router",
    )(x1, o, wo, ln_g, ln_b, wr, br)


MOE_SUB_ROWS = 256
MOE_FF_CHUNK = 256


def _ffn_body(te_ref, tr_ref, nu_ref, idc_ref, idn_ref, x_hbm, wg_ref, bg_ref, wu_ref, bu_ref, wd_ref, bd_ref,
              o_ref, stg, sem, xbb, wgb, wub, wdb, *, tm, sb, nch):
    t = pl.program_id(0)
    c = pl.program_id(1)
    n_used = nu_ref[0]
    ps = tm // nch
    cur = lax.rem(t, 2)
    ring = lax.rem(c, 2)

    def row_copy(tok, sl, r):
        return pltpu.make_async_copy(x_hbm.at[pl.ds(tok, 1), :], stg.at[sl, pl.ds(r, 1), :], sem.at[sl])

    def group_wait(sl):
        pltpu.make_async_copy(x_hbm.at[pl.ds(0, ps), :], stg.at[sl], sem.at[sl]).wait()

    def ffn_rows(rs, wg, wu, wd):
        xb = xbb[cur, rs, :]
        g = jnp.minimum(_dot(xb, wg) + bg_ref[0], SWIGLU_LIMIT)
        u = jnp.clip(_dot(xb, wu) + bu_ref[0], -SWIGLU_LIMIT, SWIGLU_LIMIT)
        hid = (u + 1.0) * (g * _sigmoid(SWIGLU_ALPHA * g))
        o_ref[rs, :] += _dot(hid.astype(BF16), wd)

    @pl.when(t >= n_used)
    def _():
        @pl.when(c == 0)
        def _():
            o_ref[...] = jnp.zeros((tm, D_MODEL), F32)

    @pl.when(t < n_used)
    def _():
        @pl.when((t == 0) & (c == 0))
        def _():
            for j in range(nch):
                def issue(r, carry):
                    row_copy(idc_ref[0, 0, j * ps + r], 0, r).start()
                    return carry
                lax.fori_loop(0, ps, issue, 0, unroll=8)
                group_wait(0)
                xbb[0, j * ps:(j + 1) * ps, :] = stg[0].astype(BF16)

        for r in range(ps):
            row_copy(idn_ref[0, 0, c * ps + r], ring, r).start()

        @pl.when((t > 0) | (c > 0))
        def _():
            prev_c = lax.rem(c + nch - 1, nch)
            dst = jnp.where(c == 0, cur, 1 - cur)
            group_wait(1 - ring)
            xbb[dst, pl.ds(pl.multiple_of(prev_c * ps, ps), ps), :] = stg[1 - ring].astype(BF16)

        rows = tr_ref[t]

        @pl.when(rows > tm - sb)
        def _():
            @pl.when(c == 0)
            def _():
                o_ref[...] = jnp.broadcast_to(bd_ref[0], (tm, D_MODEL))
            ffn_rows(slice(0, tm), wg_ref[0, 0].astype(BF16), wu_ref[0, 0].astype(BF16), wd_ref[0, 0].astype(BF16))

        @pl.when(rows <= tm - sb)
        def _():
            wgb[...] = wg_ref[0, 0].astype(BF16)
            wub[...] = wu_ref[0, 0].astype(BF16)
            wdb[...] = wd_ref[0, 0].astype(BF16)
            for s in range(tm // sb):
                rs = slice(s * sb, (s + 1) * sb)

                @pl.when((s * sb >= rows) & (c == 0))
                def _():
                    o_ref[rs, :] = jnp.zeros((sb, D_MODEL), F32)

                @pl.when(s * sb < rows)
                def _():
                    @pl.when(c == 0)
                    def _():
                        o_ref[rs, :] = jnp.broadcast_to(bd_ref[0], (sb, D_MODEL))
                    ffn_rows(rs, wgb[...], wub[...], wdb[...])

        @pl.when((t == n_used - 1) & (c == nch - 1))
        def _():
            group_wait(ring)


def _moe_ffn(x, row_tok, tile_e, tile_rows, n_used, wg, bg, wu, bu, wd, bd, tm, sb):
    P = row_tok.shape[0]
    n_tiles = P // tm
    nch = D_FF // MOE_FF_CHUNK
    fc = MOE_FF_CHUNK
    assert tm % nch == 0 and (tm // nch) % 16 == 0 and nch % 2 == 0

    def tile(t, nu):
        return jnp.minimum(t, jnp.maximum(nu[0] - 1, 0))

    def chunk(t, c, nu):
        return jnp.where(t < nu[0], c, nch - 1)

    idx_blk = (1, 1, tm)
    return pl.pallas_call(
        functools.partial(_ffn_body, tm=tm, sb=sb, nch=nch),
        grid_spec=pltpu.PrefetchScalarGridSpec(
            num_scalar_prefetch=3,
            grid=(n_tiles, nch),
            in_specs=[
                pl.BlockSpec(idx_blk, lambda t, c, te, tr, nu: (tile(t, nu), 0, 0), memory_space=pltpu.SMEM),
                pl.BlockSpec(idx_blk, lambda t, c, te, tr, nu: (tile(t + 1, nu), 0, 0), memory_space=pltpu.SMEM),
                pl.BlockSpec(memory_space=pl.ANY),
                pl.BlockSpec((1, 1, D_MODEL, fc), lambda t, c, te, tr, nu: (0, te[tile(t, nu)], 0, chunk(t, c, nu))),
                pl.BlockSpec((1, 1, fc), lambda t, c, te, tr, nu: (te[tile(t, nu)], 0, chunk(t, c, nu))),
                pl.BlockSpec((1, 1, D_MODEL, fc), lambda t, c, te, tr, nu: (0, te[tile(t, nu)], 0, chunk(t, c, nu))),
                pl.BlockSpec((1, 1, fc), lambda t, c, te, tr, nu: (te[tile(t, nu)], 0, chunk(t, c, nu))),
                pl.BlockSpec((1, 1, fc, D_MODEL), lambda t, c, te, tr, nu: (0, te[tile(t, nu)], chunk(t, c, nu), 0)),
                pl.BlockSpec((1, 1, D_MODEL), lambda t, c, te, tr, nu: (te[tile(t, nu)], 0, 0)),
            ],
            out_specs=pl.BlockSpec((tm, D_MODEL), lambda t, c, te, tr, nu: (t, 0)),
            scratch_shapes=[pltpu.VMEM((2, tm // nch, D_MODEL), F32), pltpu.SemaphoreType.DMA((2,)),
                            pltpu.VMEM((2, tm, D_MODEL), BF16),
                            pltpu.VMEM((D_MODEL, fc), BF16), pltpu.VMEM((D_MODEL, fc), BF16),
                            pltpu.VMEM((fc, D_MODEL), BF16)]),
        out_shape=jax.ShapeDtypeStruct((P, D_MODEL), F32),
        compiler_params=_cparams(2),
        name="moe_ffn",
    )(tile_e, tile_rows, n_used, row_tok.reshape(n_tiles, 1, tm), row_tok.reshape(n_tiles, 1, tm), x,
      wg, bg, wu, bu, wd, bd)


def _combine_body(posc_ref, posn_ref, x2_ref, tg_ref, ys_hbm, g_ref, b_ref, oa_ref, ob_ref, buf, sem, *,
                  tc, nblk, nblk_a):
    i = pl.program_id(0)
    slot = lax.rem(i, 2)

    def row_copy(src_row, sl, k, j):
        return pltpu.make_async_copy(ys_hbm.at[pl.ds(src_row, 1), :], buf.at[sl, k, pl.ds(j, 1), :], sem.at[sl])

    def slot_wait(sl):
        pltpu.make_async_copy(buf.at[sl], buf.at[sl], sem.at[sl]).wait()

    @pl.when(i == 0)
    def _():
        def issue(j, carry):
            for k in range(TOP_K):
                row_copy(posc_ref[0, 0, j * TOP_K + k], 0, k, j).start()
            return carry
        lax.fori_loop(0, tc, issue, 0, unroll=4)

    slot_wait(slot)
    for j in range(tc):
        for k in range(TOP_K):
            row_copy(posn_ref[0, 0, j * TOP_K + k], 1 - slot, k, j).start()
    y = None
    for k in range(TOP_K):
        term = buf[slot, k] * tg_ref[:, k:k + 1]
        y = term if y is None else y + term
    out = _layer_norm(DN_ALPHA * x2_ref[...] + y, g_ref[...], b_ref[...])

    @pl.when(i < nblk_a)
    def _():
        oa_ref[...] = out

    @pl.when(i >= nblk_a)
    def _():
        ob_ref[...] = out

    @pl.when(i == nblk - 1)
    def _():
        slot_wait(1 - slot)


def _combine_ln3(x2, tg, ys, pos, ln_g, ln_b, tc, split):
    M = pos.shape[0] // TOP_K
    nblk = M // tc
    nblk_a = split // tc
    assert split % tc == 0 and 0 < nblk_a < nblk
    pos3 = pos.reshape(nblk, 1, tc * TOP_K)
    return pl.pallas_call(
        functools.partial(_combine_body, tc=tc, nblk=nblk, nblk_a=nblk_a),
        grid=(nblk,),
        in_specs=[pl.BlockSpec((1, 1, tc * TOP_K), lambda i: (i, 0, 0), memory_space=pltpu.SMEM),
                  pl.BlockSpec((1, 1, tc * TOP_K), lambda i: (jnp.minimum(i + 1, nblk - 1), 0, 0),
                               memory_space=pltpu.SMEM),
                  pl.BlockSpec((tc, D_MODEL), lambda i: (i, 0)),
                  pl.BlockSpec((tc, LANES), lambda i: (i, 0)),
                  pl.BlockSpec(memory_space=pl.ANY),
                  _resident(ln_g.shape), _resident(ln_b.shape)],
        out_specs=[pl.BlockSpec((tc, D_MODEL), lambda i: (jnp.minimum(i, nblk_a - 1), 0)),
                   pl.BlockSpec((tc, D_MODEL), lambda i: (jnp.maximum(i - nblk_a, 0), 0))],
        out_shape=[jax.ShapeDtypeStruct((split, D_MODEL), F32), jax.ShapeDtypeStruct((M - split, D_MODEL), F32)],
        scratch_shapes=[pltpu.VMEM((2, TOP_K, tc, D_MODEL), F32), pltpu.SemaphoreType.DMA((2,))],
        compiler_params=_cparams(1),
        name="moe_combine",
    )(pos3, pos3, x2, tg, ys, ln_g, ln_b)


def _moe(x2, topi, gates, wg, bg, wu, bu, wd, bd, ln_g, ln_b, tm, tc, split):
    ntok = x2.shape[0]
    nk = ntok * TOP_K
    sb = min(tm, MOE_SUB_ROWS)
    n_tiles = -(-(nk + N_EXPERTS * (tm - 1)) // tm)
    e_flat = topi[:, :TOP_K].reshape(nk)
    onehot = (e_flat[:, None] == jnp.arange(N_EXPERTS, dtype=I32)[None, :]).astype(I32)
    csum = jnp.cumsum(onehot, axis=0)
    counts = csum[-1]
    rank = jnp.take_along_axis(csum, e_flat[:, None], axis=1)[:, 0] - 1
    tiles_e = (counts + tm - 1) // tm
    tile_end = jnp.cumsum(tiles_e)
    tile_start = tile_end - tiles_e
    dest = tile_start[e_flat] * tm + rank
    n_used = tile_end[-1:].astype(I32)
    t_ids = jnp.arange(n_tiles, dtype=I32)
    tile_e = jnp.minimum(jnp.searchsorted(tile_end, t_ids, side="right"), N_EXPERTS - 1).astype(I32)
    tile_rows = jnp.clip(counts[tile_e] - (t_ids - tile_start[tile_e]) * tm, 0, tm)
    tile_rows = jnp.where(t_ids < n_used[0], tile_rows, 0).astype(I32)
    P = n_tiles * tm
    row_tok = jnp.zeros((P,), I32).at[dest].set(jnp.arange(nk, dtype=I32) // TOP_K, unique_indices=True)
    ys = _moe_ffn(x2, row_tok, tile_e, tile_rows, n_used, wg, bg, wu, bu, wd, bd, tm, sb)
    return _combine_ln3(x2, gates, ys, dest, ln_g, ln_b, tc, split)


def kernel(x_prompt, x_sample, mem_prompt, state_mlstm_C, state_mlstm_n, state_mlstm_m, cache_swa_k, cache_swa_v, cache_mem_k, cache_mem_v, w_in, b_igate, b_fgate, g_mnorm, w_mix_out, ln1_g, ln1_b, w_xq, w_xk, w_xv, w_xo, ln2_g, ln2_b, w_router, b_router, w_gate, b_gate, w_up, b_up, w_down, b_down, ln3_g, ln3_b):
    assert DEPTH == 1
    B, T, _ = x_prompt.shape
    DB, S, _ = x_sample.shape
    assert B == 1
    n_mem = mem_prompt.shape[1]
    wb = cache_swa_k.shape[2]
    row2 = lambda a: a[0].reshape(1, -1)

    wi = w_in[0]
    c0 = 2 * MIX_QK + 2 * MIX_V
    w_m = wi[:, :c0].astype(BF16)
    w_g = jnp.pad(wi[:, c0:c0 + 2 * M_HEADS], ((0, 0), (0, LANES - 2 * M_HEADS))).astype(BF16)
    w_a = wi[:, c0 + 2 * M_HEADS:].astype(BF16)
    gate_bias = jnp.pad(jnp.concatenate([b_igate[0], b_fgate[0]]), (0, LANES - 2 * M_HEADS)).reshape(1, LANES)
    w = dict(
        wmo=w_mix_out[0].astype(BF16), ln1_g=row2(ln1_g), ln1_b=row2(ln1_b), wxq=w_xq[0].astype(BF16),
        wxo=w_xo[0].astype(BF16), ln2_g=row2(ln2_g), ln2_b=row2(ln2_b),
        wr=jnp.pad(w_router[0], ((0, 0), (0, LANES - N_EXPERTS))).astype(BF16),
        br=jnp.pad(b_router[0], (0, LANES - N_EXPERTS)).reshape(1, LANES),
        wg=w_gate, bg=b_gate[0].reshape(N_EXPERTS, 1, D_FF), wu=w_up, bu=b_up[0].reshape(N_EXPERTS, 1, D_FF),
        wd=w_down, bd=b_down[0].reshape(N_EXPERTS, 1, D_MODEL), ln3_g=row2(ln3_g), ln3_b=row2(ln3_b))
    gm = g_mnorm[0].reshape(1, MIX_V)

    xp = x_prompt.reshape(T, D_MODEL)
    zm = _mm(xp, w_m, BF16, 512, 1024, "proj_mlstm")
    zg = _mm(xp, w_g, F32, 512, LANES, "proj_gates")
    za = _mm(xp, w_a, BF16, 512, 1024, "proj_attn")
    wbp = min(wb, T)
    kv_tail = _mm(xp[T - wbp:], w_a[:, MIX_A:], F32, 512, 1024, "proj_kv_tail")
    zeros_c = jnp.zeros((1, M_HEADS, M_DQK, M_DV), F32)
    zeros_n = jnp.zeros((1, M_HEADS, M_DQK), F32)
    zeros_m = jnp.zeros((1, M_HEADS, LANES), F32)
    hm, pC, pn, pm = _mlstm(zm, zg, gate_bias, gm, zeros_c, zeros_n, zeros_m, 1, T // 128)
    pats = [_dilated_pattern(za, window, dil) for window, dil in PATTERNS]
    mp = mem_prompt.reshape(n_mem, D_MODEL)
    mem_k = _mm(mp, w_xk[0].astype(BF16), F32, n_mem, 1024, "mem_k")
    mem_v = _mm(mp, w_xv[0].astype(BF16), F32, n_mem, 1024, "mem_v")
    mem_k5 = mem_k.reshape(1, B, n_mem, X_HEADS, X_DH)
    mem_v5 = mem_v.reshape(1, B, n_mem, X_HEADS, X_DH)
    x1_p, q_p = _mix_ln1_q(xp, hm, [p[0] for p in pats], [p[1] for p in pats], w["wmo"], w["ln1_g"], w["ln1_b"],
                           w["wxq"], 256)
    o_p = _xattn(q_p.reshape(1, T, D_MODEL), mem_k.astype(BF16).reshape(1, n_mem, D_MODEL),
                 mem_v.astype(BF16).reshape(1, n_mem, D_MODEL), 512).reshape(T, D_MODEL)
    x2_p, topi_p, gates_p = _xo_ln2_router(x1_p, o_p, w["wxo"], w["ln2_g"], w["ln2_b"], w["wr"], w["br"], 256)

    ns = DB * S
    xs_ = x_sample.reshape(ns, D_MODEL)
    zm_s = _mm(xs_, w_m, BF16, ns, 1024, "proj_mlstm_s")
    zg_s = _mm(xs_, w_g, F32, ns, LANES, "proj_gates_s")
    za_s = _mm(xs_, w_a, F32, ns, 1024, "proj_attn_s")
    pad_rows = 128 - S
    zm_pad = jnp.pad(zm_s.reshape(DB, S, -1), ((0, 0), (0, pad_rows), (0, 0))).reshape(DB * 128, -1)
    lane = jnp.arange(LANES)
    neutral = jnp.where(lane < M_HEADS, NEG_BIG, jnp.where(lane < 2 * M_HEADS, -NEG_BIG, 0.0)).astype(F32)
    zg_pad = jnp.concatenate([zg_s.reshape(DB, S, LANES), jnp.broadcast_to(neutral, (DB, pad_rows, LANES))],
                             axis=1).reshape(DB * 128, LANES)
    m0 = jnp.broadcast_to(state_mlstm_m[0][:, :, None], (DB, M_HEADS, LANES))
    hm_s, sC, sn, sm = _mlstm(zm_pad, zg_pad, gate_bias, gm, state_mlstm_C[0], state_mlstm_n[0], m0, DB, 1)
    hm_s = hm_s.reshape(DB, 128, MIX_V)[:, :S].reshape(ns, MIX_V)
    za_pad = jnp.pad(za_s.reshape(DB, S, -1), ((0, 0), (0, SWA_ROWS - S), (0, 0))).astype(BF16)
    ha_s = _swa_sample(za_pad, cache_swa_k.reshape(DB, wb * A_HEADS, A_DH),
                       cache_swa_v.reshape(DB, wb * A_HEADS, A_DH))[:, :S].reshape(ns, MIX_A)
    x1_s, q_s = _mix_ln1_q(xs_, hm_s, [ha_s], [], w["wmo"], w["ln1_g"], w["ln1_b"], w["wxq"], ns)
    q_pad = jnp.pad(q_s.reshape(DB, S, D_MODEL), ((0, 0), (0, SWA_ROWS - S), (0, 0)))
    o_s = _xattn_cache(q_pad, cache_mem_k, cache_mem_v)[:, :S].reshape(ns, D_MODEL)
    x2_s, topi_s, gates_s = _xo_ln2_router(x1_s, o_s, w["wxo"], w["ln2_g"], w["ln2_b"], w["wr"], w["br"], ns)

    yp, ys_ = _moe(jnp.concatenate([x2_p, x2_s]), jnp.concatenate([topi_p, topi_s]),
                   jnp.concatenate([gates_p, gates_s]), w["wg"], w["bg"], w["wu"], w["bu"], w["wd"], w["bd"],
                   w["ln3_g"], w["ln3_b"], 1024, 128, T)

    return (yp.reshape(B, T, D_MODEL), ys_.reshape(DB, S, D_MODEL),
            pC[None], pn[None], pm[:, :, 0][None],
            kv_tail[:, :MIX_A].reshape(1, B, wbp, A_HEADS, A_DH), kv_tail[:, MIX_A:].reshape(1, B, wbp, A_HEADS, A_DH),
            mem_k5, mem_v5,
            sC[None], sn[None], sm[:, :, 0][None],
            za_s[:, MIX_A:2 * MIX_A].reshape(1, DB, S, A_HEADS, A_DH),
            za_s[:, 2 * MIX_A:].reshape(1, DB, S, A_HEADS, A_DH))
```

```python
import functools

import jax
import jax.numpy as jnp
from jax import lax
from jax.experimental import pallas as pl
from jax.experimental.pallas import tpu as pltpu

F32, BF16, I32 = jnp.float32, jnp.bfloat16, jnp.int32

D_MODEL = 2048
DEPTH = 1
M_HEADS, M_DQK, M_DV = 4, 128, 256
HEAD_EPS = 1e-6
A_HEADS, A_DH = 8, 128
PATTERNS = ((128, 1), (512, 4), (2048, 16))
BAND_BLOCK = 128
X_HEADS = 4
X_DH = D_MODEL // X_HEADS
N_EXPERTS, TOP_K, D_FF = 32, 4, 2048
SWIGLU_LIMIT, SWIGLU_ALPHA = 7.0, 1.702
DN_ALPHA = (2 * DEPTH) ** 0.25
LN_EPS = 1e-5
MIX_V = M_HEADS * M_DV
MIX_A = A_HEADS * A_DH
MIX_QK = M_HEADS * M_DQK

LANES = 128
VMEM_LIMIT_BYTES = 58 * 1024 * 1024

NEG_BIG = -1e30


def _cparams(n_axes, vmem=VMEM_LIMIT_BYTES):
    return pltpu.CompilerParams(dimension_semantics=("arbitrary",) * n_axes, vmem_limit_bytes=vmem)


def _dot(a, b):
    return jnp.dot(a, b, preferred_element_type=F32)


def _dot_nt(a, b):
    return lax.dot_general(a, b, (((1,), (1,)), ((), ())), preferred_element_type=F32)


def _dot_tn(a, b):
    return lax.dot_general(a, b, (((0,), (0,)), ((), ())), preferred_element_type=F32)


def _log_sigmoid(x):
    return jnp.minimum(x, 0.0) - jnp.log(1.0 + jnp.exp(-jnp.abs(x)))


def _sigmoid(x):
    return 1.0 / (1.0 + jnp.exp(-x))


def _layer_norm(v, g, b):
    mu = jnp.mean(v, axis=-1, keepdims=True)
    d = v - mu
    var = jnp.mean(d * d, axis=-1, keepdims=True)
    return d * lax.rsqrt(var + LN_EPS) * g + b


def _resident(shape):
    nd = len(shape)
    return pl.BlockSpec(shape, lambda *_: (0,) * nd, pipeline_mode=pl.Buffered(1))


def _mm_body(x_ref, w_ref, o_ref):
    o_ref[...] = _dot(x_ref[...].astype(BF16), w_ref[...]).astype(o_ref.dtype)


def _mm(x, w, out_dtype, tm, tn, name):
    M, K = x.shape
    N = w.shape[1]
    assert M % tm == 0 and N % tn == 0
    return pl.pallas_call(
        _mm_body,
        grid=(M // tm, N // tn),
        in_specs=[pl.BlockSpec((tm, K), lambda i, j: (i, 0)), pl.BlockSpec((K, tn), lambda i, j: (0, j))],
        out_specs=pl.BlockSpec((tm, tn), lambda i, j: (i, j)),
        out_shape=jax.ShapeDtypeStruct((M, N), out_dtype),
        compiler_params=_cparams(2),
        name=name,
    )(x, w)


def _mlstm_body(q_ref, k_ref, v_ref, zo_ref, zg_ref, gb_ref, gm_ref, c0_ref, n0_ref, m0_ref,
                hm_ref, c_ref, n_ref, m_ref, *, L):
    @pl.when(pl.program_id(1) == 0)
    def _():
        c_ref[...] = c0_ref[...]
        n_ref[...] = n0_ref[...]
        m_ref[...] = m0_ref[...]

    scale = M_DQK ** -0.5
    g = zg_ref[...] + gb_ref[...]
    gt = g.T
    row = lax.broadcasted_iota(I32, (L, L), 0)
    col = lax.broadcasted_iota(I32, (L, L), 1)
    tri = row >= col
    for h in range(M_HEADS):
        i_col = g[:, h:h + 1]
        f_col = _log_sigmoid(g[:, M_HEADS + h:M_HEADS + h + 1])
        i_row = gt[h:h + 1, :]
        f_row = _log_sigmoid(gt[M_HEADS + h:M_HEADS + h + 1, :])
        q = q_ref[:, h * M_DQK:(h + 1) * M_DQK]
        k = k_ref[:, h * M_DQK:(h + 1) * M_DQK]
        v = v_ref[:, h * M_DV:(h + 1) * M_DV]
        zo = zo_ref[:, h * M_DV:(h + 1) * M_DV].astype(F32)
        C = c_ref[0, h]
        n = n_ref[0, h:h + 1, :]
        m = m_ref[0, h:h + 1, 0:1]
        b_col = jnp.sum(jnp.where(tri, f_row, 0.0), axis=1, keepdims=True)
        b_row = jnp.sum(jnp.where(row <= col, f_col, 0.0), axis=0, keepdims=True)
        logd = jnp.where(tri, b_col - b_row + i_row, -jnp.inf)
        inter = b_col + m
        mt = jnp.maximum(inter, jnp.max(logd, axis=1, keepdims=True))
        sd = _dot_nt(q, k) * scale * jnp.exp(logd - mt)
        sc = jnp.exp(inter - mt)
        num = _dot(sd.astype(BF16), v) + sc * _dot(q, C.astype(BF16))
        den = jnp.sum(sd, axis=1, keepdims=True) + sc * jnp.sum(q.astype(F32) * n, axis=1, keepdims=True)
        hh = num / jnp.maximum(jnp.abs(den), jnp.exp(-mt))
        hn = hh * lax.rsqrt(jnp.mean(hh * hh, axis=1, keepdims=True) + HEAD_EPS)
        out = hn * gm_ref[:, h * M_DV:(h + 1) * M_DV] * _sigmoid(zo)
        hm_ref[:, h * M_DV:(h + 1) * M_DV] = out.astype(hm_ref.dtype)
        bl = b_col[L - 1:L, :]
        ml = mt[L - 1:L, :]
        w_col = jnp.exp(bl - b_col + i_col - ml)
        scl = jnp.exp(bl + m - ml)
        kw = k.astype(F32) * (w_col * scale)
        c_ref[0, h] = scl * C + _dot_tn(kw.astype(BF16), v)
        n_ref[0, h:h + 1, :] = scl * n + jnp.sum(kw, axis=0, keepdims=True)
        m_ref[0, h:h + 1, :] = jnp.broadcast_to(ml, (1, LANES))


def _mlstm(zm, zg, gate_bias, g_mnorm, c0, n0, m0, B, nc, L=128):
    T = B * nc * L
    assert zm.shape == (T, 2 * MIX_QK + 2 * MIX_V)
    row_blk = lambda b, c: b * nc + c
    state_specs = [pl.BlockSpec((1, M_HEADS, M_DQK, M_DV), lambda b, c: (b, 0, 0, 0)),
                   pl.BlockSpec((1, M_HEADS, M_DQK), lambda b, c: (b, 0, 0)),
                   pl.BlockSpec((1, M_HEADS, LANES), lambda b, c: (b, 0, 0))]
    return pl.pallas_call(
        functools.partial(_mlstm_body, L=L),
        grid=(B, nc),
        in_specs=[pl.BlockSpec((L, MIX_QK), lambda b, c: (row_blk(b, c), 0)),
                  pl.BlockSpec((L, MIX_QK), lambda b, c: (row_blk(b, c), 1)),
                  pl.BlockSpec((L, MIX_V), lambda b, c: (row_blk(b, c), 1)),
                  pl.BlockSpec((L, MIX_V), lambda b, c: (row_blk(b, c), 2)),
                  pl.BlockSpec((L, LANES), lambda b, c: (row_blk(b, c), 0)),
                  pl.BlockSpec((1, LANES), lambda b, c: (0, 0)),
                  pl.BlockSpec((1, MIX_V), lambda b, c: (0, 0))] + state_specs,
        out_specs=[pl.BlockSpec((L, MIX_V), lambda b, c: (row_blk(b, c), 0))] + state_specs,
        out_shape=[jax.ShapeDtypeStruct((T, MIX_V), BF16),
                   jax.ShapeDtypeStruct((B, M_HEADS, M_DQK, M_DV), F32),
                   jax.ShapeDtypeStruct((B, M_HEADS, M_DQK), F32),
                   jax.ShapeDtypeStruct((B, M_HEADS, LANES), F32)],
        compiler_params=_cparams(2),
        name="mlstm",
    )(zm, zm, zm, zm, zg, gate_bias, g_mnorm, c0, n0, m0)


def _dil_body(q_ref, kc_ref, kp_ref, vc_ref, vp_ref, o_ref, l_ref, *, span):
    scale = A_DH ** -0.5
    n = pl.program_id(1)
    row = lax.broadcasted_iota(I32, (BAND_BLOCK, BAND_BLOCK), 0)
    col = lax.broadcasted_iota(I32, (BAND_BLOCK, BAND_BLOCK), 1)
    first = jnp.where(n > 0, 0, 2 * BAND_BLOCK)
    mask_p = (BAND_BLOCK + row - col + first) <= span
    mask_c = row >= col
    lane = lax.broadcasted_iota(I32, (BAND_BLOCK, LANES), 1)
    lse_tile = jnp.zeros((BAND_BLOCK, LANES), F32)
    ones = jnp.ones((BAND_BLOCK, A_DH), BF16)
    for h in range(A_HEADS):
        sl = slice(h * A_DH, (h + 1) * A_DH)
        q = q_ref[:, sl]
        sp = jnp.where(mask_p, _dot_nt(q, kp_ref[:, sl]) * scale, -jnp.inf)
        sc = jnp.where(mask_c, _dot_nt(q, kc_ref[:, sl]) * scale, -jnp.inf)
        m = jnp.max(jnp.maximum(sp, sc), axis=1, keepdims=True)
        pp = jnp.exp(sp - m).astype(BF16)
        pc = jnp.exp(sc - m).astype(BF16)
        oa = (_dot(pp, jnp.concatenate([vp_ref[:, sl], ones], axis=1))
              + _dot(pc, jnp.concatenate([vc_ref[:, sl], ones], axis=1)))
        l = oa[:, A_DH:A_DH + 1]
        o_ref[:, sl] = (oa[:, :A_DH] / l).astype(o_ref.dtype)
        lse_tile = jnp.where(lane == h, m + jnp.log(l), lse_tile)
    l_ref[...] = lse_tile


PROJ_ATTN_ROWS = 512


def _proj_attn_body(x_ref, w_ref, *refs, dils):
    o_refs, acc = refs[:len(dils)], refs[len(dils)]
    tm = x_ref.shape[0]
    nslab = acc.shape[0]
    tn = nslab * LANES
    xb = x_ref[...].astype(BF16)
    for j in range(w_ref.shape[1] // tn):
        z = _dot(xb, w_ref[:, j * tn:(j + 1) * tn])
        for dil, o_ref in zip(dils, o_refs):
            if dil == 1:
                o_ref[:, j * tn:(j + 1) * tn] = z.astype(o_ref.dtype)
        for s in range(nslab):
            acc[s] = z[:, s * LANES:(s + 1) * LANES]
        for dil, o_ref in zip(dils, o_refs):
            if dil == 1:
                continue
            rows = tm // dil
            for r in range(dil):
                for s in range(nslab):
                    c0 = r * w_ref.shape[1] + j * tn + s * LANES
                    o_ref[:, c0:c0 + LANES] = acc[s, pl.ds(r, rows, stride=dil), :].astype(o_ref.dtype)


def _proj_attn(x, w, dils):
    T, K = x.shape
    N = w.shape[1]
    tm, tn = PROJ_ATTN_ROWS, 1024
    assert T % tm == 0 and N % tn == 0 and all(tm % (16 * d) == 0 for d in dils)
    return pl.pallas_call(
        functools.partial(_proj_attn_body, dils=dils),
        grid=(T // tm,),
        in_specs=[pl.BlockSpec((tm, K), lambda i: (i, 0)), _resident(w.shape)],
        out_specs=[pl.BlockSpec((tm // d, d * N), lambda i: (i, 0)) for d in dils],
        out_shape=[jax.ShapeDtypeStruct((T // d, d * N), BF16) for d in dils],
        scratch_shapes=[pltpu.VMEM((tn // LANES, tm, LANES), F32)],
        compiler_params=_cparams(1),
        name="proj_attn",
    )(x, w)


def _dilated_pattern(zv, window, dil):
    N = zv.shape[0]
    span = window // dil
    assert zv.shape[1] == dil * 3 * MIX_A and N % BAND_BLOCK == 0 and BAND_BLOCK - 1 <= span
    nb = N // BAND_BLOCK
    prev = lambda n: jnp.maximum(n - 1, 0)
    blk = (BAND_BLOCK, MIX_A)
    o, lse = pl.pallas_call(
        functools.partial(_dil_body, span=span),
        grid=(dil, nb),
        in_specs=[pl.BlockSpec(blk, lambda r, n: (n, 3 * r)),
                  pl.BlockSpec(blk, lambda r, n: (n, 3 * r + 1)),
                  pl.BlockSpec(blk, lambda r, n: (prev(n), 3 * r + 1)),
                  pl.BlockSpec(blk, lambda r, n: (n, 3 * r + 2)),
                  pl.BlockSpec(blk, lambda r, n: (prev(n), 3 * r + 2))],
        out_specs=[pl.BlockSpec(blk, lambda r, n: (n, r)),
                   pl.BlockSpec((BAND_BLOCK, LANES), lambda r, n: (n, r))],
        out_shape=[jax.ShapeDtypeStruct((N, dil * MIX_A), BF16),
                   jax.ShapeDtypeStruct((N, dil * LANES), F32)],
        compiler_params=_cparams(2),
        name=f"dilated_d{dil}",
    )(zv, zv, zv, zv, zv)
    return o, lse


SWA_ROWS = 16


def _swa_body(q_ref, kn_ref, vn_ref, kb_ref, vb_ref, o_ref, *, wb):
    scale = A_DH ** -0.5
    s_k = lax.broadcasted_iota(I32, (SWA_ROWS, wb), 0)
    p_k = lax.broadcasted_iota(I32, (SWA_ROWS, wb), 1)
    d_k = wb + s_k - p_k
    s_n = lax.broadcasted_iota(I32, (SWA_ROWS, SWA_ROWS), 0)
    p_n = lax.broadcasted_iota(I32, (SWA_ROWS, SWA_ROWS), 1)
    d_n = s_n - p_n
    masks = []
    for window, dil in PATTERNS:
        assert dil & (dil - 1) == 0 and window % dil == 0
        mk = jnp.where(jnp.bitwise_and(d_k, dil - 1) == 0, d_k, window + 1) <= window
        mn = jnp.where(jnp.bitwise_and(d_n, dil - 1) == 0, jnp.where(d_n >= 0, d_n, window + 1), window + 1) <= window
        masks.append((mk, mn))
    for h in range(A_HEADS):
        sl = slice(h * A_DH, (h + 1) * A_DH)
        q = q_ref[0, :, sl]
        kb = kb_ref[0, pl.ds(h, wb, stride=A_HEADS), :].astype(BF16)
        vb = vb_ref[0, pl.ds(h, wb, stride=A_HEADS), :].astype(BF16)
        kn = kn_ref[0, :, sl]
        vn = vn_ref[0, :, sl]
        s_cache = _dot_nt(q, kb) * scale
        s_new = _dot_nt(q, kn) * scale
        ps, lses = [], []
        for mk, mn in masks:
            sk = jnp.where(mk, s_cache, -jnp.inf)
            sn = jnp.where(mn, s_new, -jnp.inf)
            m = jnp.maximum(jnp.max(sk, axis=1, keepdims=True), jnp.max(sn, axis=1, keepdims=True))
            pk = jnp.exp(sk - m)
            pn = jnp.exp(sn - m)
            l = jnp.sum(pk, axis=1, keepdims=True) + jnp.sum(pn, axis=1, keepdims=True)
            ps.append((pk, pn, l))
            lses.append(m + jnp.log(l))
        top = functools.reduce(jnp.maximum, lses)
        es = [jnp.exp(x - top) for x in lses]
        tot = functools.reduce(lambda a, b: a + b, es)
        pk_all = None
        pn_all = None
        for (pk, pn, l), e in zip(ps, es):
            coef = e / (tot * l)
            pk_all = pk * coef if pk_all is None else pk_all + pk * coef
            pn_all = pn * coef if pn_all is None else pn_all + pn * coef
        o = _dot(pk_all.astype(BF16), vb) + _dot(pn_all.astype(BF16), vn)
        o_ref[0, :, sl] = o.astype(o_ref.dtype)


def _swa_sample(za_pad, kbuf, vbuf):
    B, wb = kbuf.shape[0], kbuf.shape[1] // A_HEADS
    new_blk = (1, SWA_ROWS, MIX_A)
    cache_blk = (1, wb * A_HEADS, A_DH)
    return pl.pallas_call(
        functools.partial(_swa_body, wb=wb),
        grid=(B,),
        in_specs=[pl.BlockSpec(new_blk, lambda b: (b, 0, 0)),
                  pl.BlockSpec(new_blk, lambda b: (b, 0, 1)),
                  pl.BlockSpec(new_blk, lambda b: (b, 0, 2)),
                  pl.BlockSpec(cache_blk, lambda b: (b, 0, 0)),
                  pl.BlockSpec(cache_blk, lambda b: (b, 0, 0))],
        out_specs=pl.BlockSpec(new_blk, lambda b: (b, 0, 0)),
        out_shape=jax.ShapeDtypeStruct((B, SWA_ROWS, MIX_A), BF16),
        compiler_params=_cparams(1),
        name="swa_sample",
    )(za_pad, za_pad, za_pad, kbuf, vbuf)


def _mix_body(*refs, dils):
    n_pat = len(dils)
    x_ref, hm_ref = refs[0], refs[1]
    n_lse = n_pat if n_pat > 1 else 0
    o_refs = refs[2:2 + n_pat]
    l_refs = refs[2 + n_pat:2 + n_pat + n_lse]
    wmo_ref, g_ref, b_ref, wq_ref, x1_ref, q_ref = refs[2 + n_pat + n_lse:8 + n_pat + n_lse]
    if n_pat == 1:
        ha = o_refs[0][...]
    else:
        o_nat, l_nat = refs[8 + n_pat + n_lse:]
        tm = x_ref.shape[0]
        for p, dil in enumerate(dils):
            if dil == 1:
                continue
            rows = tm // dil
            for r in range(dil):
                dst = pl.ds(r, rows, stride=dil)
                l_nat[p, dst, :] = l_refs[p][:, r * LANES:(r + 1) * LANES]
                for h in range(A_HEADS):
                    c0 = r * MIX_A + h * A_DH
                    o_nat[p, h, dst, :] = o_refs[p][:, c0:c0 + A_DH].astype(F32)

        def lse_of(p, h):
            return l_refs[p][:, h:h + 1] if dils[p] == 1 else l_nat[p, :, h:h + 1]

        def out_of(p, h):
            return o_refs[p][:, h * A_DH:(h + 1) * A_DH].astype(F32) if dils[p] == 1 else o_nat[p, h]

        cols = []
        for h in range(A_HEADS):
            ls = [lse_of(p, h) for p in range(n_pat)]
            top = functools.reduce(jnp.maximum, ls)
            es = [jnp.exp(x - top) for x in ls]
            inv = 1.0 / functools.reduce(lambda a, b: a + b, es)
            acc = None
            for p, e in enumerate(es):
                term = out_of(p, h) * (e * inv)
                acc = term if acc is None else acc + term
            cols.append(acc.astype(BF16))
        ha = jnp.concatenate(cols, axis=1)
    mix = _dot(hm_ref[...], wmo_ref[0:MIX_V, :]) + _dot(ha, wmo_ref[MIX_V:MIX_V + MIX_A, :])
    x1 = _layer_norm(DN_ALPHA * x_ref[...] + mix, g_ref[...], b_ref[...])
    x1_ref[...] = x1
    q_ref[...] = _dot(x1.astype(BF16), wq_ref[...]).astype(q_ref.dtype)


def _mix_ln1_q(x, hm, outs, lses, dils, wmo, ln_g, ln_b, wq, tm):
    M = x.shape[0]
    n_pat = len(outs)
    assert len(lses) == (n_pat if n_pat > 1 else 0) and len(dils) == n_pat
    assert all(tm % (16 * d) == 0 for d in dils)
    row = lambda w: pl.BlockSpec((tm, w), lambda i: (i, 0))
    view = lambda w, d: pl.BlockSpec((tm // d, d * w), lambda i: (i, 0))
    scratch = []
    if n_pat > 1:
        scratch = [pltpu.VMEM((n_pat, A_HEADS, tm, A_DH), F32), pltpu.VMEM((n_pat, tm, LANES), F32)]
    return pl.pallas_call(
        functools.partial(_mix_body, dils=tuple(dils)),
        grid=(M // tm,),
        in_specs=[row(D_MODEL), row(MIX_V)] + [view(MIX_A, d) for d in dils]
        + [view(LANES, d) for d in dils[:len(lses)]]
        + [_resident(wmo.shape), _resident(ln_g.shape), _resident(ln_b.shape), _resident(wq.shape)],
        out_specs=[row(D_MODEL), row(D_MODEL)],
        out_shape=[jax.ShapeDtypeStruct((M, D_MODEL), F32), jax.ShapeDtypeStruct((M, D_MODEL), BF16)],
        scratch_shapes=scratch,
        compiler_params=_cparams(1),
        name="mix_ln1_q",
    )(x, hm, *outs, *lses, wmo, ln_g, ln_b, wq)


def _xattn_body(q_ref, k_ref, v_ref, o_ref):
    for h in range(X_HEADS):
        sl = slice(h * X_DH, (h + 1) * X_DH)
        o_ref[0, :, sl] = _xattn_head(q_ref[0, :, sl], k_ref[0, :, sl], v_ref[0, :, sl]).astype(o_ref.dtype)


def _xattn_head(q, k, v):
    s = _dot_nt(q, k.astype(BF16)) * (X_DH ** -0.5)
    p = jnp.exp(s - jnp.max(s, axis=1, keepdims=True))
    p = p / jnp.sum(p, axis=1, keepdims=True)
    return _dot(p.astype(BF16), v.astype(BF16))


def _xattn_cache_body(q_ref, k_hbm, v_hbm, o_ref, kbuf, vbuf, sem, *, nb):
    b = pl.program_id(0)
    slot = lax.rem(b, 2)

    def head_copies(bi, sl):
        cps = []
        for h in range(X_HEADS):
            cps.append(pltpu.make_async_copy(k_hbm.at[0, bi, :, h, :], kbuf.at[sl, h], sem.at[sl]))
            cps.append(pltpu.make_async_copy(v_hbm.at[0, bi, :, h, :], vbuf.at[sl, h], sem.at[sl]))
        return cps

    @pl.when(b == 0)
    def _():
        for cp in head_copies(0, 0):
            cp.start()

    for cp in head_copies(b, slot):
        cp.wait()
    nxt = jnp.minimum(b + 1, nb - 1)
    for cp in head_copies(nxt, 1 - slot):
        cp.start()
    for h in range(X_HEADS):
        sl = slice(h * X_DH, (h + 1) * X_DH)
        o_ref[0, :, sl] = _xattn_head(q_ref[0, :, sl], kbuf[slot, h], vbuf[slot, h]).astype(o_ref.dtype)

    @pl.when(b == nb - 1)
    def _():
        for cp in head_copies(nxt, 1 - slot):
            cp.wait()


def _xattn_cache(q, ck, cv):
    B, Tq, _ = q.shape
    nm = ck.shape[2]
    return pl.pallas_call(
        functools.partial(_xattn_cache_body, nb=B),
        grid=(B,),
        in_specs=[pl.BlockSpec((1, Tq, D_MODEL), lambda b: (b, 0, 0)),
                  pl.BlockSpec(memory_space=pl.ANY), pl.BlockSpec(memory_space=pl.ANY)],
        out_specs=pl.BlockSpec((1, Tq, D_MODEL), lambda b: (b, 0, 0)),
        out_shape=jax.ShapeDtypeStruct((B, Tq, D_MODEL), BF16),
        scratch_shapes=[pltpu.VMEM((2, X_HEADS, nm, X_DH), F32), pltpu.VMEM((2, X_HEADS, nm, X_DH), F32),
                        pltpu.SemaphoreType.DMA((2,))],
        compiler_params=_cparams(1),
        name="xattn_cache",
    )(q, ck, cv)


def _xattn(q, mk, mv, tq):
    B, Tq, _ = q.shape
    nm = mk.shape[1]
    mem_blk = (1, nm, D_MODEL)
    return pl.pallas_call(
        _xattn_body,
        grid=(B, Tq // tq),
        in_specs=[pl.BlockSpec((1, tq, D_MODEL), lambda b, i: (b, i, 0)),
                  pl.BlockSpec(mem_blk, lambda b, i: (b, 0, 0)),
                  pl.BlockSpec(mem_blk, lambda b, i: (b, 0, 0))],
        out_specs=pl.BlockSpec((1, tq, D_MODEL), lambda b, i: (b, i, 0)),
        out_shape=jax.ShapeDtypeStruct((B, Tq, D_MODEL), BF16),
        compiler_params=_cparams(2),
        name="xattn",
    )(q, mk, mv)


def _xo_body(x1_ref, o_ref, wo_ref, g_ref, b_ref, wr_ref, br_ref, x2_ref, ti_ref, tg_ref):
    y = _dot(o_ref[...], wo_ref[...])
    x2 = _layer_norm(DN_ALPHA * x1_ref[...] + y, g_ref[...], b_ref[...])
    x2_ref[...] = x2
    tm = x2.shape[0]
    lane = lax.broadcasted_iota(I32, (tm, LANES), 1)
    lanef = lane.astype(F32)
    logits = jnp.where(lane < N_EXPERTS, _dot(x2.astype(BF16), wr_ref[...]) + br_ref[...], -jnp.inf)
    vals, idxs = [], []
    cur = logits
    for _ in range(TOP_K):
        top = jnp.max(cur, axis=1, keepdims=True)
        idx = jnp.min(jnp.where(cur == top, lanef, float(LANES)), axis=1, keepdims=True)
        vals.append(top)
        idxs.append(idx)
        cur = jnp.where(lanef == idx, -jnp.inf, cur)
    es = [jnp.exp(v - vals[0]) for v in vals]
    inv = 1.0 / functools.reduce(lambda a, b: a + b, es)
    ti = jnp.zeros((tm, LANES), F32)
    tg = jnp.zeros((tm, LANES), F32)
    for k in range(TOP_K):
        ti = jnp.where(lane == k, idxs[k], ti)
        tg = jnp.where(lane == k, es[k] * inv, tg)
    ti_ref[...] = ti.astype(I32)
    tg_ref[...] = tg


def _xo_ln2_router(x1, o, wo, ln_g, ln_b, wr, br, tm):
    M = x1.shape[0]
    row = lambda w: pl.BlockSpec((tm, w), lambda i: (i, 0))
    return pl.pallas_call(
        _xo_body,
        grid=(M // tm,),
        in_specs=[row(D_MODEL), row(D_MODEL), _resident(wo.shape), _resident(ln_g.shape), _resident(ln_b.shape),
                  _resident(wr.shape), _resident(br.shape)],
        out_specs=[row(D_MODEL), row(LANES), row(LANES)],
        out_shape=[jax.ShapeDtypeStruct((M, D_MODEL), F32), jax.ShapeDtypeStruct((M, LANES), I32),
                   jax.ShapeDtypeStruct((M, LANES), F32)],
        compiler_params=_cparams(1),
        name="xo_ln2_router",
    )(x1, o, wo, ln_g, ln_b, wr, br)


MOE_SUB_ROWS = 256
MOE_FF_CHUNK = 256


def _ffn_body(te_ref, tr_ref, nu_ref, idc_ref, idn_ref, x_hbm, wg_ref, bg_ref, wu_ref, bu_ref, wd_ref, bd_ref,
              o_ref, stg, sem, xbb, wgb, wub, wdb, *, tm, sb, nch):
    t = pl.program_id(0)
    c = pl.program_id(1)
    n_used = nu_ref[0]
    ps = tm // nch
    cur = lax.rem(t, 2)
    ring = lax.rem(c, 2)

    def row_copy(tok, sl, r):
        return pltpu.make_async_copy(x_hbm.at[pl.ds(tok, 1), :], stg.at[sl, pl.ds(r, 1), :], sem.at[sl])

    def group_wait(sl):
        pltpu.make_async_copy(x_hbm.at[pl.ds(0, ps), :], stg.at[sl], sem.at[sl]).wait()

    def ffn_rows(rs, wg, wu, wd):
        xb = xbb[cur, rs, :]
        g = jnp.minimum(_dot(xb, wg) + bg_ref[0], SWIGLU_LIMIT)
        u = jnp.clip(_dot(xb, wu) + bu_ref[0], -SWIGLU_LIMIT, SWIGLU_LIMIT)
        hid = (u + 1.0) * (g * _sigmoid(SWIGLU_ALPHA * g))
        o_ref[rs, :] += _dot(hid.astype(BF16), wd)

    @pl.when(t >= n_used)
    def _():
        @pl.when(c == 0)
        def _():
            o_ref[...] = jnp.zeros((tm, D_MODEL), F32)

    @pl.when(t < n_used)
    def _():
        @pl.when((t == 0) & (c == 0))
        def _():
            for j in range(nch):
                def issue(r, carry):
                    row_copy(idc_ref[0, 0, j * ps + r], 0, r).start()
                    return carry
                lax.fori_loop(0, ps, issue, 0, unroll=8)
                group_wait(0)
                xbb[0, j * ps:(j + 1) * ps, :] = stg[0].astype(BF16)

        for r in range(ps):
            row_copy(idn_ref[0, 0, c * ps + r], ring, r).start()

        @pl.when((t > 0) | (c > 0))
        def _():
            prev_c = lax.rem(c + nch - 1, nch)
            dst = jnp.where(c == 0, cur, 1 - cur)
            group_wait(1 - ring)
            xbb[dst, pl.ds(pl.multiple_of(prev_c * ps, ps), ps), :] = stg[1 - ring].astype(BF16)

        rows = tr_ref[t]

        @pl.when(rows > tm - sb)
        def _():
            @pl.when(c == 0)
            def _():
                o_ref[...] = jnp.broadcast_to(bd_ref[0], (tm, D_MODEL))
            ffn_rows(slice(0, tm), wg_ref[0, 0].astype(BF16), wu_ref[0, 0].astype(BF16), wd_ref[0, 0].astype(BF16))

        @pl.when(rows <= tm - sb)
        def _():
            wgb[...] = wg_ref[0, 0].astype(BF16)
            wub[...] = wu_ref[0, 0].astype(BF16)
            wdb[...] = wd_ref[0, 0].astype(BF16)
            for s in range(tm // sb):
                rs = slice(s * sb, (s + 1) * sb)

                @pl.when((s * sb >= rows) & (c == 0))
                def _():
                    o_ref[rs, :] = jnp.zeros((sb, D_MODEL), F32)

                @pl.when(s * sb < rows)
                def _():
                    @pl.when(c == 0)
                    def _():
                        o_ref[rs, :] = jnp.broadcast_to(bd_ref[0], (sb, D_MODEL))
                    ffn_rows(rs, wgb[...], wub[...], wdb[...])

        @pl.when((t == n_used - 1) & (c == nch - 1))
        def _():
            group_wait(ring)


def _moe_ffn(x, row_tok, tile_e, tile_rows, n_used, wg, bg, wu, bu, wd, bd, tm, sb):
    P = row_tok.shape[0]
    n_tiles = P // tm
    nch = D_FF // MOE_FF_CHUNK
    fc = MOE_FF_CHUNK
    assert tm % nch == 0 and (tm // nch) % 16 == 0 and nch % 2 == 0

    def tile(t, nu):
        return jnp.minimum(t, jnp.maximum(nu[0] - 1, 0))

    def chunk(t, c, nu):
        return jnp.where(t < nu[0], c, nch - 1)

    idx_blk = (1, 1, tm)
    return pl.pallas_call(
        functools.partial(_ffn_body, tm=tm, sb=sb, nch=nch),
        grid_spec=pltpu.PrefetchScalarGridSpec(
            num_scalar_prefetch=3,
            grid=(n_tiles, nch),
            in_specs=[
                pl.BlockSpec(idx_blk, lambda t, c, te, tr, nu: (tile(t, nu), 0, 0), memory_space=pltpu.SMEM),
                pl.BlockSpec(idx_blk, lambda t, c, te, tr, nu: (tile(t + 1, nu), 0, 0), memory_space=pltpu.SMEM),
                pl.BlockSpec(memory_space=pl.ANY),
                pl.BlockSpec((1, 1, D_MODEL, fc), lambda t, c, te, tr, nu: (0, te[tile(t, nu)], 0, chunk(t, c, nu))),
                pl.BlockSpec((1, 1, fc), lambda t, c, te, tr, nu: (te[tile(t, nu)], 0, chunk(t, c, nu))),
                pl.BlockSpec((1, 1, D_MODEL, fc), lambda t, c, te, tr, nu: (0, te[tile(t, nu)], 0, chunk(t, c, nu))),
                pl.BlockSpec((1, 1, fc), lambda t, c, te, tr, nu: (te[tile(t, nu)], 0, chunk(t, c, nu))),
                pl.BlockSpec((1, 1, fc, D_MODEL), lambda t, c, te, tr, nu: (0, te[tile(t, nu)], chunk(t, c, nu), 0)),
                pl.BlockSpec((1, 1, D_MODEL), lambda t, c, te, tr, nu: (te[tile(t, nu)], 0, 0)),
            ],
            out_specs=pl.BlockSpec((tm, D_MODEL), lambda t, c, te, tr, nu: (t, 0)),
            scratch_shapes=[pltpu.VMEM((2, tm // nch, D_MODEL), F32), pltpu.SemaphoreType.DMA((2,)),
                            pltpu.VMEM((2, tm, D_MODEL), BF16),
                            pltpu.VMEM((D_MODEL, fc), BF16), pltpu.VMEM((D_MODEL, fc), BF16),
                            pltpu.VMEM((fc, D_MODEL), BF16)]),
        out_shape=jax.ShapeDtypeStruct((P, D_MODEL), F32),
        compiler_params=_cparams(2),
        name="moe_ffn",
    )(tile_e, tile_rows, n_used, row_tok.reshape(n_tiles, 1, tm), row_tok.reshape(n_tiles, 1, tm), x,
      wg, bg, wu, bu, wd, bd)


def _combine_body(posc_ref, posn_ref, x2_ref, tg_ref, ys_hbm, g_ref, b_ref, oa_ref, ob_ref, buf, sem, *,
                  tc, nblk, nblk_a):
    i = pl.program_id(0)
    slot = lax.rem(i, 2)

    def row_copy(src_row, sl, k, j):
        return pltpu.make_async_copy(ys_hbm.at[pl.ds(src_row, 1), :], buf.at[sl, k, pl.ds(j, 1), :], sem.at[sl])

    def slot_wait(sl):
        pltpu.make_async_copy(buf.at[sl], buf.at[sl], sem.at[sl]).wait()

    @pl.when(i == 0)
    def _():
        def issue(j, carry):
            for k in range(TOP_K):
                row_copy(posc_ref[0, 0, j * TOP_K + k], 0, k, j).start()
            return carry
        lax.fori_loop(0, tc, issue, 0, unroll=4)

    slot_wait(slot)
    for j in range(tc):
        for k in range(TOP_K):
            row_copy(posn_ref[0, 0, j * TOP_K + k], 1 - slot, k, j).start()
    y = None
    for k in range(TOP_K):
        term = buf[slot, k] * tg_ref[:, k:k + 1]
        y = term if y is None else y + term
    out = _layer_norm(DN_ALPHA * x2_ref[...] + y, g_ref[...], b_ref[...])

    @pl.when(i < nblk_a)
    def _():
        oa_ref[...] = out

    @pl.when(i >= nblk_a)
    def _():
        ob_ref[...] = out

    @pl.when(i == nblk - 1)
    def _():
        slot_wait(1 - slot)


def _combine_ln3(x2, tg, ys, pos, ln_g, ln_b, tc, split):
    M = pos.shape[0] // TOP_K
    nblk = M // tc
    nblk_a = split // tc
    assert split % tc == 0 and 0 < nblk_a < nblk
    pos3 = pos.reshape(nblk, 1, tc * TOP_K)
    return pl.pallas_call(
        functools.partial(_combine_body, tc=tc, nblk=nblk, nblk_a=nblk_a),
        grid=(nblk,),
        in_specs=[pl.BlockSpec((1, 1, tc * TOP_K), lambda i: (i, 0, 0), memory_space=pltpu.SMEM),
                  pl.BlockSpec((1, 1, tc * TOP_K), lambda i: (jnp.minimum(i + 1, nblk - 1), 0, 0),
                               memory_space=pltpu.SMEM),
                  pl.BlockSpec((tc, D_MODEL), lambda i: (i, 0)),
                  pl.BlockSpec((tc, LANES), lambda i: (i, 0)),
                  pl.BlockSpec(memory_space=pl.ANY),
                  _resident(ln_g.shape), _resident(ln_b.shape)],
        out_specs=[pl.BlockSpec((tc, D_MODEL), lambda i: (jnp.minimum(i, nblk_a - 1), 0)),
                   pl.BlockSpec((tc, D_MODEL), lambda i: (jnp.maximum(i - nblk_a, 0), 0))],
        out_shape=[jax.ShapeDtypeStruct((split, D_MODEL), F32), jax.ShapeDtypeStruct((M - split, D_MODEL), F32)],
        scratch_shapes=[pltpu.VMEM((2, TOP_K, tc, D_MODEL), F32), pltpu.SemaphoreType.DMA((2,))],
        compiler_params=_cparams(1),
        name="moe_combine",
    )(pos3, pos3, x2, tg, ys, ln_g, ln_b)


def _moe(x2, topi, gates, wg, bg, wu, bu, wd, bd, ln_g, ln_b, tm, tc, split):
    ntok = x2.shape[0]
    nk = ntok * TOP_K
    sb = min(tm, MOE_SUB_ROWS)
    n_tiles = -(-(nk + N_EXPERTS * (tm - 1)) // tm)
    e_flat = topi[:, :TOP_K].reshape(nk)
    onehot = (e_flat[:, None] == jnp.arange(N_EXPERTS, dtype=I32)[None, :]).astype(I32)
    csum = jnp.cumsum(onehot, axis=0)
    counts = csum[-1]
    rank = jnp.take_along_axis(csum, e_flat[:, None], axis=1)[:, 0] - 1
    tiles_e = (counts + tm - 1) // tm
    tile_end = jnp.cumsum(tiles_e)
    tile_start = tile_end - tiles_e
    dest = tile_start[e_flat] * tm + rank
    n_used = tile_end[-1:].astype(I32)
    t_ids = jnp.arange(n_tiles, dtype=I32)
    tile_e = jnp.minimum(jnp.searchsorted(tile_end, t_ids, side="right"), N_EXPERTS - 1).astype(I32)
    tile_rows = jnp.clip(counts[tile_e] - (t_ids - tile_start[tile_e]) * tm, 0, tm)
    tile_rows = jnp.where(t_ids < n_used[0], tile_rows, 0).astype(I32)
    P = n_tiles * tm
    row_tok = jnp.zeros((P,), I32).at[dest].set(jnp.arange(nk, dtype=I32) // TOP_K, unique_indices=True)
    ys = _moe_ffn(x2, row_tok, tile_e, tile_rows, n_used, wg, bg, wu, bu, wd, bd, tm, sb)
    return _combine_ln3(x2, gates, ys, dest, ln_g, ln_b, tc, split)


def kernel(x_prompt, x_sample, mem_prompt, state_mlstm_C, state_mlstm_n, state_mlstm_m, cache_swa_k, cache_swa_v, cache_mem_k, cache_mem_v, w_in, b_igate, b_fgate, g_mnorm, w_mix_out, ln1_g, ln1_b, w_xq, w_xk, w_xv, w_xo, ln2_g, ln2_b, w_router, b_router, w_gate, b_gate, w_up, b_up, w_down, b_down, ln3_g, ln3_b):
    assert DEPTH == 1
    B, T, _ = x_prompt.shape
    DB, S, _ = x_sample.shape
    assert B == 1
    n_mem = mem_prompt.shape[1]
    wb = cache_swa_k.shape[2]
    row2 = lambda a: a[0].reshape(1, -1)

    wi = w_in[0]
    c0 = 2 * MIX_QK + 2 * MIX_V
    w_m = wi[:, :c0].astype(BF16)
    w_g = jnp.pad(wi[:, c0:c0 + 2 * M_HEADS], ((0, 0), (0, LANES - 2 * M_HEADS))).astype(BF16)
    w_a = wi[:, c0 + 2 * M_HEADS:].astype(BF16)
    gate_bias = jnp.pad(jnp.concatenate([b_igate[0], b_fgate[0]]), (0, LANES - 2 * M_HEADS)).reshape(1, LANES)
    w = dict(
        wmo=w_mix_out[0].astype(BF16), ln1_g=row2(ln1_g), ln1_b=row2(ln1_b), wxq=w_xq[0].astype(BF16),
        wxo=w_xo[0].astype(BF16), ln2_g=row2(ln2_g), ln2_b=row2(ln2_b),
        wr=jnp.pad(w_router[0], ((0, 0), (0, LANES - N_EXPERTS))).astype(BF16),
        br=jnp.pad(b_router[0], (0, LANES - N_EXPERTS)).reshape(1, LANES),
        wg=w_gate, bg=b_gate[0].reshape(N_EXPERTS, 1, D_FF), wu=w_up, bu=b_up[0].reshape(N_EXPERTS, 1, D_FF),
        wd=w_down, bd=b_down[0].reshape(N_EXPERTS, 1, D_MODEL), ln3_g=row2(ln3_g), ln3_b=row2(ln3_b))
    gm = g_mnorm[0].reshape(1, MIX_V)

    xp = x_prompt.reshape(T, D_MODEL)
    zm = _mm(xp, w_m, BF16, 512, 1024, "proj_mlstm")
    zg = _mm(xp, w_g, F32, 512, LANES, "proj_gates")
    dils = tuple(dil for _, dil in PATTERNS)
    za_views = _proj_attn(xp, w_a, dils)
    wbp = min(wb, T)
    kv_tail = _mm(xp[T - wbp:], w_a[:, MIX_A:], F32, 512, 1024, "proj_kv_tail")
    zeros_c = jnp.zeros((1, M_HEADS, M_DQK, M_DV), F32)
    zeros_n = jnp.zeros((1, M_HEADS, M_DQK), F32)
    zeros_m = jnp.zeros((1, M_HEADS, LANES), F32)
    hm, pC, pn, pm = _mlstm(zm, zg, gate_bias, gm, zeros_c, zeros_n, zeros_m, 1, T // 128)
    pats = [_dilated_pattern(zv, window, dil) for zv, (window, dil) in zip(za_views, PATTERNS)]
    mp = mem_prompt.reshape(n_mem, D_MODEL)
    mem_k = _mm(mp, w_xk[0].astype(BF16), F32, n_mem, 1024, "mem_k")
    mem_v = _mm(mp, w_xv[0].astype(BF16), F32, n_mem, 1024, "mem_v")
    mem_k5 = mem_k.reshape(1, B, n_mem, X_HEADS, X_DH)
    mem_v5 = mem_v.reshape(1, B, n_mem, X_HEADS, X_DH)
    x1_p, q_p = _mix_ln1_q(xp, hm, [p[0] for p in pats], [p[1] for p in pats], dils, w["wmo"], w["ln1_g"], w["ln1_b"],
                           w["wxq"], 256)
    o_p = _xattn(q_p.reshape(1, T, D_MODEL), mem_k.astype(BF16).reshape(1, n_mem, D_MODEL),
                 mem_v.astype(BF16).reshape(1, n_mem, D_MODEL), 512).reshape(T, D_MODEL)
    x2_p, topi_p, gates_p = _xo_ln2_router(x1_p, o_p, w["wxo"], w["ln2_g"], w["ln2_b"], w["wr"], w["br"], 256)

    ns = DB * S
    xs_ = x_sample.reshape(ns, D_MODEL)
    zm_s = _mm(xs_, w_m, BF16, ns, 1024, "proj_mlstm_s")
    zg_s = _mm(xs_, w_g, F32, ns, LANES, "proj_gates_s")
    za_s = _mm(xs_, w_a, F32, ns, 1024, "proj_attn_s")
    pad_rows = 128 - S
    zm_pad = jnp.pad(zm_s.reshape(DB, S, -1), ((0, 0), (0, pad_rows), (0, 0))).reshape(DB * 128, -1)
    lane = jnp.arange(LANES)
    neutral = jnp.where(lane < M_HEADS, NEG_BIG, jnp.where(lane < 2 * M_HEADS, -NEG_BIG, 0.0)).astype(F32)
    zg_pad = jnp.concatenate([zg_s.reshape(DB, S, LANES), jnp.broadcast_to(neutral, (DB, pad_rows, LANES))],
                             axis=1).reshape(DB * 128, LANES)
    m0 = jnp.broadcast_to(state_mlstm_m[0][:, :, None], (DB, M_HEADS, LANES))
    hm_s, sC, sn, sm = _mlstm(zm_pad, zg_pad, gate_bias, gm, state_mlstm_C[0], state_mlstm_n[0], m0, DB, 1)
    hm_s = hm_s.reshape(DB, 128, MIX_V)[:, :S].reshape(ns, MIX_V)
    za_pad = jnp.pad(za_s.reshape(DB, S, -1), ((0, 0), (0, SWA_ROWS - S), (0, 0))).astype(BF16)
    ha_s = _swa_sample(za_pad, cache_swa_k.reshape(DB, wb * A_HEADS, A_DH),
                       cache_swa_v.reshape(DB, wb * A_HEADS, A_DH))[:, :S].reshape(ns, MIX_A)
    x1_s, q_s = _mix_ln1_q(xs_, hm_s, [ha_s], [], (1,), w["wmo"], w["ln1_g"], w["ln1_b"], w["wxq"], ns)
    q_pad = jnp.pad(q_s.reshape(DB, S, D_MODEL), ((0, 0), (0, SWA_ROWS - S), (0, 0)))
    o_s = _xattn_cache(q_pad, cache_mem_k, cache_mem_v)[:, :S].reshape(ns, D_MODEL)
    x2_s, topi_s, gates_s = _xo_ln2_router(x1_s, o_s, w["wxo"], w["ln2_g"], w["ln2_b"], w["wr"], w["br"], ns)

    yp, ys_ = _moe(jnp.concatenate([x2_p, x2_s]), jnp.concatenate([topi_p, topi_s]),
                   jnp.concatenate([gates_p, gates_s]), w["wg"], w["bg"], w["wu"], w["bu"], w["wd"], w["bd"],
                   w["ln3_g"], w["ln3_b"], 1024, 128, T)

    return (yp.reshape(B, T, D_MODEL), ys_.reshape(DB, S, D_MODEL),
            pC[None], pn[None], pm[:, :, 0][None],
            kv_tail[:, :MIX_A].reshape(1, B, wbp, A_HEADS, A_DH), kv_tail[:, MIX_A:].reshape(1, B, wbp, A_HEADS, A_DH),
            mem_k5, mem_v5,
            sC[None], sn[None], sm[:, :, 0][None],
            za_s[:, MIX_A:2 * MIX_A].reshape(1, DB, S, A_HEADS, A_DH),
            za_s[:, 2 * MIX_A:].reshape(1, DB, S, A_HEADS, A_DH))
```

```python
import functools

import jax
import jax.numpy as jnp
from jax import lax
from jax.experimental import pallas as pl
from jax.experimental.pallas import tpu as pltpu

F32, BF16, I32 = jnp.float32, jnp.bfloat16, jnp.int32

D_MODEL = 2048
DEPTH = 1
M_HEADS, M_DQK, M_DV = 4, 128, 256
HEAD_EPS = 1e-6
A_HEADS, A_DH = 8, 128
PATTERNS = ((128, 1), (512, 4), (2048, 16))
BAND_BLOCK = 128
X_HEADS = 4
X_DH = D_MODEL // X_HEADS
N_EXPERTS, TOP_K, D_FF = 32, 4, 2048
SWIGLU_LIMIT, SWIGLU_ALPHA = 7.0, 1.702
DN_ALPHA = (2 * DEPTH) ** 0.25
LN_EPS = 1e-5
MIX_V = M_HEADS * M_DV
MIX_A = A_HEADS * A_DH
MIX_QK = M_HEADS * M_DQK

LANES = 128
VMEM_LIMIT_BYTES = 58 * 1024 * 1024

NEG_BIG = -1e30


def _cparams(n_axes, vmem=VMEM_LIMIT_BYTES):
    return pltpu.CompilerParams(dimension_semantics=("arbitrary",) * n_axes, vmem_limit_bytes=vmem)


def _dot(a, b):
    return jnp.dot(a, b, preferred_element_type=F32)


def _dot_nt(a, b):
    return lax.dot_general(a, b, (((1,), (1,)), ((), ())), preferred_element_type=F32)


def _dot_tn(a, b):
    return lax.dot_general(a, b, (((0,), (0,)), ((), ())), preferred_element_type=F32)


def _log_sigmoid(x):
    return jnp.minimum(x, 0.0) - jnp.log(1.0 + jnp.exp(-jnp.abs(x)))


def _sigmoid(x):
    return 1.0 / (1.0 + jnp.exp(-x))


def _layer_norm(v, g, b):
    mu = jnp.mean(v, axis=-1, keepdims=True)
    d = v - mu
    var = jnp.mean(d * d, axis=-1, keepdims=True)
    return d * lax.rsqrt(var + LN_EPS) * g + b


def _resident(shape):
    nd = len(shape)
    return pl.BlockSpec(shape, lambda *_: (0,) * nd, pipeline_mode=pl.Buffered(1))


def _mm_body(x_ref, w_ref, o_ref):
    o_ref[...] = _dot(x_ref[...].astype(BF16), w_ref[...]).astype(o_ref.dtype)


def _mm(x, w, out_dtype, tm, tn, name):
    M, K = x.shape
    N = w.shape[1]
    assert M % tm == 0 and N % tn == 0
    return pl.pallas_call(
        _mm_body,
        grid=(M // tm, N // tn),
        in_specs=[pl.BlockSpec((tm, K), lambda i, j: (i, 0)), pl.BlockSpec((K, tn), lambda i, j: (0, j))],
        out_specs=pl.BlockSpec((tm, tn), lambda i, j: (i, j)),
        out_shape=jax.ShapeDtypeStruct((M, N), out_dtype),
        compiler_params=_cparams(2),
        name=name,
    )(x, w)


def _mlstm_body(q_ref, k_ref, v_ref, zo_ref, zg_ref, gb_ref, gm_ref, c0_ref, n0_ref, m0_ref,
                hm_ref, c_ref, n_ref, m_ref, *, L, sub):
    @pl.when(pl.program_id(1) == 0)
    def _():
        c_ref[...] = c0_ref[...]
        n_ref[...] = n0_ref[...]
        m_ref[...] = m0_ref[...]

    for ci in range(sub):
        _mlstm_chunk(slice(ci * L, (ci + 1) * L), q_ref, k_ref, v_ref, zo_ref, zg_ref, gb_ref, gm_ref,
                     hm_ref, c_ref, n_ref, m_ref, L)


def _mlstm_chunk(rows, q_ref, k_ref, v_ref, zo_ref, zg_ref, gb_ref, gm_ref, hm_ref, c_ref, n_ref, m_ref, L):
    scale = M_DQK ** -0.5
    g = zg_ref[rows, :] + gb_ref[...]
    gt = g.T
    row = lax.broadcasted_iota(I32, (L, L), 0)
    col = lax.broadcasted_iota(I32, (L, L), 1)
    tri = row >= col
    for h in range(M_HEADS):
        i_col = g[:, h:h + 1]
        f_col = _log_sigmoid(g[:, M_HEADS + h:M_HEADS + h + 1])
        i_row = gt[h:h + 1, :]
        f_row = _log_sigmoid(gt[M_HEADS + h:M_HEADS + h + 1, :])
        q = q_ref[rows, h * M_DQK:(h + 1) * M_DQK]
        k = k_ref[rows, h * M_DQK:(h + 1) * M_DQK]
        v = v_ref[rows, h * M_DV:(h + 1) * M_DV]
        zo = zo_ref[rows, h * M_DV:(h + 1) * M_DV].astype(F32)
        C = c_ref[0, h]
        n = n_ref[0, h:h + 1, :]
        m = m_ref[0, h:h + 1, 0:1]
        b_col = jnp.sum(jnp.where(tri, f_row, 0.0), axis=1, keepdims=True)
        b_row = jnp.sum(jnp.where(row <= col, f_col, 0.0), axis=0, keepdims=True)
        logd = jnp.where(tri, b_col - b_row + i_row, -jnp.inf)
        inter = b_col + m
        mt = jnp.maximum(inter, jnp.max(logd, axis=1, keepdims=True))
        sd = _dot_nt(q, k) * scale * jnp.exp(logd - mt)
        sc = jnp.exp(inter - mt)
        num = _dot(sd.astype(BF16), v) + sc * _dot(q, C.astype(BF16))
        den = jnp.sum(sd, axis=1, keepdims=True) + sc * jnp.sum(q.astype(F32) * n, axis=1, keepdims=True)
        hh = num / jnp.maximum(jnp.abs(den), jnp.exp(-mt))
        hn = hh * lax.rsqrt(jnp.mean(hh * hh, axis=1, keepdims=True) + HEAD_EPS)
        out = hn * gm_ref[:, h * M_DV:(h + 1) * M_DV] * _sigmoid(zo)
        hm_ref[rows, h * M_DV:(h + 1) * M_DV] = out.astype(hm_ref.dtype)
        bl = b_col[L - 1:L, :]
        ml = mt[L - 1:L, :]
        w_col = jnp.exp(bl - b_col + i_col - ml)
        scl = jnp.exp(bl + m - ml)
        kw = k.astype(F32) * (w_col * scale)
        c_ref[0, h] = scl * C + _dot_tn(kw.astype(BF16), v)
        n_ref[0, h:h + 1, :] = scl * n + jnp.sum(kw, axis=0, keepdims=True)
        m_ref[0, h:h + 1, :] = jnp.broadcast_to(ml, (1, LANES))


def _mlstm(zm, zg, gate_bias, g_mnorm, c0, n0, m0, B, nc, sub=1, L=128):
    T = B * nc * L
    assert zm.shape == (T, 2 * MIX_QK + 2 * MIX_V) and nc % sub == 0
    nc, L_chunk, L = nc // sub, L, L * sub
    row_blk = lambda b, c: b * nc + c
    state_specs = [pl.BlockSpec((1, M_HEADS, M_DQK, M_DV), lambda b, c: (b, 0, 0, 0)),
                   pl.BlockSpec((1, M_HEADS, M_DQK), lambda b, c: (b, 0, 0)),
                   pl.BlockSpec((1, M_HEADS, LANES), lambda b, c: (b, 0, 0))]
    return pl.pallas_call(
        functools.partial(_mlstm_body, L=L_chunk, sub=sub),
        grid=(B, nc),
        in_specs=[pl.BlockSpec((L, MIX_QK), lambda b, c: (row_blk(b, c), 0)),
                  pl.BlockSpec((L, MIX_QK), lambda b, c: (row_blk(b, c), 1)),
                  pl.BlockSpec((L, MIX_V), lambda b, c: (row_blk(b, c), 1)),
                  pl.BlockSpec((L, MIX_V), lambda b, c: (row_blk(b, c), 2)),
                  pl.BlockSpec((L, LANES), lambda b, c: (row_blk(b, c), 0)),
                  pl.BlockSpec((1, LANES), lambda b, c: (0, 0)),
                  pl.BlockSpec((1, MIX_V), lambda b, c: (0, 0))] + state_specs,
        out_specs=[pl.BlockSpec((L, MIX_V), lambda b, c: (row_blk(b, c), 0))] + state_specs,
        out_shape=[jax.ShapeDtypeStruct((T, MIX_V), BF16),
                   jax.ShapeDtypeStruct((B, M_HEADS, M_DQK, M_DV), F32),
                   jax.ShapeDtypeStruct((B, M_HEADS, M_DQK), F32),
                   jax.ShapeDtypeStruct((B, M_HEADS, LANES), F32)],
        compiler_params=_cparams(2),
        name="mlstm",
    )(zm, zm, zm, zm, zg, gate_bias, g_mnorm, c0, n0, m0)


def _dil_body(q_ref, kc_ref, kp_ref, vc_ref, vp_ref, o_ref, l_ref, *, span):
    scale = A_DH ** -0.5
    n = pl.program_id(1)
    row = lax.broadcasted_iota(I32, (BAND_BLOCK, BAND_BLOCK), 0)
    col = lax.broadcasted_iota(I32, (BAND_BLOCK, BAND_BLOCK), 1)
    first = jnp.where(n > 0, 0, 2 * BAND_BLOCK)
    mask_p = (BAND_BLOCK + row - col + first) <= span
    mask_c = row >= col
    lane = lax.broadcasted_iota(I32, (BAND_BLOCK, LANES), 1)
    lse_tile = jnp.zeros((BAND_BLOCK, LANES), F32)
    ones = jnp.ones((BAND_BLOCK, A_DH), BF16)
    for h in range(A_HEADS):
        sl = slice(h * A_DH, (h + 1) * A_DH)
        q = q_ref[:, sl]
        sp = jnp.where(mask_p, _dot_nt(q, kp_ref[:, sl]) * scale, -jnp.inf)
        sc = jnp.where(mask_c, _dot_nt(q, kc_ref[:, sl]) * scale, -jnp.inf)
        m = jnp.max(jnp.maximum(sp, sc), axis=1, keepdims=True)
        pp = jnp.exp(sp - m).astype(BF16)
        pc = jnp.exp(sc - m).astype(BF16)
        oa = (_dot(pp, jnp.concatenate([vp_ref[:, sl], ones], axis=1))
              + _dot(pc, jnp.concatenate([vc_ref[:, sl], ones], axis=1)))
        l = oa[:, A_DH:A_DH + 1]
        o_ref[:, sl] = (oa[:, :A_DH] / l).astype(o_ref.dtype)
        lse_tile = jnp.where(lane == h, m + jnp.log(l), lse_tile)
    l_ref[...] = lse_tile


PROJ_ATTN_ROWS = 512


def _proj_attn_body(x_ref, w_ref, *refs, dils):
    o_refs, acc = refs[:len(dils)], refs[len(dils)]
    tm = x_ref.shape[0]
    nslab = acc.shape[0]
    tn = nslab * LANES
    xb = x_ref[...].astype(BF16)
    for j in range(w_ref.shape[1] // tn):
        z = _dot(xb, w_ref[:, j * tn:(j + 1) * tn])
        for dil, o_ref in zip(dils, o_refs):
            if dil == 1:
                o_ref[:, j * tn:(j + 1) * tn] = z.astype(o_ref.dtype)
        for s in range(nslab):
            acc[s] = z[:, s * LANES:(s + 1) * LANES]
        for dil, o_ref in zip(dils, o_refs):
            if dil == 1:
                continue
            rows = tm // dil
            for r in range(dil):
                for s in range(nslab):
                    c0 = r * w_ref.shape[1] + j * tn + s * LANES
                    o_ref[:, c0:c0 + LANES] = acc[s, pl.ds(r, rows, stride=dil), :].astype(o_ref.dtype)


def _proj_attn(x, w, dils):
    T, K = x.shape
    N = w.shape[1]
    tm, tn = PROJ_ATTN_ROWS, 1024
    assert T % tm == 0 and N % tn == 0 and all(tm % (16 * d) == 0 for d in dils)
    return pl.pallas_call(
        functools.partial(_proj_attn_body, dils=dils),
        grid=(T // tm,),
        in_specs=[pl.BlockSpec((tm, K), lambda i: (i, 0)), _resident(w.shape)],
        out_specs=[pl.BlockSpec((tm // d, d * N), lambda i: (i, 0)) for d in dils],
        out_shape=[jax.ShapeDtypeStruct((T // d, d * N), BF16) for d in dils],
        scratch_shapes=[pltpu.VMEM((tn // LANES, tm, LANES), F32)],
        compiler_params=_cparams(1),
        name="proj_attn",
    )(x, w)


def _dilated_pattern(zv, window, dil):
    N = zv.shape[0]
    span = window // dil
    assert zv.shape[1] == dil * 3 * MIX_A and N % BAND_BLOCK == 0 and BAND_BLOCK - 1 <= span
    nb = N // BAND_BLOCK
    prev = lambda n: jnp.maximum(n - 1, 0)
    blk = (BAND_BLOCK, MIX_A)
    o, lse = pl.pallas_call(
        functools.partial(_dil_body, span=span),
        grid=(dil, nb),
        in_specs=[pl.BlockSpec(blk, lambda r, n: (n, 3 * r)),
                  pl.BlockSpec(blk, lambda r, n: (n, 3 * r + 1)),
                  pl.BlockSpec(blk, lambda r, n: (prev(n), 3 * r + 1)),
                  pl.BlockSpec(blk, lambda r, n: (n, 3 * r + 2)),
                  pl.BlockSpec(blk, lambda r, n: (prev(n), 3 * r + 2))],
        out_specs=[pl.BlockSpec(blk, lambda r, n: (n, r)),
                   pl.BlockSpec((BAND_BLOCK, LANES), lambda r, n: (n, r))],
        out_shape=[jax.ShapeDtypeStruct((N, dil * MIX_A), BF16),
                   jax.ShapeDtypeStruct((N, dil * LANES), F32)],
        compiler_params=_cparams(2),
        name=f"dilated_d{dil}",
    )(zv, zv, zv, zv, zv)
    return o, lse


SWA_ROWS = 16


def _swa_body(q_ref, kn_ref, vn_ref, kb_ref, vb_ref, o_ref, *, wb):
    scale = A_DH ** -0.5
    s_k = lax.broadcasted_iota(I32, (SWA_ROWS, wb), 0)
    p_k = lax.broadcasted_iota(I32, (SWA_ROWS, wb), 1)
    d_k = wb + s_k - p_k
    s_n = lax.broadcasted_iota(I32, (SWA_ROWS, SWA_ROWS), 0)
    p_n = lax.broadcasted_iota(I32, (SWA_ROWS, SWA_ROWS), 1)
    d_n = s_n - p_n
    masks = []
    for window, dil in PATTERNS:
        assert dil & (dil - 1) == 0 and window % dil == 0
        mk = jnp.where(jnp.bitwise_and(d_k, dil - 1) == 0, d_k, window + 1) <= window
        mn = jnp.where(jnp.bitwise_and(d_n, dil - 1) == 0, jnp.where(d_n >= 0, d_n, window + 1), window + 1) <= window
        masks.append((mk, mn))
    for h in range(A_HEADS):
        sl = slice(h * A_DH, (h + 1) * A_DH)
        q = q_ref[0, :, sl]
        kb = kb_ref[0, pl.ds(h, wb, stride=A_HEADS), :].astype(BF16)
        vb = vb_ref[0, pl.ds(h, wb, stride=A_HEADS), :].astype(BF16)
        kn = kn_ref[0, :, sl]
        vn = vn_ref[0, :, sl]
        s_cache = _dot_nt(q, kb) * scale
        s_new = _dot_nt(q, kn) * scale
        ps, lses = [], []
        for mk, mn in masks:
            sk = jnp.where(mk, s_cache, -jnp.inf)
            sn = jnp.where(mn, s_new, -jnp.inf)
            m = jnp.maximum(jnp.max(sk, axis=1, keepdims=True), jnp.max(sn, axis=1, keepdims=True))
            pk = jnp.exp(sk - m)
            pn = jnp.exp(sn - m)
            l = jnp.sum(pk, axis=1, keepdims=True) + jnp.sum(pn, axis=1, keepdims=True)
            ps.append((pk, pn, l))
            lses.append(m + jnp.log(l))
        top = functools.reduce(jnp.maximum, lses)
        es = [jnp.exp(x - top) for x in lses]
        tot = functools.reduce(lambda a, b: a + b, es)
        pk_all = None
        pn_all = None
        for (pk, pn, l), e in zip(ps, es):
            coef = e / (tot * l)
            pk_all = pk * coef if pk_all is None else pk_all + pk * coef
            pn_all = pn * coef if pn_all is None else pn_all + pn * coef
        o = _dot(pk_all.astype(BF16), vb) + _dot(pn_all.astype(BF16), vn)
        o_ref[0, :, sl] = o.astype(o_ref.dtype)


def _swa_sample(za_pad, kbuf, vbuf):
    B, wb = kbuf.shape[0], kbuf.shape[1] // A_HEADS
    new_blk = (1, SWA_ROWS, MIX_A)
    cache_blk = (1, wb * A_HEADS, A_DH)
    return pl.pallas_call(
        functools.partial(_swa_body, wb=wb),
        grid=(B,),
        in_specs=[pl.BlockSpec(new_blk, lambda b: (b, 0, 0)),
                  pl.BlockSpec(new_blk, lambda b: (b, 0, 1)),
                  pl.BlockSpec(new_blk, lambda b: (b, 0, 2)),
                  pl.BlockSpec(cache_blk, lambda b: (b, 0, 0)),
                  pl.BlockSpec(cache_blk, lambda b: (b, 0, 0))],
        out_specs=pl.BlockSpec(new_blk, lambda b: (b, 0, 0)),
        out_shape=jax.ShapeDtypeStruct((B, SWA_ROWS, MIX_A), BF16),
        compiler_params=_cparams(1),
        name="swa_sample",
    )(za_pad, za_pad, za_pad, kbuf, vbuf)


def _mix_body(*refs, dils):
    n_pat = len(dils)
    x_ref, hm_ref = refs[0], refs[1]
    n_lse = n_pat if n_pat > 1 else 0
    o_refs = refs[2:2 + n_pat]
    l_refs = refs[2 + n_pat:2 + n_pat + n_lse]
    wmo_ref, g_ref, b_ref, wq_ref, x1_ref, q_ref = refs[2 + n_pat + n_lse:8 + n_pat + n_lse]
    if n_pat == 1:
        ha = o_refs[0][...]
    else:
        o_nat, l_nat = refs[8 + n_pat + n_lse:]
        tm = x_ref.shape[0]
        for p, dil in enumerate(dils):
            if dil == 1:
                continue
            rows = tm // dil
            for r in range(dil):
                dst = pl.ds(r, rows, stride=dil)
                l_nat[p, dst, :] = l_refs[p][:, r * LANES:(r + 1) * LANES]
                for h in range(A_HEADS):
                    c0 = r * MIX_A + h * A_DH
                    o_nat[p, h, dst, :] = o_refs[p][:, c0:c0 + A_DH].astype(F32)

        def lse_of(p, h):
            return l_refs[p][:, h:h + 1] if dils[p] == 1 else l_nat[p, :, h:h + 1]

        def out_of(p, h):
            return o_refs[p][:, h * A_DH:(h + 1) * A_DH].astype(F32) if dils[p] == 1 else o_nat[p, h]

        cols = []
        for h in range(A_HEADS):
            ls = [lse_of(p, h) for p in range(n_pat)]
            top = functools.reduce(jnp.maximum, ls)
            es = [jnp.exp(x - top) for x in ls]
            inv = 1.0 / functools.reduce(lambda a, b: a + b, es)
            acc = None
            for p, e in enumerate(es):
                term = out_of(p, h) * (e * inv)
                acc = term if acc is None else acc + term
            cols.append(acc.astype(BF16))
        ha = jnp.concatenate(cols, axis=1)
    mix = _dot(hm_ref[...], wmo_ref[0:MIX_V, :]) + _dot(ha, wmo_ref[MIX_V:MIX_V + MIX_A, :])
    x1 = _layer_norm(DN_ALPHA * x_ref[...] + mix, g_ref[...], b_ref[...])
    x1_ref[...] = x1
    q_ref[...] = _dot(x1.astype(BF16), wq_ref[...]).astype(q_ref.dtype)


def _mix_ln1_q(x, hm, outs, lses, dils, wmo, ln_g, ln_b, wq, tm):
    M = x.shape[0]
    n_pat = len(outs)
    assert len(lses) == (n_pat if n_pat > 1 else 0) and len(dils) == n_pat
    assert all(tm % (16 * d) == 0 for d in dils)
    row = lambda w: pl.BlockSpec((tm, w), lambda i: (i, 0))
    view = lambda w, d: pl.BlockSpec((tm // d, d * w), lambda i: (i, 0))
    scratch = []
    if n_pat > 1:
        scratch = [pltpu.VMEM((n_pat, A_HEADS, tm, A_DH), F32), pltpu.VMEM((n_pat, tm, LANES), F32)]
    return pl.pallas_call(
        functools.partial(_mix_body, dils=tuple(dils)),
        grid=(M // tm,),
        in_specs=[row(D_MODEL), row(MIX_V)] + [view(MIX_A, d) for d in dils]
        + [view(LANES, d) for d in dils[:len(lses)]]
        + [_resident(wmo.shape), _resident(ln_g.shape), _resident(ln_b.shape), _resident(wq.shape)],
        out_specs=[row(D_MODEL), row(D_MODEL)],
        out_shape=[jax.ShapeDtypeStruct((M, D_MODEL), F32), jax.ShapeDtypeStruct((M, D_MODEL), BF16)],
        scratch_shapes=scratch,
        compiler_params=_cparams(1),
        name="mix_ln1_q",
    )(x, hm, *outs, *lses, wmo, ln_g, ln_b, wq)


def _xattn_body(q_ref, k_ref, v_ref, o_ref):
    for h in range(X_HEADS):
        sl = slice(h * X_DH, (h + 1) * X_DH)
        o_ref[0, :, sl] = _xattn_head(q_ref[0, :, sl], k_ref[0, :, sl], v_ref[0, :, sl]).astype(o_ref.dtype)


def _xattn_head(q, k, v):
    s = _dot_nt(q, k.astype(BF16)) * (X_DH ** -0.5)
    p = jnp.exp(s - jnp.max(s, axis=1, keepdims=True))
    p = p / jnp.sum(p, axis=1, keepdims=True)
    return _dot(p.astype(BF16), v.astype(BF16))


def _xattn_cache_body(q_ref, k_hbm, v_hbm, o_ref, kbuf, vbuf, sem, *, nb):
    b = pl.program_id(0)
    slot = lax.rem(b, 2)

    def head_copies(bi, sl):
        cps = []
        for h in range(X_HEADS):
            cps.append(pltpu.make_async_copy(k_hbm.at[0, bi, :, h, :], kbuf.at[sl, h], sem.at[sl]))
            cps.append(pltpu.make_async_copy(v_hbm.at[0, bi, :, h, :], vbuf.at[sl, h], sem.at[sl]))
        return cps

    @pl.when(b == 0)
    def _():
        for cp in head_copies(0, 0):
            cp.start()

    for cp in head_copies(b, slot):
        cp.wait()
    nxt = jnp.minimum(b + 1, nb - 1)
    for cp in head_copies(nxt, 1 - slot):
        cp.start()
    for h in range(X_HEADS):
        sl = slice(h * X_DH, (h + 1) * X_DH)
        o_ref[0, :, sl] = _xattn_head(q_ref[0, :, sl], kbuf[slot, h], vbuf[slot, h]).astype(o_ref.dtype)

    @pl.when(b == nb - 1)
    def _():
        for cp in head_copies(nxt, 1 - slot):
            cp.wait()


def _xattn_cache(q, ck, cv):
    B, Tq, _ = q.shape
    nm = ck.shape[2]
    return pl.pallas_call(
        functools.partial(_xattn_cache_body, nb=B),
        grid=(B,),
        in_specs=[pl.BlockSpec((1, Tq, D_MODEL), lambda b: (b, 0, 0)),
                  pl.BlockSpec(memory_space=pl.ANY), pl.BlockSpec(memory_space=pl.ANY)],
        out_specs=pl.BlockSpec((1, Tq, D_MODEL), lambda b: (b, 0, 0)),
        out_shape=jax.ShapeDtypeStruct((B, Tq, D_MODEL), BF16),
        scratch_shapes=[pltpu.VMEM((2, X_HEADS, nm, X_DH), F32), pltpu.VMEM((2, X_HEADS, nm, X_DH), F32),
                        pltpu.SemaphoreType.DMA((2,))],
        compiler_params=_cparams(1),
        name="xattn_cache",
    )(q, ck, cv)


def _xattn(q, mk, mv, tq):
    B, Tq, _ = q.shape
    nm = mk.shape[1]
    mem_blk = (1, nm, D_MODEL)
    return pl.pallas_call(
        _xattn_body,
        grid=(B, Tq // tq),
        in_specs=[pl.BlockSpec((1, tq, D_MODEL), lambda b, i: (b, i, 0)),
                  pl.BlockSpec(mem_blk, lambda b, i: (b, 0, 0)),
                  pl.BlockSpec(mem_blk, lambda b, i: (b, 0, 0))],
        out_specs=pl.BlockSpec((1, tq, D_MODEL), lambda b, i: (b, i, 0)),
        out_shape=jax.ShapeDtypeStruct((B, Tq, D_MODEL), BF16),
        compiler_params=_cparams(2),
        name="xattn",
    )(q, mk, mv)


PACK_SLABS = D_MODEL // (2 * LANES)


def _pack_bf16_pairs(lo, hi):
    lo_bits = pltpu.bitcast(lo.astype(BF16).astype(F32), jnp.uint32)
    hi_bits = pltpu.bitcast(hi.astype(BF16).astype(F32), jnp.uint32)
    return jnp.bitwise_or(jnp.bitwise_and(hi_bits, jnp.uint32(0xFFFF0000)), lax.shift_right_logical(lo_bits, jnp.uint32(16)))


def _unpack_bf16_pairs(w):
    lo = pltpu.bitcast(lax.shift_left(w, jnp.uint32(16)), F32).astype(BF16)
    hi = pltpu.bitcast(jnp.bitwise_and(w, jnp.uint32(0xFFFF0000)), F32).astype(BF16)
    return lo, hi


def _xo_body(x1_ref, o_ref, wo_ref, g_ref, b_ref, wr_ref, br_ref, *rest, tail_rows, nsteps):
    if not tail_rows:
        _xo_rows(x1_ref, o_ref, wo_ref, g_ref, b_ref, wr_ref, br_ref, *rest)
        return
    tx2_ref, txp_ref, x2_ref, xp_ref, ti_ref, tg_ref = rest
    i = pl.program_id(0)

    @pl.when(i < nsteps)
    def _():
        _xo_rows(x1_ref, o_ref, wo_ref, g_ref, b_ref, wr_ref, br_ref, x2_ref, xp_ref, ti_ref, tg_ref)

    @pl.when(i == nsteps)
    def _():
        tm = x2_ref.shape[0]
        x2_ref[0:tail_rows, :] = tx2_ref[...]
        x2_ref[tail_rows:, :] = jnp.zeros((tm - tail_rows, D_MODEL), F32)
        xp_ref[0:tail_rows * PACK_SLABS, :] = txp_ref[...]
        xp_ref[tail_rows * PACK_SLABS:, :] = jnp.zeros(((tm - tail_rows) * PACK_SLABS, LANES), jnp.uint32)


def _xo_rows(x1_ref, o_ref, wo_ref, g_ref, b_ref, wr_ref, br_ref, x2_ref, xp_ref, ti_ref, tg_ref):
    y = _dot(o_ref[...], wo_ref[...])
    x2 = _layer_norm(DN_ALPHA * x1_ref[...] + y, g_ref[...], b_ref[...])
    x2_ref[...] = x2
    tm = x2.shape[0]
    for s in range(PACK_SLABS):
        lo = x2[:, s * LANES:(s + 1) * LANES]
        hi = x2[:, D_MODEL // 2 + s * LANES:D_MODEL // 2 + (s + 1) * LANES]
        xp_ref[pl.ds(s, tm, stride=PACK_SLABS), :] = _pack_bf16_pairs(lo, hi)
    lane = lax.broadcasted_iota(I32, (tm, LANES), 1)
    lanef = lane.astype(F32)
    logits = jnp.where(lane < N_EXPERTS, _dot(x2.astype(BF16), wr_ref[...]) + br_ref[...], -jnp.inf)
    vals, idxs = [], []
    cur = logits
    for _ in range(TOP_K):
        top = jnp.max(cur, axis=1, keepdims=True)
        idx = jnp.min(jnp.where(cur == top, lanef, float(LANES)), axis=1, keepdims=True)
        vals.append(top)
        idxs.append(idx)
        cur = jnp.where(lanef == idx, -jnp.inf, cur)
    es = [jnp.exp(v - vals[0]) for v in vals]
    inv = 1.0 / functools.reduce(lambda a, b: a + b, es)
    ti = jnp.zeros((tm, LANES), F32)
    tg = jnp.zeros((tm, LANES), F32)
    for k in range(TOP_K):
        ti = jnp.where(lane == k, idxs[k], ti)
        tg = jnp.where(lane == k, es[k] * inv, tg)
    ti_ref[...] = ti.astype(I32)
    tg_ref[...] = tg


def _xo_ln2_router(x1, o, wo, ln_g, ln_b, wr, br, tm, tail=None):
    M = x1.shape[0]
    nsteps = M // tm
    tail_rows = 0 if tail is None else tail[0].shape[0]
    assert tail_rows <= tm
    extra = 1 if tail_rows else 0
    last = nsteps - 1
    row_in = lambda w: pl.BlockSpec((tm, w), lambda i: (jnp.minimum(i, last), 0))
    tail_specs = [] if tail is None else [_resident(tail[0].shape), _resident(tail[1].shape)]
    return pl.pallas_call(
        functools.partial(_xo_body, tail_rows=tail_rows, nsteps=nsteps),
        grid=(nsteps + extra,),
        in_specs=[row_in(D_MODEL), row_in(D_MODEL), _resident(wo.shape), _resident(ln_g.shape),
                  _resident(ln_b.shape), _resident(wr.shape), _resident(br.shape)] + tail_specs,
        out_specs=[pl.BlockSpec((tm, D_MODEL), lambda i: (i, 0)),
                   pl.BlockSpec((tm * PACK_SLABS, LANES), lambda i: (i, 0)), row_in(LANES), row_in(LANES)],
        out_shape=[jax.ShapeDtypeStruct((M + extra * tm, D_MODEL), F32),
                   jax.ShapeDtypeStruct(((M + extra * tm) * PACK_SLABS, LANES), jnp.uint32),
                   jax.ShapeDtypeStruct((M, LANES), I32), jax.ShapeDtypeStruct((M, LANES), F32)],
        compiler_params=_cparams(1),
        name="xo_ln2_router",
    )(x1, o, wo, ln_g, ln_b, wr, br, *([] if tail is None else list(tail)))


MOE_SUB_ROWS = 256
MOE_FF_CHUNK = 256


def _ffn_body(te_ref, tr_ref, nu_ref, idc_ref, idn_ref, x_hbm, wg_ref, bg_ref, wu_ref, bu_ref, wd_ref, bd_ref,
              o_ref, stg, sem, xbb, wgb, wub, wdb, *, tm, sb, nch):
    t = pl.program_id(0)
    c = pl.program_id(1)
    n_used = nu_ref[0]
    ps = tm // nch
    cur = lax.rem(t, 2)
    tp = tm * PACK_SLABS

    def row_copy(src_row, sl, r):
        src_row = pl.multiple_of(src_row, PACK_SLABS)
        return pltpu.make_async_copy(x_hbm.at[pl.ds(src_row, PACK_SLABS), :],
                                     stg.at[pl.ds(sl * tp + r * PACK_SLABS, PACK_SLABS), :], sem.at[sl])

    def tile_wait(sl):
        pltpu.make_async_copy(x_hbm.at[pl.ds(0, tp), :], stg.at[pl.ds(sl * tp, tp), :], sem.at[sl]).wait()

    def unpack_tile(sl):
        for s in range(PACK_SLABS):
            lo, hi = _unpack_bf16_pairs(stg[pl.ds(sl * tp + s, tm, stride=PACK_SLABS), :])
            xbb[:, s * LANES:(s + 1) * LANES] = lo
            xbb[:, D_MODEL // 2 + s * LANES:D_MODEL // 2 + (s + 1) * LANES] = hi

    def ffn_rows(rs, wg, wu, wd):
        xb = xbb[rs, :]
        g = jnp.minimum(_dot(xb, wg) + bg_ref[0], SWIGLU_LIMIT)
        u = jnp.clip(_dot(xb, wu) + bu_ref[0], -SWIGLU_LIMIT, SWIGLU_LIMIT)
        hid = (u + 1.0) * (g * _sigmoid(SWIGLU_ALPHA * g))
        o_ref[rs, :] += _dot(hid.astype(BF16), wd)

    @pl.when(t >= n_used)
    def _():
        @pl.when(c == 0)
        def _():
            o_ref[...] = jnp.zeros((tm, D_MODEL), F32)

    @pl.when(t < n_used)
    def _():
        @pl.when((t == 0) & (c == 0))
        def _():
            def issue(r, carry):
                row_copy(idc_ref[0, 0, r], 0, r).start()
                return carry
            lax.fori_loop(0, tm, issue, 0, unroll=8)

        @pl.when(c == 0)
        def _():
            tile_wait(cur)
            unpack_tile(cur)

        def gather_next_group():
            for r in range(ps):
                row_copy(idn_ref[0, 0, c * ps + r], 1 - cur, c * ps + r).start()

        rows = tr_ref[t]

        @pl.when(rows > tm - sb)
        def _():
            @pl.when(c == 0)
            def _():
                o_ref[...] = jnp.broadcast_to(bd_ref[0], (tm, D_MODEL))
            gather_next_group()
            ffn_rows(slice(0, tm), wg_ref[0, 0].astype(BF16), wu_ref[0, 0].astype(BF16), wd_ref[0, 0].astype(BF16))

        @pl.when(rows <= tm - sb)
        def _():
            gather_next_group()
            wgb[...] = wg_ref[0, 0].astype(BF16)
            wub[...] = wu_ref[0, 0].astype(BF16)
            wdb[...] = wd_ref[0, 0].astype(BF16)
            for s in range(tm // sb):
                rs = slice(s * sb, (s + 1) * sb)

                @pl.when((s * sb >= rows) & (c == 0))
                def _():
                    o_ref[rs, :] = jnp.zeros((sb, D_MODEL), F32)

                @pl.when(s * sb < rows)
                def _():
                    @pl.when(c == 0)
                    def _():
                        o_ref[rs, :] = jnp.broadcast_to(bd_ref[0], (sb, D_MODEL))
                    ffn_rows(rs, wgb[...], wub[...], wdb[...])

        @pl.when((t == n_used - 1) & (c == nch - 1))
        def _():
            tile_wait(1 - cur)


def _moe_ffn(x, row_tok, tile_e, tile_rows, n_used, wg, bg, wu, bu, wd, bd, tm, sb):
    P = row_tok.shape[0]
    n_tiles = P // tm
    nch = D_FF // MOE_FF_CHUNK
    fc = MOE_FF_CHUNK
    assert tm % nch == 0 and (tm // nch) % 16 == 0 and nch >= 3

    def tile(t, nu):
        return jnp.minimum(t, jnp.maximum(nu[0] - 1, 0))

    def chunk(t, c, nu):
        return jnp.where(t < nu[0], c, nch - 1)

    idx_blk = (1, 1, tm)
    return pl.pallas_call(
        functools.partial(_ffn_body, tm=tm, sb=sb, nch=nch),
        grid_spec=pltpu.PrefetchScalarGridSpec(
            num_scalar_prefetch=3,
            grid=(n_tiles, nch),
            in_specs=[
                pl.BlockSpec(idx_blk, lambda t, c, te, tr, nu: (tile(t, nu), 0, 0), memory_space=pltpu.SMEM),
                pl.BlockSpec(idx_blk, lambda t, c, te, tr, nu: (tile(t + 1, nu), 0, 0), memory_space=pltpu.SMEM),
                pl.BlockSpec(memory_space=pl.ANY),
                pl.BlockSpec((1, 1, D_MODEL, fc), lambda t, c, te, tr, nu: (0, te[tile(t, nu)], 0, chunk(t, c, nu))),
                pl.BlockSpec((1, 1, fc), lambda t, c, te, tr, nu: (te[tile(t, nu)], 0, chunk(t, c, nu))),
                pl.BlockSpec((1, 1, D_MODEL, fc), lambda t, c, te, tr, nu: (0, te[tile(t, nu)], 0, chunk(t, c, nu))),
                pl.BlockSpec((1, 1, fc), lambda t, c, te, tr, nu: (te[tile(t, nu)], 0, chunk(t, c, nu))),
                pl.BlockSpec((1, 1, fc, D_MODEL), lambda t, c, te, tr, nu: (0, te[tile(t, nu)], chunk(t, c, nu), 0)),
                pl.BlockSpec((1, 1, D_MODEL), lambda t, c, te, tr, nu: (te[tile(t, nu)], 0, 0)),
            ],
            out_specs=pl.BlockSpec((tm, D_MODEL), lambda t, c, te, tr, nu: (t, 0)),
            scratch_shapes=[pltpu.VMEM((2 * tm * PACK_SLABS, LANES), jnp.uint32),
                            pltpu.SemaphoreType.DMA((2,)),
                            pltpu.VMEM((tm, D_MODEL), BF16),
                            pltpu.VMEM((D_MODEL, fc), BF16), pltpu.VMEM((D_MODEL, fc), BF16),
                            pltpu.VMEM((fc, D_MODEL), BF16)]),
        out_shape=jax.ShapeDtypeStruct((P, D_MODEL), F32),
        compiler_params=_cparams(2),
        name="moe_ffn",
    )(tile_e, tile_rows, n_used, row_tok.reshape(n_tiles, 1, tm), row_tok.reshape(n_tiles, 1, tm), x,
      wg, bg, wu, bu, wd, bd)


def _combine_body(posc_ref, posn_ref, x2_ref, tg_ref, ys_hbm, g_ref, b_ref, oa_ref, ob_ref, buf, sem, *,
                  tc, nblk, nblk_a):
    i = pl.program_id(0)
    slot = lax.rem(i, 2)

    def row_copy(src_row, sl, k, j):
        return pltpu.make_async_copy(ys_hbm.at[pl.ds(src_row, 1), :], buf.at[sl, k, pl.ds(j, 1), :], sem.at[sl])

    def slot_wait(sl):
        pltpu.make_async_copy(buf.at[sl], buf.at[sl], sem.at[sl]).wait()

    @pl.when(i == 0)
    def _():
        def issue(j, carry):
            for k in range(TOP_K):
                row_copy(posc_ref[0, 0, j * TOP_K + k], 0, k, j).start()
            return carry
        lax.fori_loop(0, tc, issue, 0, unroll=4)

    slot_wait(slot)
    for j in range(tc):
        for k in range(TOP_K):
            row_copy(posn_ref[0, 0, j * TOP_K + k], 1 - slot, k, j).start()
    y = None
    for k in range(TOP_K):
        term = buf[slot, k] * tg_ref[:, k:k + 1]
        y = term if y is None else y + term
    out = _layer_norm(DN_ALPHA * x2_ref[...] + y, g_ref[...], b_ref[...])

    @pl.when(i < nblk_a)
    def _():
        oa_ref[...] = out

    @pl.when(i >= nblk_a)
    def _():
        ob_ref[...] = out

    @pl.when(i == nblk - 1)
    def _():
        slot_wait(1 - slot)


def _combine_ln3(x2, tg, ys, pos, ln_g, ln_b, tc, split):
    M = pos.shape[0] // TOP_K
    nblk = M // tc
    nblk_a = split // tc
    assert split % tc == 0 and 0 < nblk_a < nblk
    pos3 = pos.reshape(nblk, 1, tc * TOP_K)
    return pl.pallas_call(
        functools.partial(_combine_body, tc=tc, nblk=nblk, nblk_a=nblk_a),
        grid=(nblk,),
        in_specs=[pl.BlockSpec((1, 1, tc * TOP_K), lambda i: (i, 0, 0), memory_space=pltpu.SMEM),
                  pl.BlockSpec((1, 1, tc * TOP_K), lambda i: (jnp.minimum(i + 1, nblk - 1), 0, 0),
                               memory_space=pltpu.SMEM),
                  pl.BlockSpec((tc, D_MODEL), lambda i: (i, 0)),
                  pl.BlockSpec((tc, LANES), lambda i: (i, 0)),
                  pl.BlockSpec(memory_space=pl.ANY),
                  _resident(ln_g.shape), _resident(ln_b.shape)],
        out_specs=[pl.BlockSpec((tc, D_MODEL), lambda i: (jnp.minimum(i, nblk_a - 1), 0)),
                   pl.BlockSpec((tc, D_MODEL), lambda i: (jnp.maximum(i - nblk_a, 0), 0))],
        out_shape=[jax.ShapeDtypeStruct((split, D_MODEL), F32), jax.ShapeDtypeStruct((M - split, D_MODEL), F32)],
        scratch_shapes=[pltpu.VMEM((2, TOP_K, tc, D_MODEL), F32), pltpu.SemaphoreType.DMA((2,))],
        compiler_params=_cparams(1),
        name="moe_combine",
    )(pos3, pos3, x2, tg, ys, ln_g, ln_b)


def _moe(x2, x2_packed, topi, gates, wg, bg, wu, bu, wd, bd, ln_g, ln_b, tm, tc, split):
    ntok = topi.shape[0]
    nk = ntok * TOP_K
    sb = min(tm, MOE_SUB_ROWS)
    n_tiles = -(-(nk + N_EXPERTS * (tm - 1)) // tm)
    e_flat = topi[:, :TOP_K].reshape(nk)
    onehot = (e_flat[:, None] == jnp.arange(N_EXPERTS, dtype=I32)[None, :]).astype(I32)
    csum = jnp.cumsum(onehot, axis=0)
    counts = csum[-1]
    rank = jnp.take_along_axis(csum, e_flat[:, None], axis=1)[:, 0] - 1
    tiles_e = (counts + tm - 1) // tm
    tile_end = jnp.cumsum(tiles_e)
    tile_start = tile_end - tiles_e
    dest = tile_start[e_flat] * tm + rank
    n_used = tile_end[-1:].astype(I32)
    t_ids = jnp.arange(n_tiles, dtype=I32)
    tile_e = jnp.minimum(jnp.searchsorted(tile_end, t_ids, side="right"), N_EXPERTS - 1).astype(I32)
    tile_rows = jnp.clip(counts[tile_e] - (t_ids - tile_start[tile_e]) * tm, 0, tm)
    tile_rows = jnp.where(t_ids < n_used[0], tile_rows, 0).astype(I32)
    P = n_tiles * tm
    first_row = (jnp.arange(nk, dtype=I32) // TOP_K) * PACK_SLABS
    row_src = jnp.zeros((P,), I32).at[dest].set(first_row, unique_indices=True)
    ys = _moe_ffn(x2_packed, row_src, tile_e, tile_rows, n_used, wg, bg, wu, bu, wd, bd, tm, sb)
    return _combine_ln3(x2, gates, ys, dest, ln_g, ln_b, tc, split)


def kernel(x_prompt, x_sample, mem_prompt, state_mlstm_C, state_mlstm_n, state_mlstm_m, cache_swa_k, cache_swa_v, cache_mem_k, cache_mem_v, w_in, b_igate, b_fgate, g_mnorm, w_mix_out, ln1_g, ln1_b, w_xq, w_xk, w_xv, w_xo, ln2_g, ln2_b, w_router, b_router, w_gate, b_gate, w_up, b_up, w_down, b_down, ln3_g, ln3_b):
    assert DEPTH == 1
    B, T, _ = x_prompt.shape
    DB, S, _ = x_sample.shape
    assert B == 1
    n_mem = mem_prompt.shape[1]
    wb = cache_swa_k.shape[2]
    row2 = lambda a: a[0].reshape(1, -1)

    wi = w_in[0]
    c0 = 2 * MIX_QK + 2 * MIX_V
    w_m = wi[:, :c0].astype(BF16)
    w_g = jnp.pad(wi[:, c0:c0 + 2 * M_HEADS], ((0, 0), (0, LANES - 2 * M_HEADS))).astype(BF16)
    w_a = wi[:, c0 + 2 * M_HEADS:].astype(BF16)
    gate_bias = jnp.pad(jnp.concatenate([b_igate[0], b_fgate[0]]), (0, LANES - 2 * M_HEADS)).reshape(1, LANES)
    w = dict(
        wmo=w_mix_out[0].astype(BF16), ln1_g=row2(ln1_g), ln1_b=row2(ln1_b), wxq=w_xq[0].astype(BF16),
        wxo=w_xo[0].astype(BF16), ln2_g=row2(ln2_g), ln2_b=row2(ln2_b),
        wr=jnp.pad(w_router[0], ((0, 0), (0, LANES - N_EXPERTS))).astype(BF16),
        br=jnp.pad(b_router[0], (0, LANES - N_EXPERTS)).reshape(1, LANES),
        wg=w_gate, bg=b_gate[0].reshape(N_EXPERTS, 1, D_FF), wu=w_up, bu=b_up[0].reshape(N_EXPERTS, 1, D_FF),
        wd=w_down, bd=b_down[0].reshape(N_EXPERTS, 1, D_MODEL), ln3_g=row2(ln3_g), ln3_b=row2(ln3_b))
    gm = g_mnorm[0].reshape(1, MIX_V)

    xp = x_prompt.reshape(T, D_MODEL)
    zm = _mm(xp, w_m, BF16, 512, 1024, "proj_mlstm")
    zg = _mm(xp, w_g, F32, 512, LANES, "proj_gates")
    dils = tuple(dil for _, dil in PATTERNS)
    za_views = _proj_attn(xp, w_a, dils)
    wbp = min(wb, T)
    kv_tail = _mm(xp[T - wbp:], w_a[:, MIX_A:], F32, 512, 1024, "proj_kv_tail")
    zeros_c = jnp.zeros((1, M_HEADS, M_DQK, M_DV), F32)
    zeros_n = jnp.zeros((1, M_HEADS, M_DQK), F32)
    zeros_m = jnp.zeros((1, M_HEADS, LANES), F32)
    hm, pC, pn, pm = _mlstm(zm, zg, gate_bias, gm, zeros_c, zeros_n, zeros_m, 1, T // 128)
    pats = [_dilated_pattern(zv, window, dil) for zv, (window, dil) in zip(za_views, PATTERNS)]
    mp = mem_prompt.reshape(n_mem, D_MODEL)
    mem_k = _mm(mp, w_xk[0].astype(BF16), F32, n_mem, 1024, "mem_k")
    mem_v = _mm(mp, w_xv[0].astype(BF16), F32, n_mem, 1024, "mem_v")
    mem_k5 = mem_k.reshape(1, B, n_mem, X_HEADS, X_DH)
    mem_v5 = mem_v.reshape(1, B, n_mem, X_HEADS, X_DH)
    x1_p, q_p = _mix_ln1_q(xp, hm, [p[0] for p in pats], [p[1] for p in pats], dils, w["wmo"], w["ln1_g"], w["ln1_b"],
                           w["wxq"], 256)
    o_p = _xattn(q_p.reshape(1, T, D_MODEL), mem_k.astype(BF16).reshape(1, n_mem, D_MODEL),
                 mem_v.astype(BF16).reshape(1, n_mem, D_MODEL), 512).reshape(T, D_MODEL)

    ns = DB * S
    xs_ = x_sample.reshape(ns, D_MODEL)
    zm_s = _mm(xs_, w_m, BF16, ns, 1024, "proj_mlstm_s")
    zg_s = _mm(xs_, w_g, F32, ns, LANES, "proj_gates_s")
    za_s = _mm(xs_, w_a, F32, ns, 1024, "proj_attn_s")
    pad_rows = 128 - S
    zm_pad = jnp.pad(zm_s.reshape(DB, S, -1), ((0, 0), (0, pad_rows), (0, 0))).reshape(DB * 128, -1)
    lane = jnp.arange(LANES)
    neutral = jnp.where(lane < M_HEADS, NEG_BIG, jnp.where(lane < 2 * M_HEADS, -NEG_BIG, 0.0)).astype(F32)
    zg_pad = jnp.concatenate([zg_s.reshape(DB, S, LANES), jnp.broadcast_to(neutral, (DB, pad_rows, LANES))],
                             axis=1).reshape(DB * 128, LANES)
    m0 = jnp.broadcast_to(state_mlstm_m[0][:, :, None], (DB, M_HEADS, LANES))
    hm_s, sC, sn, sm = _mlstm(zm_pad, zg_pad, gate_bias, gm, state_mlstm_C[0], state_mlstm_n[0], m0, DB, 1)
    hm_s = hm_s.reshape(DB, 128, MIX_V)[:, :S].reshape(ns, MIX_V)
    za_pad = jnp.pad(za_s.reshape(DB, S, -1), ((0, 0), (0, SWA_ROWS - S), (0, 0))).astype(BF16)
    ha_s = _swa_sample(za_pad, cache_swa_k.reshape(DB, wb * A_HEADS, A_DH),
                       cache_swa_v.reshape(DB, wb * A_HEADS, A_DH))[:, :S].reshape(ns, MIX_A)
    x1_s, q_s = _mix_ln1_q(xs_, hm_s, [ha_s], [], (1,), w["wmo"], w["ln1_g"], w["ln1_b"], w["wxq"], ns)
    q_pad = jnp.pad(q_s.reshape(DB, S, D_MODEL), ((0, 0), (0, SWA_ROWS - S), (0, 0)))
    o_s = _xattn_cache(q_pad, cache_mem_k, cache_mem_v)[:, :S].reshape(ns, D_MODEL)
    x2_s, xk_s, topi_s, gates_s = _xo_ln2_router(x1_s, o_s, w["wxo"], w["ln2_g"], w["ln2_b"], w["wr"], w["br"], ns)

    x2_all, xk_all, topi_p, gates_p = _xo_ln2_router(x1_p, o_p, w["wxo"], w["ln2_g"], w["ln2_b"], w["wr"], w["br"],
                                                     256, tail=(x2_s, xk_s))
    yp, ys_ = _moe(x2_all, xk_all, jnp.concatenate([topi_p, topi_s]),
                   jnp.concatenate([gates_p, gates_s]), w["wg"], w["bg"], w["wu"], w["bu"], w["wd"], w["bd"],
                   w["ln3_g"], w["ln3_b"], 1024, 128, T)

    return (yp.reshape(B, T, D_MODEL), ys_.reshape(DB, S, D_MODEL),
            pC[None], pn[None], pm[:, :, 0][None],
            kv_tail[:, :MIX_A].reshape(1, B, wbp, A_HEADS, A_DH), kv_tail[:, MIX_A:].reshape(1, B, wbp, A_HEADS, A_DH),
            mem_k5, mem_v5,
            sC[None], sn[None], sm[:, :, 0][None],
            za_s[:, MIX_A:2 * MIX_A].reshape(1, DB, S, A_HEADS, A_DH),
            za_s[:, 2 * MIX_A:].reshape(1, DB, S, A_HEADS, A_DH))
```

```python
import functools

import jax
import jax.numpy as jnp
from jax import lax
from jax.experimental import pallas as pl
from jax.experimental.pallas import tpu as pltpu

F32, BF16, I32 = jnp.float32, jnp.bfloat16, jnp.int32

D_MODEL = 2048
DEPTH = 1
M_HEADS, M_DQK, M_DV = 4, 128, 256
HEAD_EPS = 1e-6
A_HEADS, A_DH = 8, 128
PATTERNS = ((128, 1), (512, 4), (2048, 16))
BAND_BLOCK = 128
X_HEADS = 4
X_DH = D_MODEL // X_HEADS
N_EXPERTS, TOP_K, D_FF = 32, 4, 2048
SWIGLU_LIMIT, SWIGLU_ALPHA = 7.0, 1.702
DN_ALPHA = (2 * DEPTH) ** 0.25
LN_EPS = 1e-5
MIX_V = M_HEADS * M_DV
MIX_A = A_HEADS * A_DH
MIX_QK = M_HEADS * M_DQK

LANES = 128
VMEM_LIMIT_BYTES = 58 * 1024 * 1024

NEG_BIG = -1e30


def _cparams(n_axes, vmem=VMEM_LIMIT_BYTES):
    return pltpu.CompilerParams(dimension_semantics=("arbitrary",) * n_axes, vmem_limit_bytes=vmem)


def _dot(a, b):
    return jnp.dot(a, b, preferred_element_type=F32)


def _dot_nt(a, b):
    return lax.dot_general(a, b, (((1,), (1,)), ((), ())), preferred_element_type=F32)


def _dot_tn(a, b):
    return lax.dot_general(a, b, (((0,), (0,)), ((), ())), preferred_element_type=F32)


def _log_sigmoid(x):
    return jnp.minimum(x, 0.0) - jnp.log(1.0 + jnp.exp(-jnp.abs(x)))


def _sigmoid(x):
    return 1.0 / (1.0 + jnp.exp(-x))


def _layer_norm(v, g, b):
    mu = jnp.mean(v, axis=-1, keepdims=True)
    d = v - mu
    var = jnp.mean(d * d, axis=-1, keepdims=True)
    return d * lax.rsqrt(var + LN_EPS) * g + b


def _resident(shape):
    nd = len(shape)
    return pl.BlockSpec(shape, lambda *_: (0,) * nd, pipeline_mode=pl.Buffered(1))


def _mm_body(x_ref, w_ref, o_ref):
    o_ref[...] = _dot(x_ref[...].astype(BF16), w_ref[...]).astype(o_ref.dtype)


def _mm(x, w, out_dtype, tm, tn, name):
    M, K = x.shape
    N = w.shape[1]
    assert M % tm == 0 and N % tn == 0
    return pl.pallas_call(
        _mm_body,
        grid=(M // tm, N // tn),
        in_specs=[pl.BlockSpec((tm, K), lambda i, j: (i, 0)), pl.BlockSpec((K, tn), lambda i, j: (0, j))],
        out_specs=pl.BlockSpec((tm, tn), lambda i, j: (i, j)),
        out_shape=jax.ShapeDtypeStruct((M, N), out_dtype),
        compiler_params=_cparams(2),
        name=name,
    )(x, w)


def _mm_multi_body(x_ref, *refs, tn):
    n = len(refs) // 2
    xb = x_ref[...].astype(BF16)
    for w_ref, o_ref in zip(refs[:n], refs[n:]):
        width = w_ref.shape[1]
        step = min(tn, width)
        for j in range(width // step):
            o_ref[:, j * step:(j + 1) * step] = _dot(xb, w_ref[:, j * step:(j + 1) * step]).astype(o_ref.dtype)


def _mm_multi(x, ws, out_dtypes, tm, name, tn=1024):
    M, K = x.shape
    assert M % tm == 0 and all(w.shape[1] % min(tn, w.shape[1]) == 0 for w in ws)
    return pl.pallas_call(
        functools.partial(_mm_multi_body, tn=tn),
        grid=(M // tm,),
        in_specs=[pl.BlockSpec((tm, K), lambda i: (i, 0))] + [_resident(w.shape) for w in ws],
        out_specs=[pl.BlockSpec((tm, w.shape[1]), lambda i: (i, 0)) for w in ws],
        out_shape=[jax.ShapeDtypeStruct((M, w.shape[1]), dt) for w, dt in zip(ws, out_dtypes)],
        compiler_params=_cparams(1),
        name=name,
    )(x, *ws)


def _mlstm_body(q_ref, k_ref, v_ref, zo_ref, zg_ref, gb_ref, gm_ref, c0_ref, n0_ref, m0_ref,
                hm_ref, c_ref, n_ref, m_ref, *, L, sub):
    @pl.when(pl.program_id(1) == 0)
    def _():
        c_ref[...] = c0_ref[...]
        n_ref[...] = n0_ref[...]
        m_ref[...] = m0_ref[...]

    for ci in range(sub):
        _mlstm_chunk(slice(ci * L, (ci + 1) * L), q_ref, k_ref, v_ref, zo_ref, zg_ref, gb_ref, gm_ref,
                     hm_ref, c_ref, n_ref, m_ref, L)


def _mlstm_chunk(rows, q_ref, k_ref, v_ref, zo_ref, zg_ref, gb_ref, gm_ref, hm_ref, c_ref, n_ref, m_ref, L):
    scale = M_DQK ** -0.5
    g = zg_ref[rows, :] + gb_ref[...]
    gt = g.T
    row = lax.broadcasted_iota(I32, (L, L), 0)
    col = lax.broadcasted_iota(I32, (L, L), 1)
    tri = row >= col
    for h in range(M_HEADS):
        i_col = g[:, h:h + 1]
        f_col = _log_sigmoid(g[:, M_HEADS + h:M_HEADS + h + 1])
        i_row = gt[h:h + 1, :]
        f_row = _log_sigmoid(gt[M_HEADS + h:M_HEADS + h + 1, :])
        q = q_ref[rows, h * M_DQK:(h + 1) * M_DQK]
        k = k_ref[rows, h * M_DQK:(h + 1) * M_DQK]
        v = v_ref[rows, h * M_DV:(h + 1) * M_DV]
        zo = zo_ref[rows, h * M_DV:(h + 1) * M_DV].astype(F32)
        C = c_ref[0, h]
        n = n_ref[0, h:h + 1, :]
        m = m_ref[0, h:h + 1, 0:1]
        b_col = jnp.sum(jnp.where(tri, f_row, 0.0), axis=1, keepdims=True)
        b_row = jnp.sum(jnp.where(row <= col, f_col, 0.0), axis=0, keepdims=True)
        logd = jnp.where(tri, b_col - b_row + i_row, -jnp.inf)
        inter = b_col + m
        mt = jnp.maximum(inter, jnp.max(logd, axis=1, keepdims=True))
        sd = _dot_nt(q, k) * scale * jnp.exp(logd - mt)
        sc = jnp.exp(inter - mt)
        num = _dot(sd.astype(BF16), v) + sc * _dot(q, C.astype(BF16))
        den = jnp.sum(sd, axis=1, keepdims=True) + sc * jnp.sum(q.astype(F32) * n, axis=1, keepdims=True)
        hh = num / jnp.maximum(jnp.abs(den), jnp.exp(-mt))
        hn = hh * lax.rsqrt(jnp.mean(hh * hh, axis=1, keepdims=True) + HEAD_EPS)
        out = hn * gm_ref[:, h * M_DV:(h + 1) * M_DV] * _sigmoid(zo)
        hm_ref[rows, h * M_DV:(h + 1) * M_DV] = out.astype(hm_ref.dtype)
        bl = b_col[L - 1:L, :]
        ml = mt[L - 1:L, :]
        w_col = jnp.exp(bl - b_col + i_col - ml)
        scl = jnp.exp(bl + m - ml)
        kw = k.astype(F32) * (w_col * scale)
        c_ref[0, h] = scl * C + _dot_tn(kw.astype(BF16), v)
        n_ref[0, h:h + 1, :] = scl * n + jnp.sum(kw, axis=0, keepdims=True)
        m_ref[0, h:h + 1, :] = jnp.broadcast_to(ml, (1, LANES))


def _mlstm(zm, zg, gate_bias, g_mnorm, c0, n0, m0, B, nc, sub=1, L=128):
    T = B * nc * L
    assert zm.shape == (T, 2 * MIX_QK + 2 * MIX_V) and nc % sub == 0
    nc, L_chunk, L = nc // sub, L, L * sub
    row_blk = lambda b, c: b * nc + c
    state_specs = [pl.BlockSpec((1, M_HEADS, M_DQK, M_DV), lambda b, c: (b, 0, 0, 0)),
                   pl.BlockSpec((1, M_HEADS, M_DQK), lambda b, c: (b, 0, 0)),
                   pl.BlockSpec((1, M_HEADS, LANES), lambda b, c: (b, 0, 0))]
    return pl.pallas_call(
        functools.partial(_mlstm_body, L=L_chunk, sub=sub),
        grid=(B, nc),
        in_specs=[pl.BlockSpec((L, MIX_QK), lambda b, c: (row_blk(b, c), 0)),
                  pl.BlockSpec((L, MIX_QK), lambda b, c: (row_blk(b, c), 1)),
                  pl.BlockSpec((L, MIX_V), lambda b, c: (row_blk(b, c), 1)),
                  pl.BlockSpec((L, MIX_V), lambda b, c: (row_blk(b, c), 2)),
                  pl.BlockSpec((L, LANES), lambda b, c: (row_blk(b, c), 0)),
                  pl.BlockSpec((1, LANES), lambda b, c: (0, 0)),
                  pl.BlockSpec((1, MIX_V), lambda b, c: (0, 0))] + state_specs,
        out_specs=[pl.BlockSpec((L, MIX_V), lambda b, c: (row_blk(b, c), 0))] + state_specs,
        out_shape=[jax.ShapeDtypeStruct((T, MIX_V), BF16),
                   jax.ShapeDtypeStruct((B, M_HEADS, M_DQK, M_DV), F32),
                   jax.ShapeDtypeStruct((B, M_HEADS, M_DQK), F32),
                   jax.ShapeDtypeStruct((B, M_HEADS, LANES), F32)],
        compiler_params=_cparams(2),
        name="mlstm",
    )(zm, zm, zm, zm, zg, gate_bias, g_mnorm, c0, n0, m0)


DIL_BLOCKS_PER_STEP = 2


def _dil_body(q_ref, kc_ref, kp_ref, vc_ref, vp_ref, o_ref, l_ref, *, span):
    scale = A_DH ** -0.5
    n = pl.program_id(1)
    row = lax.broadcasted_iota(I32, (BAND_BLOCK, BAND_BLOCK), 0)
    col = lax.broadcasted_iota(I32, (BAND_BLOCK, BAND_BLOCK), 1)
    first = jnp.where(n > 0, 0, 2 * BAND_BLOCK)
    mask_c = row >= col
    lane = lax.broadcasted_iota(I32, (BAND_BLOCK, LANES), 1)
    ones = jnp.ones((BAND_BLOCK, A_DH), BF16)
    for j in range(DIL_BLOCKS_PER_STEP):
        rows = slice(j * BAND_BLOCK, (j + 1) * BAND_BLOCK)
        prev = slice((j - 1) * BAND_BLOCK, j * BAND_BLOCK)
        mask_p = (BAND_BLOCK + row - col + (first if j == 0 else 0)) <= span
        lse_tile = jnp.zeros((BAND_BLOCK, LANES), F32)
        for h in range(A_HEADS):
            sl = slice(h * A_DH, (h + 1) * A_DH)
            q = q_ref[rows, sl]
            kp = kp_ref[:, sl] if j == 0 else kc_ref[prev, sl]
            vp = vp_ref[:, sl] if j == 0 else vc_ref[prev, sl]
            sp = jnp.where(mask_p, _dot_nt(q, kp) * scale, -jnp.inf)
            sc = jnp.where(mask_c, _dot_nt(q, kc_ref[rows, sl]) * scale, -jnp.inf)
            m = jnp.max(jnp.maximum(sp, sc), axis=1, keepdims=True)
            pp = jnp.exp(sp - m).astype(BF16)
            pc = jnp.exp(sc - m).astype(BF16)
            oa = (_dot(pp, jnp.concatenate([vp, ones], axis=1))
                  + _dot(pc, jnp.concatenate([vc_ref[rows, sl], ones], axis=1)))
            l = oa[:, A_DH:A_DH + 1]
            o_ref[rows, sl] = (oa[:, :A_DH] / l).astype(o_ref.dtype)
            lse_tile = jnp.where(lane == h, m + jnp.log(l), lse_tile)
        l_ref[rows, :] = lse_tile


PROJ_ATTN_ROWS = 512


def _proj_attn_body(x_ref, w_ref, *refs, dils):
    o_refs, tail_ref, acc = refs[:len(dils)], refs[len(dils)], refs[len(dils) + 1]
    tm = x_ref.shape[0]
    nslab = acc.shape[0]
    tn = nslab * LANES
    xb = x_ref[...].astype(BF16)
    for j in range(w_ref.shape[1] // tn):
        z = _dot(xb, w_ref[:, j * tn:(j + 1) * tn])
        if j * tn >= MIX_A:
            tail_ref[:, j * tn - MIX_A:(j + 1) * tn - MIX_A] = z
        for dil, o_ref in zip(dils, o_refs):
            if dil == 1:
                o_ref[:, j * tn:(j + 1) * tn] = z.astype(o_ref.dtype)
        for s in range(nslab):
            acc[s] = z[:, s * LANES:(s + 1) * LANES]
        for dil, o_ref in zip(dils, o_refs):
            if dil == 1:
                continue
            rows = tm // dil
            for r in range(dil):
                for s in range(nslab):
                    c0 = r * w_ref.shape[1] + j * tn + s * LANES
                    o_ref[:, c0:c0 + LANES] = acc[s, pl.ds(r, rows, stride=dil), :].astype(o_ref.dtype)


def _proj_attn(x, w, dils, tail_rows):
    T, K = x.shape
    N = w.shape[1]
    tm, tn = PROJ_ATTN_ROWS, 1024
    assert T % tm == 0 and N % tn == 0 and all(tm % (16 * d) == 0 for d in dils)
    assert tail_rows % tm == 0 and MIX_A % tn == 0 and N == 3 * MIX_A
    first_tail = (T - tail_rows) // tm
    return pl.pallas_call(
        functools.partial(_proj_attn_body, dils=dils),
        grid=(T // tm,),
        in_specs=[pl.BlockSpec((tm, K), lambda i: (i, 0)), _resident(w.shape)],
        out_specs=[pl.BlockSpec((tm // d, d * N), lambda i: (i, 0)) for d in dils]
        + [pl.BlockSpec((tm, 2 * MIX_A), lambda i: (jnp.maximum(i - first_tail, 0), 0))],
        out_shape=[jax.ShapeDtypeStruct((T // d, d * N), BF16) for d in dils]
        + [jax.ShapeDtypeStruct((tail_rows, 2 * MIX_A), F32)],
        scratch_shapes=[pltpu.VMEM((tn // LANES, tm, LANES), F32)],
        compiler_params=_cparams(1),
        name="proj_attn",
    )(x, w)


def _dilated_pattern(zv, window, dil):
    N = zv.shape[0]
    span = window // dil
    step_rows = DIL_BLOCKS_PER_STEP * BAND_BLOCK
    assert zv.shape[1] == dil * 3 * MIX_A and N % step_rows == 0 and BAND_BLOCK - 1 <= span
    nb = N // step_rows
    prev = lambda n: jnp.maximum(n * DIL_BLOCKS_PER_STEP - 1, 0)
    blk = (step_rows, MIX_A)
    pblk = (BAND_BLOCK, MIX_A)
    o, lse = pl.pallas_call(
        functools.partial(_dil_body, span=span),
        grid=(dil, nb),
        in_specs=[pl.BlockSpec(blk, lambda r, n: (n, 3 * r)),
                  pl.BlockSpec(blk, lambda r, n: (n, 3 * r + 1)),
                  pl.BlockSpec(pblk, lambda r, n: (prev(n), 3 * r + 1)),
                  pl.BlockSpec(blk, lambda r, n: (n, 3 * r + 2)),
                  pl.BlockSpec(pblk, lambda r, n: (prev(n), 3 * r + 2))],
        out_specs=[pl.BlockSpec(blk, lambda r, n: (n, r)),
                   pl.BlockSpec((step_rows, LANES), lambda r, n: (n, r))],
        out_shape=[jax.ShapeDtypeStruct((N, dil * MIX_A), BF16),
                   jax.ShapeDtypeStruct((N, dil * LANES), F32)],
        compiler_params=_cparams(2),
        name=f"dilated_d{dil}",
    )(zv, zv, zv, zv, zv)
    return o, lse


SWA_ROWS = 16


def _swa_body(q_ref, kn_ref, vn_ref, pos_ref, kr_ref, vr_ref, k_hbm, v_hbm, o_ref, oldk, oldv, sem, *,
              wb, nb, n_new, n_old, big_dil, dense):
    scale = A_DH ** -0.5
    b = pl.program_id(0)
    slot = lax.rem(b, 2)
    grp = n_new * A_HEADS

    def old_copies(bi, sl):
        cps = []
        for a in range(n_old):
            src = pl.ds(a * big_dil * A_HEADS, grp)
            dst = pl.ds(sl * n_old * grp + a * grp, grp)
            cps.append(pltpu.make_async_copy(k_hbm.at[bi, src, :], oldk.at[dst, :], sem.at[sl]))
            cps.append(pltpu.make_async_copy(v_hbm.at[bi, src, :], oldv.at[dst, :], sem.at[sl]))
        return cps

    @pl.when(b == 0)
    def _():
        for cp in old_copies(0, 0):
            cp.start()

    for cp in old_copies(b, slot):
        cp.wait()
    nxt = jnp.minimum(b + 1, nb - 1)
    for cp in old_copies(nxt, 1 - slot):
        cp.start()

    ncols = n_new * n_old + dense
    s_k = lax.broadcasted_iota(I32, (SWA_ROWS, ncols), 0)
    d_k = wb + s_k - pos_ref[...]
    s_n = lax.broadcasted_iota(I32, (SWA_ROWS, SWA_ROWS), 0)
    p_n = lax.broadcasted_iota(I32, (SWA_ROWS, SWA_ROWS), 1)
    d_n = s_n - p_n
    masks = []
    for window, dil in PATTERNS:
        assert dil & (dil - 1) == 0 and window % dil == 0
        mk = jnp.where(jnp.bitwise_and(d_k, dil - 1) == 0, d_k, window + 1) <= window
        mn = jnp.where(jnp.bitwise_and(d_n, dil - 1) == 0, jnp.where(d_n >= 0, d_n, window + 1), window + 1) <= window
        masks.append((mk, mn))
    for h in range(A_HEADS):
        sl = slice(h * A_DH, (h + 1) * A_DH)
        q = q_ref[0, :, sl]

        def head_rows(old, rec_ref):
            parts = [old[pl.ds(slot * n_old * grp + s * A_HEADS + h, n_old, stride=grp), :] for s in range(n_new)]
            parts.append(rec_ref[0, pl.ds(h, dense, stride=A_HEADS), :])
            return jnp.concatenate(parts, axis=0).astype(BF16)

        kb = head_rows(oldk, kr_ref)
        vb = head_rows(oldv, vr_ref)
        kn = kn_ref[0, :, sl]
        vn = vn_ref[0, :, sl]
        s_cache = _dot_nt(q, kb) * scale
        s_new = _dot_nt(q, kn) * scale
        ps, lses = [], []
        for mk, mn in masks:
            sk = jnp.where(mk, s_cache, -jnp.inf)
            sn = jnp.where(mn, s_new, -jnp.inf)
            m = jnp.maximum(jnp.max(sk, axis=1, keepdims=True), jnp.max(sn, axis=1, keepdims=True))
            pk = jnp.exp(sk - m)
            pn = jnp.exp(sn - m)
            l = jnp.sum(pk, axis=1, keepdims=True) + jnp.sum(pn, axis=1, keepdims=True)
            ps.append((pk, pn, l))
            lses.append(m + jnp.log(l))
        top = functools.reduce(jnp.maximum, lses)
        es = [jnp.exp(x - top) for x in lses]
        tot = functools.reduce(lambda a, b: a + b, es)
        pk_all = None
        pn_all = None
        for (pk, pn, l), e in zip(ps, es):
            coef = e / (tot * l)
            pk_all = pk * coef if pk_all is None else pk_all + pk * coef
            pn_all = pn * coef if pn_all is None else pn_all + pn * coef
        o = _dot(pk_all.astype(BF16), vb) + _dot(pn_all.astype(BF16), vn)
        o_ref[0, :, sl] = o.astype(o_ref.dtype)

    @pl.when(b == nb - 1)
    def _():
        for cp in old_copies(nxt, 1 - slot):
            cp.wait()


def _swa_sample(za_pad, kbuf, vbuf, n_new):
    B, wb = kbuf.shape[0], kbuf.shape[1] // A_HEADS
    big_dil = PATTERNS[-1][1]
    dense = max(w for w, _ in PATTERNS[:-1])
    assert all(w <= PATTERNS[-1][0] and d <= big_dil for w, d in PATTERNS[:-1])
    assert wb % dense == 0 and dense % big_dil == 0 and wb % big_dil == 0 and n_new <= big_dil
    n_old = (wb - dense) // big_dil
    assert n_old % 8 == 0
    grp = n_new * A_HEADS
    old_pos = (jnp.arange(n_old, dtype=I32)[None, :] * big_dil + jnp.arange(n_new, dtype=I32)[:, None]).reshape(-1)
    pos = jnp.concatenate([old_pos, wb - dense + jnp.arange(dense, dtype=I32)]).reshape(1, -1)
    new_blk = (1, SWA_ROWS, MIX_A)
    rec_blk = (1, dense * A_HEADS, A_DH)
    last = wb // dense - 1
    return pl.pallas_call(
        functools.partial(_swa_body, wb=wb, nb=B, n_new=n_new, n_old=n_old, big_dil=big_dil, dense=dense),
        grid=(B,),
        in_specs=[pl.BlockSpec(new_blk, lambda b: (b, 0, 0)),
                  pl.BlockSpec(new_blk, lambda b: (b, 0, 1)),
                  pl.BlockSpec(new_blk, lambda b: (b, 0, 2)),
                  _resident(pos.shape),
                  pl.BlockSpec(rec_blk, lambda b: (b, last, 0)),
                  pl.BlockSpec(rec_blk, lambda b: (b, last, 0)),
                  pl.BlockSpec(memory_space=pl.ANY), pl.BlockSpec(memory_space=pl.ANY)],
        out_specs=pl.BlockSpec(new_blk, lambda b: (b, 0, 0)),
        out_shape=jax.ShapeDtypeStruct((B, SWA_ROWS, MIX_A), BF16),
        scratch_shapes=[pltpu.VMEM((2 * n_old * grp, A_DH), F32), pltpu.VMEM((2 * n_old * grp, A_DH), F32),
                        pltpu.SemaphoreType.DMA((2,))],
        compiler_params=_cparams(1),
        name="swa_sample",
    )(za_pad, za_pad, za_pad, pos, kbuf, vbuf, kbuf, vbuf)


def _mix_body(*refs, dils):
    n_pat = len(dils)
    x_ref, hm_ref = refs[0], refs[1]
    n_lse = n_pat if n_pat > 1 else 0
    o_refs = refs[2:2 + n_pat]
    l_refs = refs[2 + n_pat:2 + n_pat + n_lse]
    wmo_ref, g_ref, b_ref, wq_ref, x1_ref, q_ref = refs[2 + n_pat + n_lse:8 + n_pat + n_lse]
    if n_pat == 1:
        ha = o_refs[0][...]
    else:
        o_nat, l_nat = refs[8 + n_pat + n_lse:]
        tm = x_ref.shape[0]
        for p, dil in enumerate(dils):
            if dil == 1:
                continue
            rows = tm // dil
            for r in range(dil):
                dst = pl.ds(r, rows, stride=dil)
                l_nat[p, dst, :] = l_refs[p][:, r * LANES:(r + 1) * LANES]
                for h in range(A_HEADS):
                    c0 = r * MIX_A + h * A_DH
                    o_nat[p, h, dst, :] = o_refs[p][:, c0:c0 + A_DH].astype(F32)

        def lse_of(p, h):
            return l_refs[p][:, h:h + 1] if dils[p] == 1 else l_nat[p, :, h:h + 1]

        def out_of(p, h):
            return o_refs[p][:, h * A_DH:(h + 1) * A_DH].astype(F32) if dils[p] == 1 else o_nat[p, h]

        cols = []
        for h in range(A_HEADS):
            ls = [lse_of(p, h) for p in range(n_pat)]
            top = functools.reduce(jnp.maximum, ls)
            es = [jnp.exp(x - top) for x in ls]
            inv = 1.0 / functools.reduce(lambda a, b: a + b, es)
            acc = None
            for p, e in enumerate(es):
                term = out_of(p, h) * (e * inv)
                acc = term if acc is None else acc + term
            cols.append(acc.astype(BF16))
        ha = jnp.concatenate(cols, axis=1)
    mix = _dot(hm_ref[...], wmo_ref[0:MIX_V, :]) + _dot(ha, wmo_ref[MIX_V:MIX_V + MIX_A, :])
    x1 = _layer_norm(DN_ALPHA * x_ref[...] + mix, g_ref[...], b_ref[...])
    x1_ref[...] = x1
    q_ref[...] = _dot(x1.astype(BF16), wq_ref[...]).astype(q_ref.dtype)


def _mix_ln1_q(x, hm, outs, lses, dils, wmo, ln_g, ln_b, wq, tm):
    M = x.shape[0]
    n_pat = len(outs)
    assert len(lses) == (n_pat if n_pat > 1 else 0) and len(dils) == n_pat
    assert all(tm % (16 * d) == 0 for d in dils)
    row = lambda w: pl.BlockSpec((tm, w), lambda i: (i, 0))
    view = lambda w, d: pl.BlockSpec((tm // d, d * w), lambda i: (i, 0))
    scratch = []
    if n_pat > 1:
        scratch = [pltpu.VMEM((n_pat, A_HEADS, tm, A_DH), F32), pltpu.VMEM((n_pat, tm, LANES), F32)]
    return pl.pallas_call(
        functools.partial(_mix_body, dils=tuple(dils)),
        grid=(M // tm,),
        in_specs=[row(D_MODEL), row(MIX_V)] + [view(MIX_A, d) for d in dils]
        + [view(LANES, d) for d in dils[:len(lses)]]
        + [_resident(wmo.shape), _resident(ln_g.shape), _resident(ln_b.shape), _resident(wq.shape)],
        out_specs=[row(D_MODEL), row(D_MODEL)],
        out_shape=[jax.ShapeDtypeStruct((M, D_MODEL), F32), jax.ShapeDtypeStruct((M, D_MODEL), BF16)],
        scratch_shapes=scratch,
        compiler_params=_cparams(1),
        name="mix_ln1_q",
    )(x, hm, *outs, *lses, wmo, ln_g, ln_b, wq)


def _xattn_body(q_ref, k_ref, v_ref, o_ref):
    for h in range(X_HEADS):
        sl = slice(h * X_DH, (h + 1) * X_DH)
        o_ref[0, :, sl] = _xattn_head(q_ref[0, :, sl], k_ref[0, :, sl], v_ref[0, :, sl]).astype(o_ref.dtype)


def _xattn_head(q, k, v):
    s = _dot_nt(q, k.astype(BF16)) * (X_DH ** -0.5)
    p = jnp.exp(s - jnp.max(s, axis=1, keepdims=True))
    p = p / jnp.sum(p, axis=1, keepdims=True)
    return _dot(p.astype(BF16), v.astype(BF16))


def _xattn_cache_body(q_ref, k_hbm, v_hbm, o_ref, kbuf, vbuf, sem, *, nb):
    b = pl.program_id(0)
    slot = lax.rem(b, 2)

    def head_copies(bi, sl):
        cps = []
        for h in range(X_HEADS):
            cps.append(pltpu.make_async_copy(k_hbm.at[0, bi, :, h, :], kbuf.at[sl, h], sem.at[sl]))
            cps.append(pltpu.make_async_copy(v_hbm.at[0, bi, :, h, :], vbuf.at[sl, h], sem.at[sl]))
        return cps

    @pl.when(b == 0)
    def _():
        for cp in head_copies(0, 0):
            cp.start()

    for cp in head_copies(b, slot):
        cp.wait()
    nxt = jnp.minimum(b + 1, nb - 1)
    for cp in head_copies(nxt, 1 - slot):
        cp.start()
    for h in range(X_HEADS):
        sl = slice(h * X_DH, (h + 1) * X_DH)
        o_ref[0, :, sl] = _xattn_head(q_ref[0, :, sl], kbuf[slot, h], vbuf[slot, h]).astype(o_ref.dtype)

    @pl.when(b == nb - 1)
    def _():
        for cp in head_copies(nxt, 1 - slot):
            cp.wait()


def _xattn_cache(q, ck, cv):
    B, Tq, _ = q.shape
    nm = ck.shape[2]
    return pl.pallas_call(
        functools.partial(_xattn_cache_body, nb=B),
        grid=(B,),
        in_specs=[pl.BlockSpec((1, Tq, D_MODEL), lambda b: (b, 0, 0)),
                  pl.BlockSpec(memory_space=pl.ANY), pl.BlockSpec(memory_space=pl.ANY)],
        out_specs=pl.BlockSpec((1, Tq, D_MODEL), lambda b: (b, 0, 0)),
        out_shape=jax.ShapeDtypeStruct((B, Tq, D_MODEL), BF16),
        scratch_shapes=[pltpu.VMEM((2, X_HEADS, nm, X_DH), F32), pltpu.VMEM((2, X_HEADS, nm, X_DH), F32),
                        pltpu.SemaphoreType.DMA((2,))],
        compiler_params=_cparams(1),
        name="xattn_cache",
    )(q, ck, cv)


def _xattn(q, mk, mv, tq):
    B, Tq, _ = q.shape
    nm = mk.shape[1]
    mem_blk = (1, nm, D_MODEL)
    return pl.pallas_call(
        _xattn_body,
        grid=(B, Tq // tq),
        in_specs=[pl.BlockSpec((1, tq, D_MODEL), lambda b, i: (b, i, 0)),
                  pl.BlockSpec(mem_blk, lambda b, i: (b, 0, 0)),
                  pl.BlockSpec(mem_blk, lambda b, i: (b, 0, 0))],
        out_specs=pl.BlockSpec((1, tq, D_MODEL), lambda b, i: (b, i, 0)),
        out_shape=jax.ShapeDtypeStruct((B, Tq, D_MODEL), BF16),
        compiler_params=_cparams(2),
        name="xattn",
    )(q, mk, mv)


PACK_SLABS = D_MODEL // (2 * LANES)


def _pack_bf16_pairs(lo, hi):
    lo_bits = pltpu.bitcast(lo.astype(BF16).astype(F32), jnp.uint32)
    hi_bits = pltpu.bitcast(hi.astype(BF16).astype(F32), jnp.uint32)
    return jnp.bitwise_or(jnp.bitwise_and(hi_bits, jnp.uint32(0xFFFF0000)), lax.shift_right_logical(lo_bits, jnp.uint32(16)))


def _unpack_bf16_pairs(w):
    lo = pltpu.bitcast(lax.shift_left(w, jnp.uint32(16)), F32).astype(BF16)
    hi = pltpu.bitcast(jnp.bitwise_and(w, jnp.uint32(0xFFFF0000)), F32).astype(BF16)
    return lo, hi


def _xo_body(x1_ref, o_ref, wo_ref, g_ref, b_ref, wr_ref, br_ref, *rest, tail_rows, nsteps):
    if not tail_rows:
        _xo_rows(x1_ref, o_ref, wo_ref, g_ref, b_ref, wr_ref, br_ref, *rest)
        return
    tx2_ref, txp_ref, x2_ref, xp_ref, ti_ref, tg_ref = rest
    i = pl.program_id(0)

    @pl.when(i < nsteps)
    def _():
        _xo_rows(x1_ref, o_ref, wo_ref, g_ref, b_ref, wr_ref, br_ref, x2_ref, xp_ref, ti_ref, tg_ref)

    @pl.when(i == nsteps)
    def _():
        tm = x2_ref.shape[0]
        x2_ref[0:tail_rows, :] = tx2_ref[...]
        x2_ref[tail_rows:, :] = jnp.zeros((tm - tail_rows, D_MODEL), F32)
        xp_ref[0:tail_rows * PACK_SLABS, :] = txp_ref[...]
        xp_ref[tail_rows * PACK_SLABS:, :] = jnp.zeros(((tm - tail_rows) * PACK_SLABS, LANES), jnp.uint32)


def _xo_rows(x1_ref, o_ref, wo_ref, g_ref, b_ref, wr_ref, br_ref, x2_ref, xp_ref, ti_ref, tg_ref):
    y = _dot(o_ref[...], wo_ref[...])
    x2 = _layer_norm(DN_ALPHA * x1_ref[...] + y, g_ref[...], b_ref[...])
    x2_ref[...] = x2
    tm = x2.shape[0]
    for s in range(PACK_SLABS):
        lo = x2[:, s * LANES:(s + 1) * LANES]
        hi = x2[:, D_MODEL // 2 + s * LANES:D_MODEL // 2 + (s + 1) * LANES]
        xp_ref[pl.ds(s, tm, stride=PACK_SLABS), :] = _pack_bf16_pairs(lo, hi)
    lane = lax.broadcasted_iota(I32, (tm, LANES), 1)
    lanef = lane.astype(F32)
    logits = jnp.where(lane < N_EXPERTS, _dot(x2.astype(BF16), wr_ref[...]) + br_ref[...], -jnp.inf)
    vals, idxs = [], []
    cur = logits
    for _ in range(TOP_K):
        top = jnp.max(cur, axis=1, keepdims=True)
        idx = jnp.min(jnp.where(cur == top, lanef, float(LANES)), axis=1, keepdims=True)
        vals.append(top)
        idxs.append(idx)
        cur = jnp.where(lanef == idx, -jnp.inf, cur)
    es = [jnp.exp(v - vals[0]) for v in vals]
    inv = 1.0 / functools.reduce(lambda a, b: a + b, es)
    ti = jnp.zeros((tm, LANES), F32)
    tg = jnp.zeros((tm, LANES), F32)
    for k in range(TOP_K):
        ti = jnp.where(lane == k, idxs[k], ti)
        tg = jnp.where(lane == k, es[k] * inv, tg)
    ti_ref[...] = ti.astype(I32)
    tg_ref[...] = tg


def _xo_ln2_router(x1, o, wo, ln_g, ln_b, wr, br, tm, tail=None):
    M = x1.shape[0]
    nsteps = M // tm
    tail_rows = 0 if tail is None else tail[0].shape[0]
    assert tail_rows <= tm
    extra = 1 if tail_rows else 0
    last = nsteps - 1
    row_in = lambda w: pl.BlockSpec((tm, w), lambda i: (jnp.minimum(i, last), 0))
    tail_specs = [] if tail is None else [_resident(tail[0].shape), _resident(tail[1].shape)]
    return pl.pallas_call(
        functools.partial(_xo_body, tail_rows=tail_rows, nsteps=nsteps),
        grid=(nsteps + extra,),
        in_specs=[row_in(D_MODEL), row_in(D_MODEL), _resident(wo.shape), _resident(ln_g.shape),
                  _resident(ln_b.shape), _resident(wr.shape), _resident(br.shape)] + tail_specs,
        out_specs=[pl.BlockSpec((tm, D_MODEL), lambda i: (i, 0)),
                   pl.BlockSpec((tm * PACK_SLABS, LANES), lambda i: (i, 0)), row_in(LANES), row_in(LANES)],
        out_shape=[jax.ShapeDtypeStruct((M + extra * tm, D_MODEL), F32),
                   jax.ShapeDtypeStruct(((M + extra * tm) * PACK_SLABS, LANES), jnp.uint32),
                   jax.ShapeDtypeStruct((M, LANES), I32), jax.ShapeDtypeStruct((M, LANES), F32)],
        compiler_params=_cparams(1),
        name="xo_ln2_router",
    )(x1, o, wo, ln_g, ln_b, wr, br, *([] if tail is None else list(tail)))


MOE_SUB_ROWS = 256
MOE_FF_CHUNK = 256


def _ffn_body(te_ref, tr_ref, nu_ref, idc_ref, idn_ref, x_hbm, wg_ref, bg_ref, wu_ref, bu_ref, wd_ref, bd_ref,
              o_ref, stg, sem, xbb, wgb, wub, wdb, *, tm, sb, nch):
    t = pl.program_id(0)
    c = pl.program_id(1)
    n_used = nu_ref[0]
    ps = tm // nch
    cur = lax.rem(t, 2)
    tp = tm * PACK_SLABS

    def row_copy(src_row, sl, r):
        src_row = pl.multiple_of(src_row, PACK_SLABS)
        return pltpu.make_async_copy(x_hbm.at[pl.ds(src_row, PACK_SLABS), :],
                                     stg.at[pl.ds(sl * tp + r * PACK_SLABS, PACK_SLABS), :], sem.at[sl])

    def tile_wait(sl):
        pltpu.make_async_copy(x_hbm.at[pl.ds(0, tp), :], stg.at[pl.ds(sl * tp, tp), :], sem.at[sl]).wait()

    def unpack_tile(sl):
        for s in range(PACK_SLABS):
            lo, hi = _unpack_bf16_pairs(stg[pl.ds(sl * tp + s, tm, stride=PACK_SLABS), :])
            xbb[:, s * LANES:(s + 1) * LANES] = lo
            xbb[:, D_MODEL // 2 + s * LANES:D_MODEL // 2 + (s + 1) * LANES] = hi

    def ffn_rows(rs, wg, wu, wd):
        xb = xbb[rs, :]
        g = jnp.minimum(_dot(xb, wg) + bg_ref[0], SWIGLU_LIMIT)
        u = jnp.clip(_dot(xb, wu) + bu_ref[0], -SWIGLU_LIMIT, SWIGLU_LIMIT)
        hid = (u + 1.0) * (g * _sigmoid(SWIGLU_ALPHA * g))
        o_ref[rs, :] += _dot(hid.astype(BF16), wd)

    @pl.when(t >= n_used)
    def _():
        @pl.when(c == 0)
        def _():
            o_ref[...] = jnp.zeros((tm, D_MODEL), F32)

    @pl.when(t < n_used)
    def _():
        @pl.when((t == 0) & (c == 0))
        def _():
            def issue(r, carry):
                row_copy(idc_ref[0, 0, r], 0, r).start()
                return carry
            lax.fori_loop(0, tm, issue, 0, unroll=8)

        @pl.when(c == 0)
        def _():
            tile_wait(cur)
            unpack_tile(cur)

        def gather_next_group():
            for r in range(ps):
                row_copy(idn_ref[0, 0, c * ps + r], 1 - cur, c * ps + r).start()

        rows = tr_ref[t]

        @pl.when(rows > tm - sb)
        def _():
            @pl.when(c == 0)
            def _():
                o_ref[...] = jnp.broadcast_to(bd_ref[0], (tm, D_MODEL))
            gather_next_group()
            ffn_rows(slice(0, tm), wg_ref[0, 0].astype(BF16), wu_ref[0, 0].astype(BF16), wd_ref[0, 0].astype(BF16))

        @pl.when(rows <= tm - sb)
        def _():
            gather_next_group()
            wgb[...] = wg_ref[0, 0].astype(BF16)
            wub[...] = wu_ref[0, 0].astype(BF16)
            wdb[...] = wd_ref[0, 0].astype(BF16)
            for s in range(tm // sb):
                rs = slice(s * sb, (s + 1) * sb)

                @pl.when((s * sb >= rows) & (c == 0))
                def _():
                    o_ref[rs, :] = jnp.zeros((sb, D_MODEL), F32)

                @pl.when(s * sb < rows)
                def _():
                    @pl.when(c == 0)
                    def _():
                        o_ref[rs, :] = jnp.broadcast_to(bd_ref[0], (sb, D_MODEL))
                    ffn_rows(rs, wgb[...], wub[...], wdb[...])

        @pl.when((t == n_used - 1) & (c == nch - 1))
        def _():
            tile_wait(1 - cur)


def _moe_ffn(x, row_tok, tile_e, tile_rows, n_used, wg, bg, wu, bu, wd, bd, tm, sb):
    P = row_tok.shape[0]
    n_tiles = P // tm
    nch = D_FF // MOE_FF_CHUNK
    fc = MOE_FF_CHUNK
    assert tm % nch == 0 and (tm // nch) % 16 == 0 and nch >= 3

    def tile(t, nu):
        return jnp.minimum(t, jnp.maximum(nu[0] - 1, 0))

    def chunk(t, c, nu):
        return jnp.where(t < nu[0], c, nch - 1)

    idx_blk = (1, 1, tm)
    return pl.pallas_call(
        functools.partial(_ffn_body, tm=tm, sb=sb, nch=nch),
        grid_spec=pltpu.PrefetchScalarGridSpec(
            num_scalar_prefetch=3,
            grid=(n_tiles, nch),
            in_specs=[
                pl.BlockSpec(idx_blk, lambda t, c, te, tr, nu: (tile(t, nu), 0, 0), memory_space=pltpu.SMEM),
                pl.BlockSpec(idx_blk, lambda t, c, te, tr, nu: (tile(t + 1, nu), 0, 0), memory_space=pltpu.SMEM),
                pl.BlockSpec(memory_space=pl.ANY),
                pl.BlockSpec((1, 1, D_MODEL, fc), lambda t, c, te, tr, nu: (0, te[tile(t, nu)], 0, chunk(t, c, nu))),
                pl.BlockSpec((1, 1, fc), lambda t, c, te, tr, nu: (te[tile(t, nu)], 0, chunk(t, c, nu))),
                pl.BlockSpec((1, 1, D_MODEL, fc), lambda t, c, te, tr, nu: (0, te[tile(t, nu)], 0, chunk(t, c, nu))),
                pl.BlockSpec((1, 1, fc), lambda t, c, te, tr, nu: (te[tile(t, nu)], 0, chunk(t, c, nu))),
                pl.BlockSpec((1, 1, fc, D_MODEL), lambda t, c, te, tr, nu: (0, te[tile(t, nu)], chunk(t, c, nu), 0)),
                pl.BlockSpec((1, 1, D_MODEL), lambda t, c, te, tr, nu: (te[tile(t, nu)], 0, 0)),
            ],
            out_specs=pl.BlockSpec((tm, D_MODEL), lambda t, c, te, tr, nu: (t, 0)),
            scratch_shapes=[pltpu.VMEM((2 * tm * PACK_SLABS, LANES), jnp.uint32),
                            pltpu.SemaphoreType.DMA((2,)),
                            pltpu.VMEM((tm, D_MODEL), BF16),
                            pltpu.VMEM((D_MODEL, fc), BF16), pltpu.VMEM((D_MODEL, fc), BF16),
                            pltpu.VMEM((fc, D_MODEL), BF16)]),
        out_shape=jax.ShapeDtypeStruct((P, D_MODEL), F32),
        compiler_params=_cparams(2),
        name="moe_ffn",
    )(tile_e, tile_rows, n_used, row_tok.reshape(n_tiles, 1, tm), row_tok.reshape(n_tiles, 1, tm), x,
      wg, bg, wu, bu, wd, bd)


def _combine_body(posc_ref, posn_ref, x2_ref, tg_ref, ys_hbm, g_ref, b_ref, oa_ref, ob_ref, buf, sem, *,
                  tc, nblk, nblk_a):
    i = pl.program_id(0)
    slot = lax.rem(i, 2)

    def row_copy(src_row, sl, k, j):
        return pltpu.make_async_copy(ys_hbm.at[pl.ds(src_row, 1), :], buf.at[sl, k, pl.ds(j, 1), :], sem.at[sl])

    def slot_wait(sl):
        pltpu.make_async_copy(buf.at[sl], buf.at[sl], sem.at[sl]).wait()

    @pl.when(i == 0)
    def _():
        def issue(j, carry):
            for k in range(TOP_K):
                row_copy(posc_ref[0, 0, j * TOP_K + k], 0, k, j).start()
            return carry
        lax.fori_loop(0, tc, issue, 0, unroll=4)

    slot_wait(slot)
    for j in range(tc):
        for k in range(TOP_K):
            row_copy(posn_ref[0, 0, j * TOP_K + k], 1 - slot, k, j).start()
    y = None
    for k in range(TOP_K):
        term = buf[slot, k] * tg_ref[:, k:k + 1]
        y = term if y is None else y + term
    out = _layer_norm(DN_ALPHA * x2_ref[...] + y, g_ref[...], b_ref[...])

    @pl.when(i < nblk_a)
    def _():
        oa_ref[...] = out

    @pl.when(i >= nblk_a)
    def _():
        ob_ref[...] = out

    @pl.when(i == nblk - 1)
    def _():
        slot_wait(1 - slot)


def _combine_ln3(x2, tg, ys, pos, ln_g, ln_b, tc, split):
    M = pos.shape[0] // TOP_K
    nblk = M // tc
    nblk_a = split // tc
    assert split % tc == 0 and 0 < nblk_a < nblk
    pos3 = pos.reshape(nblk, 1, tc * TOP_K)
    return pl.pallas_call(
        functools.partial(_combine_body, tc=tc, nblk=nblk, nblk_a=nblk_a),
        grid=(nblk,),
        in_specs=[pl.BlockSpec((1, 1, tc * TOP_K), lambda i: (i, 0, 0), memory_space=pltpu.SMEM),
                  pl.BlockSpec((1, 1, tc * TOP_K), lambda i: (jnp.minimum(i + 1, nblk - 1), 0, 0),
                               memory_space=pltpu.SMEM),
                  pl.BlockSpec((tc, D_MODEL), lambda i: (i, 0)),
                  pl.BlockSpec((tc, LANES), lambda i: (i, 0)),
                  pl.BlockSpec(memory_space=pl.ANY),
                  _resident(ln_g.shape), _resident(ln_b.shape)],
        out_specs=[pl.BlockSpec((tc, D_MODEL), lambda i: (jnp.minimum(i, nblk_a - 1), 0)),
                   pl.BlockSpec((tc, D_MODEL), lambda i: (jnp.maximum(i - nblk_a, 0), 0))],
        out_shape=[jax.ShapeDtypeStruct((split, D_MODEL), F32), jax.ShapeDtypeStruct((M - split, D_MODEL), F32)],
        scratch_shapes=[pltpu.VMEM((2, TOP_K, tc, D_MODEL), F32), pltpu.SemaphoreType.DMA((2,))],
        compiler_params=_cparams(1),
        name="moe_combine",
    )(pos3, pos3, x2, tg, ys, ln_g, ln_b)


def _moe(x2, x2_packed, topi, gates, wg, bg, wu, bu, wd, bd, ln_g, ln_b, tm, tc, split):
    ntok = topi.shape[0]
    nk = ntok * TOP_K
    sb = min(tm, MOE_SUB_ROWS)
    n_tiles = -(-(nk + N_EXPERTS * (tm - 1)) // tm)
    e_flat = topi[:, :TOP_K].reshape(nk)
    onehot = (e_flat[:, None] == jnp.arange(N_EXPERTS, dtype=I32)[None, :]).astype(I32)
    csum = jnp.cumsum(onehot, axis=0)
    counts = csum[-1]
    rank = jnp.take_along_axis(csum, e_flat[:, None], axis=1)[:, 0] - 1
    tiles_e = (counts + tm - 1) // tm
    tile_end = jnp.cumsum(tiles_e)
    tile_start = tile_end - tiles_e
    dest = tile_start[e_flat] * tm + rank
    n_used = tile_end[-1:].astype(I32)
    t_ids = jnp.arange(n_tiles, dtype=I32)
    tile_e = jnp.minimum(jnp.searchsorted(tile_end, t_ids, side="right"), N_EXPERTS - 1).astype(I32)
    tile_rows = jnp.clip(counts[tile_e] - (t_ids - tile_start[tile_e]) * tm, 0, tm)
    tile_rows = jnp.where(t_ids < n_used[0], tile_rows, 0).astype(I32)
    P = n_tiles * tm
    first_row = (jnp.arange(nk, dtype=I32) // TOP_K) * PACK_SLABS
    row_src = jnp.zeros((P,), I32).at[dest].set(first_row, unique_indices=True)
    ys = _moe_ffn(x2_packed, row_src, tile_e, tile_rows, n_used, wg, bg, wu, bu, wd, bd, tm, sb)
    return _combine_ln3(x2, gates, ys, dest, ln_g, ln_b, tc, split)


def kernel(x_prompt, x_sample, mem_prompt, state_mlstm_C, state_mlstm_n, state_mlstm_m, cache_swa_k, cache_swa_v, cache_mem_k, cache_mem_v, w_in, b_igate, b_fgate, g_mnorm, w_mix_out, ln1_g, ln1_b, w_xq, w_xk, w_xv, w_xo, ln2_g, ln2_b, w_router, b_router, w_gate, b_gate, w_up, b_up, w_down, b_down, ln3_g, ln3_b):
    assert DEPTH == 1
    B, T, _ = x_prompt.shape
    DB, S, _ = x_sample.shape
    assert B == 1
    n_mem = mem_prompt.shape[1]
    wb = cache_swa_k.shape[2]
    row2 = lambda a: a[0].reshape(1, -1)

    wi = w_in[0]
    c0 = 2 * MIX_QK + 2 * MIX_V
    w_m = wi[:, :c0].astype(BF16)
    w_g = jnp.pad(wi[:, c0:c0 + 2 * M_HEADS], ((0, 0), (0, LANES - 2 * M_HEADS))).astype(BF16)
    w_a = wi[:, c0 + 2 * M_HEADS:].astype(BF16)
    gate_bias = jnp.pad(jnp.concatenate([b_igate[0], b_fgate[0]]), (0, LANES - 2 * M_HEADS)).reshape(1, LANES)
    w = dict(
        wmo=w_mix_out[0].astype(BF16), ln1_g=row2(ln1_g), ln1_b=row2(ln1_b), wxq=w_xq[0].astype(BF16),
        wxo=w_xo[0].astype(BF16), ln2_g=row2(ln2_g), ln2_b=row2(ln2_b),
        wr=jnp.pad(w_router[0], ((0, 0), (0, LANES - N_EXPERTS))).astype(BF16),
        br=jnp.pad(b_router[0], (0, LANES - N_EXPERTS)).reshape(1, LANES),
        wg=w_gate, bg=b_gate[0].reshape(N_EXPERTS, 1, D_FF), wu=w_up, bu=b_up[0].reshape(N_EXPERTS, 1, D_FF),
        wd=w_down, bd=b_down[0].reshape(N_EXPERTS, 1, D_MODEL), ln3_g=row2(ln3_g), ln3_b=row2(ln3_b))
    gm = g_mnorm[0].reshape(1, MIX_V)

    xp = x_prompt.reshape(T, D_MODEL)
    zm, zg = _mm_multi(xp, [w_m, w_g], [BF16, F32], 512, "proj_mlstm")
    dils = tuple(dil for _, dil in PATTERNS)
    wbp = min(wb, T)
    *za_views, kv_tail = _proj_attn(xp, w_a, dils, wbp)
    zeros_c = jnp.zeros((1, M_HEADS, M_DQK, M_DV), F32)
    zeros_n = jnp.zeros((1, M_HEADS, M_DQK), F32)
    zeros_m = jnp.zeros((1, M_HEADS, LANES), F32)
    hm, pC, pn, pm = _mlstm(zm, zg, gate_bias, gm, zeros_c, zeros_n, zeros_m, 1, T // 128)
    pats = [_dilated_pattern(zv, window, dil) for zv, (window, dil) in zip(za_views, PATTERNS)]
    mp = mem_prompt.reshape(n_mem, D_MODEL)
    mem_k = _mm(mp, w_xk[0].astype(BF16), F32, n_mem, 1024, "mem_k")
    mem_v = _mm(mp, w_xv[0].astype(BF16), F32, n_mem, 1024, "mem_v")
    mem_k5 = mem_k.reshape(1, B, n_mem, X_HEADS, X_DH)
    mem_v5 = mem_v.reshape(1, B, n_mem, X_HEADS, X_DH)
    x1_p, q_p = _mix_ln1_q(xp, hm, [p[0] for p in pats], [p[1] for p in pats], dils, w["wmo"], w["ln1_g"], w["ln1_b"],
                           w["wxq"], 256)
    o_p = _xattn(q_p.reshape(1, T, D_MODEL), mem_k.astype(BF16).reshape(1, n_mem, D_MODEL),
                 mem_v.astype(BF16).reshape(1, n_mem, D_MODEL), 512).reshape(T, D_MODEL)

    ns = DB * S
    xs_ = x_sample.reshape(ns, D_MODEL)
    zm_s, zg_s, za_s = _mm_multi(xs_, [w_m, w_g, w_a], [BF16, F32, F32], ns, "proj_sample")
    pad_rows = 128 - S
    zm_pad = jnp.pad(zm_s.reshape(DB, S, -1), ((0, 0), (0, pad_rows), (0, 0))).reshape(DB * 128, -1)
    lane = jnp.arange(LANES)
    neutral = jnp.where(lane < M_HEADS, NEG_BIG, jnp.where(lane < 2 * M_HEADS, -NEG_BIG, 0.0)).astype(F32)
    zg_pad = jnp.concatenate([zg_s.reshape(DB, S, LANES), jnp.broadcast_to(neutral, (DB, pad_rows, LANES))],
                             axis=1).reshape(DB * 128, LANES)
    m0 = jnp.broadcast_to(state_mlstm_m[0][:, :, None], (DB, M_HEADS, LANES))
    hm_s, sC, sn, sm = _mlstm(zm_pad, zg_pad, gate_bias, gm, state_mlstm_C[0], state_mlstm_n[0], m0, DB, 1)
    hm_s = hm_s.reshape(DB, 128, MIX_V)[:, :S].reshape(ns, MIX_V)
    za_pad = jnp.pad(za_s.reshape(DB, S, -1), ((0, 0), (0, SWA_ROWS - S), (0, 0))).astype(BF16)
    ha_s = _swa_sample(za_pad, cache_swa_k.reshape(DB, wb * A_HEADS, A_DH),
                       cache_swa_v.reshape(DB, wb * A_HEADS, A_DH), S)[:, :S].reshape(ns, MIX_A)
    x1_s, q_s = _mix_ln1_q(xs_, hm_s, [ha_s], [], (1,), w["wmo"], w["ln1_g"], w["ln1_b"], w["wxq"], ns)
    q_pad = jnp.pad(q_s.reshape(DB, S, D_MODEL), ((0, 0), (0, SWA_ROWS - S), (0, 0)))
    o_s = _xattn_cache(q_pad, cache_mem_k, cache_mem_v)[:, :S].reshape(ns, D_MODEL)
    x2_s, xk_s, topi_s, gates_s = _xo_ln2_router(x1_s, o_s, w["wxo"], w["ln2_g"], w["ln2_b"], w["wr"], w["br"], ns)

    x2_all, xk_all, topi_p, gates_p = _xo_ln2_router(x1_p, o_p, w["wxo"], w["ln2_g"], w["ln2_b"], w["wr"], w["br"],
                                                     256, tail=(x2_s, xk_s))
    yp, ys_ = _moe(x2_all, xk_all, jnp.concatenate([topi_p, topi_s]),
                   jnp.concatenate([gates_p, gates_s]), w["wg"], w["bg"], w["wu"], w["bu"], w["wd"], w["bd"],
                   w["ln3_g"], w["ln3_b"], 1024, 128, T)

    return (yp.reshape(B, T, D_MODEL), ys_.reshape(DB, S, D_MODEL),
            pC[None], pn[None], pm[:, :, 0][None],
            kv_tail[:, :MIX_A].reshape(1, B, wbp, A_HEADS, A_DH), kv_tail[:, MIX_A:].reshape(1, B, wbp, A_HEADS, A_DH),
            mem_k5, mem_v5,
            sC[None], sn[None], sm[:, :, 0][None],
            za_s[:, MIX_A:2 * MIX_A].reshape(1, DB, S, A_HEADS, A_DH),
            za_s[:, 2 * MIX_A:].reshape(1, DB, S, A_HEADS, A_DH))
```

```python
import functools

import jax
import jax.numpy as jnp
from jax import lax
from jax.experimental import pallas as pl
from jax.experimental.pallas import tpu as pltpu

F32, BF16, I32 = jnp.float32, jnp.bfloat16, jnp.int32

D_MODEL = 2048
DEPTH = 1
M_HEADS, M_DQK, M_DV = 4, 128, 256
HEAD_EPS = 1e-6
A_HEADS, A_DH = 8, 128
PATTERNS = ((128, 1), (512, 4), (2048, 16))
BAND_BLOCK = 128
X_HEADS = 4
X_DH = D_MODEL // X_HEADS
N_EXPERTS, TOP_K, D_FF = 32, 4, 2048
SWIGLU_LIMIT, SWIGLU_ALPHA = 7.0, 1.702
DN_ALPHA = (2 * DEPTH) ** 0.25
LN_EPS = 1e-5
MIX_V = M_HEADS * M_DV
MIX_A = A_HEADS * A_DH
MIX_QK = M_HEADS * M_DQK

LANES = 128
VMEM_LIMIT_BYTES = 58 * 1024 * 1024

NEG_BIG = -1e30


def _cparams(n_axes, vmem=VMEM_LIMIT_BYTES):
    return pltpu.CompilerParams(dimension_semantics=("arbitrary",) * n_axes, vmem_limit_bytes=vmem)


def _dot(a, b):
    return jnp.dot(a, b, preferred_element_type=F32)


def _dot_nt(a, b):
    return lax.dot_general(a, b, (((1,), (1,)), ((), ())), preferred_element_type=F32)


def _dot_tn(a, b):
    return lax.dot_general(a, b, (((0,), (0,)), ((), ())), preferred_element_type=F32)


def _log_sigmoid(x):
    return jnp.minimum(x, 0.0) - jnp.log(1.0 + jnp.exp(-jnp.abs(x)))


def _sigmoid(x):
    return 1.0 / (1.0 + jnp.exp(-x))


def _layer_norm(v, g, b):
    mu = jnp.mean(v, axis=-1, keepdims=True)
    d = v - mu
    var = jnp.mean(d * d, axis=-1, keepdims=True)
    return d * lax.rsqrt(var + LN_EPS) * g + b


def _resident(shape):
    nd = len(shape)
    return pl.BlockSpec(shape, lambda *_: (0,) * nd, pipeline_mode=pl.Buffered(1))


def _mm_body(x_ref, w_ref, o_ref):
    o_ref[...] = _dot(x_ref[...].astype(BF16), w_ref[...]).astype(o_ref.dtype)


def _mm(x, w, out_dtype, tm, tn, name):
    M, K = x.shape
    N = w.shape[1]
    assert M % tm == 0 and N % tn == 0
    return pl.pallas_call(
        _mm_body,
        grid=(M // tm, N // tn),
        in_specs=[pl.BlockSpec((tm, K), lambda i, j: (i, 0)), pl.BlockSpec((K, tn), lambda i, j: (0, j))],
        out_specs=pl.BlockSpec((tm, tn), lambda i, j: (i, j)),
        out_shape=jax.ShapeDtypeStruct((M, N), out_dtype),
        compiler_params=_cparams(2),
        name=name,
    )(x, w)


def _mm_multi_body(x_ref, *refs, tn):
    n = len(refs) // 2
    xb = x_ref[...].astype(BF16)
    for w_ref, o_ref in zip(refs[:n], refs[n:]):
        width = w_ref.shape[1]
        step = min(tn, width)
        for j in range(width // step):
            o_ref[:, j * step:(j + 1) * step] = _dot(xb, w_ref[:, j * step:(j + 1) * step]).astype(o_ref.dtype)


def _mm_multi(x, ws, out_dtypes, tm, name, tn=1024):
    M, K = x.shape
    assert M % tm == 0 and all(w.shape[1] % min(tn, w.shape[1]) == 0 for w in ws)
    return pl.pallas_call(
        functools.partial(_mm_multi_body, tn=tn),
        grid=(M // tm,),
        in_specs=[pl.BlockSpec((tm, K), lambda i: (i, 0))] + [_resident(w.shape) for w in ws],
        out_specs=[pl.BlockSpec((tm, w.shape[1]), lambda i: (i, 0)) for w in ws],
        out_shape=[jax.ShapeDtypeStruct((M, w.shape[1]), dt) for w, dt in zip(ws, out_dtypes)],
        compiler_params=_cparams(1),
        name=name,
    )(x, *ws)


def _mlstm_body(q_ref, k_ref, v_ref, zo_ref, zg_ref, gb_ref, gm_ref, c0_ref, n0_ref, m0_ref,
                hm_ref, c_ref, n_ref, m_ref, *, L, sub):
    @pl.when(pl.program_id(1) == 0)
    def _():
        c_ref[...] = c0_ref[...]
        n_ref[...] = n0_ref[...]
        m_ref[...] = m0_ref[...]

    for ci in range(sub):
        _mlstm_chunk(slice(ci * L, (ci + 1) * L), q_ref, k_ref, v_ref, zo_ref, zg_ref, gb_ref, gm_ref,
                     hm_ref, c_ref, n_ref, m_ref, L)


def _mlstm_chunk(rows, q_ref, k_ref, v_ref, zo_ref, zg_ref, gb_ref, gm_ref, hm_ref, c_ref, n_ref, m_ref, L):
    scale = M_DQK ** -0.5
    g = zg_ref[rows, :] + gb_ref[...]
    gt = g.T
    row = lax.broadcasted_iota(I32, (L, L), 0)
    col = lax.broadcasted_iota(I32, (L, L), 1)
    tri = row >= col
    for h in range(M_HEADS):
        i_col = g[:, h:h + 1]
        f_col = _log_sigmoid(g[:, M_HEADS + h:M_HEADS + h + 1])
        i_row = gt[h:h + 1, :]
        f_row = _log_sigmoid(gt[M_HEADS + h:M_HEADS + h + 1, :])
        q = q_ref[rows, h * M_DQK:(h + 1) * M_DQK]
        k = k_ref[rows, h * M_DQK:(h + 1) * M_DQK]
        v = v_ref[rows, h * M_DV:(h + 1) * M_DV]
        zo = zo_ref[rows, h * M_DV:(h + 1) * M_DV].astype(F32)
        C = c_ref[0, h]
        n = n_ref[0, h:h + 1, :]
        m = m_ref[0, h:h + 1, 0:1]
        b_col = jnp.sum(jnp.where(tri, f_row, 0.0), axis=1, keepdims=True)
        b_row = jnp.sum(jnp.where(row <= col, f_col, 0.0), axis=0, keepdims=True)
        logd = jnp.where(tri, b_col - b_row + i_row, -jnp.inf)
        inter = b_col + m
        mt = jnp.maximum(inter, jnp.max(logd, axis=1, keepdims=True))
        sd = _dot_nt(q, k) * scale * jnp.exp(logd - mt)
        sc = jnp.exp(inter - mt)
        num = _dot(sd.astype(BF16), v) + sc * _dot(q, C.astype(BF16))
        qn = _dot_nt(q, jnp.broadcast_to(n, (16, M_DQK)).astype(BF16))[:, 0:1]
        den = jnp.sum(sd, axis=1, keepdims=True) + sc * qn
        hh = num / jnp.maximum(jnp.abs(den), jnp.exp(-mt))
        hn = hh * lax.rsqrt(jnp.mean(hh * hh, axis=1, keepdims=True) + HEAD_EPS)
        out = hn * gm_ref[:, h * M_DV:(h + 1) * M_DV] * _sigmoid(zo)
        hm_ref[rows, h * M_DV:(h + 1) * M_DV] = out.astype(hm_ref.dtype)
        bl = b_col[L - 1:L, :]
        ml = mt[L - 1:L, :]
        w_col = jnp.exp(bl - b_col + i_col - ml)
        scl = jnp.exp(bl + m - ml)
        kw = k.astype(F32) * (w_col * scale)
        c_ref[0, h] = scl * C + _dot_tn(kw.astype(BF16), v)
        n_ref[0, h:h + 1, :] = scl * n + jnp.sum(kw, axis=0, keepdims=True)
        m_ref[0, h:h + 1, :] = jnp.broadcast_to(ml, (1, LANES))


def _mlstm(zm, zg, gate_bias, g_mnorm, c0, n0, m0, B, nc, sub=1, L=128):
    T = B * nc * L
    assert zm.shape == (T, 2 * MIX_QK + 2 * MIX_V) and nc % sub == 0
    nc, L_chunk, L = nc // sub, L, L * sub
    row_blk = lambda b, c: b * nc + c
    state_specs = [pl.BlockSpec((1, M_HEADS, M_DQK, M_DV), lambda b, c: (b, 0, 0, 0)),
                   pl.BlockSpec((1, M_HEADS, M_DQK), lambda b, c: (b, 0, 0)),
                   pl.BlockSpec((1, M_HEADS, LANES), lambda b, c: (b, 0, 0))]
    return pl.pallas_call(
        functools.partial(_mlstm_body, L=L_chunk, sub=sub),
        grid=(B, nc),
        in_specs=[pl.BlockSpec((L, MIX_QK), lambda b, c: (row_blk(b, c), 0)),
                  pl.BlockSpec((L, MIX_QK), lambda b, c: (row_blk(b, c), 1)),
                  pl.BlockSpec((L, MIX_V), lambda b, c: (row_blk(b, c), 1)),
                  pl.BlockSpec((L, MIX_V), lambda b, c: (row_blk(b, c), 2)),
                  pl.BlockSpec((L, LANES), lambda b, c: (row_blk(b, c), 0)),
                  pl.BlockSpec((1, LANES), lambda b, c: (0, 0)),
                  pl.BlockSpec((1, MIX_V), lambda b, c: (0, 0))] + state_specs,
        out_specs=[pl.BlockSpec((L, MIX_V), lambda b, c: (row_blk(b, c), 0))] + state_specs,
        out_shape=[jax.ShapeDtypeStruct((T, MIX_V), BF16),
                   jax.ShapeDtypeStruct((B, M_HEADS, M_DQK, M_DV), F32),
                   jax.ShapeDtypeStruct((B, M_HEADS, M_DQK), F32),
                   jax.ShapeDtypeStruct((B, M_HEADS, LANES), F32)],
        compiler_params=_cparams(2),
        name="mlstm",
    )(zm, zm, zm, zm, zg, gate_bias, g_mnorm, c0, n0, m0)


DIL_BLOCKS_PER_STEP = 2


def _dil_body(q_ref, kc_ref, kp_ref, vc_ref, vp_ref, o_ref, l_ref, *, span):
    scale = A_DH ** -0.5
    n = pl.program_id(1)
    row = lax.broadcasted_iota(I32, (BAND_BLOCK, BAND_BLOCK), 0)
    col = lax.broadcasted_iota(I32, (BAND_BLOCK, BAND_BLOCK), 1)
    first = jnp.where(n > 0, 0, 2 * BAND_BLOCK)
    mask_c = row >= col
    lane = lax.broadcasted_iota(I32, (BAND_BLOCK, LANES), 1)
    ones = jnp.ones((BAND_BLOCK, A_DH), BF16)
    for j in range(DIL_BLOCKS_PER_STEP):
        rows = slice(j * BAND_BLOCK, (j + 1) * BAND_BLOCK)
        prev = slice((j - 1) * BAND_BLOCK, j * BAND_BLOCK)
        mask_p = (BAND_BLOCK + row - col + (first if j == 0 else 0)) <= span
        lse_tile = jnp.zeros((BAND_BLOCK, LANES), F32)
        for h in range(A_HEADS):
            sl = slice(h * A_DH, (h + 1) * A_DH)
            q = q_ref[rows, sl]
            kp = kp_ref[:, sl] if j == 0 else kc_ref[prev, sl]
            vp = vp_ref[:, sl] if j == 0 else vc_ref[prev, sl]
            sp = jnp.where(mask_p, _dot_nt(q, kp) * scale, -jnp.inf)
            sc = jnp.where(mask_c, _dot_nt(q, kc_ref[rows, sl]) * scale, -jnp.inf)
            m = jnp.max(jnp.maximum(sp, sc), axis=1, keepdims=True)
            pp = jnp.exp(sp - m).astype(BF16)
            pc = jnp.exp(sc - m).astype(BF16)
            oa = (_dot(pp, jnp.concatenate([vp, ones], axis=1))
                  + _dot(pc, jnp.concatenate([vc_ref[rows, sl], ones], axis=1)))
            l = oa[:, A_DH:A_DH + 1]
            o_ref[rows, sl] = (oa[:, :A_DH] / l).astype(o_ref.dtype)
            lse_tile = jnp.where(lane == h, m + jnp.log(l), lse_tile)
        l_ref[rows, :] = lse_tile


PROJ_ATTN_ROWS = 512


def _proj_attn_body(x_ref, w_ref, *refs, dils):
    o_refs, tail_ref, acc = refs[:len(dils)], refs[len(dils)], refs[len(dils) + 1]
    tm = x_ref.shape[0]
    nslab = acc.shape[0]
    tn = nslab * LANES
    xb = x_ref[...].astype(BF16)
    for j in range(w_ref.shape[1] // tn):
        z = _dot(xb, w_ref[:, j * tn:(j + 1) * tn])
        if j * tn >= MIX_A:
            tail_ref[:, j * tn - MIX_A:(j + 1) * tn - MIX_A] = z
        for dil, o_ref in zip(dils, o_refs):
            if dil == 1:
                o_ref[:, j * tn:(j + 1) * tn] = z.astype(o_ref.dtype)
        for s in range(nslab):
            acc[s] = z[:, s * LANES:(s + 1) * LANES]
        for dil, o_ref in zip(dils, o_refs):
            if dil == 1:
                continue
            rows = tm // dil
            for r in range(dil):
                for s in range(nslab):
                    c0 = r * w_ref.shape[1] + j * tn + s * LANES
                    o_ref[:, c0:c0 + LANES] = acc[s, pl.ds(r, rows, stride=dil), :].astype(o_ref.dtype)


def _proj_attn(x, w, dils, tail_rows):
    T, K = x.shape
    N = w.shape[1]
    tm, tn = PROJ_ATTN_ROWS, 1024
    assert T % tm == 0 and N % tn == 0 and all(tm % (16 * d) == 0 for d in dils)
    assert tail_rows % tm == 0 and MIX_A % tn == 0 and N == 3 * MIX_A
    first_tail = (T - tail_rows) // tm
    return pl.pallas_call(
        functools.partial(_proj_attn_body, dils=dils),
        grid=(T // tm,),
        in_specs=[pl.BlockSpec((tm, K), lambda i: (i, 0)), _resident(w.shape)],
        out_specs=[pl.BlockSpec((tm // d, d * N), lambda i: (i, 0)) for d in dils]
        + [pl.BlockSpec((tm, 2 * MIX_A), lambda i: (jnp.maximum(i - first_tail, 0), 0))],
        out_shape=[jax.ShapeDtypeStruct((T // d, d * N), BF16) for d in dils]
        + [jax.ShapeDtypeStruct((tail_rows, 2 * MIX_A), F32)],
        scratch_shapes=[pltpu.VMEM((tn // LANES, tm, LANES), F32)],
        compiler_params=_cparams(1),
        name="proj_attn",
    )(x, w)


def _dilated_pattern(zv, window, dil):
    N = zv.shape[0]
    span = window // dil
    step_rows = DIL_BLOCKS_PER_STEP * BAND_BLOCK
    assert zv.shape[1] == dil * 3 * MIX_A and N % step_rows == 0 and BAND_BLOCK - 1 <= span
    nb = N // step_rows
    prev = lambda n: jnp.maximum(n * DIL_BLOCKS_PER_STEP - 1, 0)
    blk = (step_rows, MIX_A)
    pblk = (BAND_BLOCK, MIX_A)
    o, lse = pl.pallas_call(
        functools.partial(_dil_body, span=span),
        grid=(dil, nb),
        in_specs=[pl.BlockSpec(blk, lambda r, n: (n, 3 * r)),
                  pl.BlockSpec(blk, lambda r, n: (n, 3 * r + 1)),
                  pl.BlockSpec(pblk, lambda r, n: (prev(n), 3 * r + 1)),
                  pl.BlockSpec(blk, lambda r, n: (n, 3 * r + 2)),
                  pl.BlockSpec(pblk, lambda r, n: (prev(n), 3 * r + 2))],
        out_specs=[pl.BlockSpec(blk, lambda r, n: (n, r)),
                   pl.BlockSpec((step_rows, LANES), lambda r, n: (n, r))],
        out_shape=[jax.ShapeDtypeStruct((N, dil * MIX_A), BF16),
                   jax.ShapeDtypeStruct((N, dil * LANES), F32)],
        compiler_params=_cparams(2),
        name=f"dilated_d{dil}",
    )(zv, zv, zv, zv, zv)
    return o, lse


SWA_ROWS = 16


def _swa_body(q_ref, kn_ref, vn_ref, pos_ref, kr_ref, vr_ref, k_hbm, v_hbm, o_ref, oldk, oldv, sem, *,
              wb, nb, n_new, n_old, big_dil, dense):
    scale = A_DH ** -0.5
    b = pl.program_id(0)
    slot = lax.rem(b, 2)
    grp = n_new * A_HEADS

    def old_copies(bi, sl):
        cps = []
        for a in range(n_old):
            src = pl.ds(a * big_dil * A_HEADS, grp)
            dst = pl.ds(sl * n_old * grp + a * grp, grp)
            cps.append(pltpu.make_async_copy(k_hbm.at[bi, src, :], oldk.at[dst, :], sem.at[sl]))
            cps.append(pltpu.make_async_copy(v_hbm.at[bi, src, :], oldv.at[dst, :], sem.at[sl]))
        return cps

    @pl.when(b == 0)
    def _():
        for cp in old_copies(0, 0):
            cp.start()

    for cp in old_copies(b, slot):
        cp.wait()
    nxt = jnp.minimum(b + 1, nb - 1)
    for cp in old_copies(nxt, 1 - slot):
        cp.start()

    ncols = n_new * n_old + dense
    s_k = lax.broadcasted_iota(I32, (SWA_ROWS, ncols), 0)
    d_k = wb + s_k - pos_ref[...]
    s_n = lax.broadcasted_iota(I32, (SWA_ROWS, SWA_ROWS), 0)
    p_n = lax.broadcasted_iota(I32, (SWA_ROWS, SWA_ROWS), 1)
    d_n = s_n - p_n
    masks = []
    for window, dil in PATTERNS:
        assert dil & (dil - 1) == 0 and window % dil == 0
        mk = jnp.where(jnp.bitwise_and(d_k, dil - 1) == 0, d_k, window + 1) <= window
        mn = jnp.where(jnp.bitwise_and(d_n, dil - 1) == 0, jnp.where(d_n >= 0, d_n, window + 1), window + 1) <= window
        masks.append((mk, mn))
    for h in range(A_HEADS):
        sl = slice(h * A_DH, (h + 1) * A_DH)
        q = q_ref[0, :, sl]

        def head_rows(old, rec_ref):
            parts = [old[pl.ds(slot * n_old * grp + s * A_HEADS + h, n_old, stride=grp), :] for s in range(n_new)]
            parts.append(rec_ref[0, pl.ds(h, dense, stride=A_HEADS), :])
            return jnp.concatenate(parts, axis=0).astype(BF16)

        kb = head_rows(oldk, kr_ref)
        vb = head_rows(oldv, vr_ref)
        kn = kn_ref[0, :, sl]
        vn = vn_ref[0, :, sl]
        s_cache = _dot_nt(q, kb) * scale
        s_new = _dot_nt(q, kn) * scale
        ps, lses = [], []
        for mk, mn in masks:
            sk = jnp.where(mk, s_cache, -jnp.inf)
            sn = jnp.where(mn, s_new, -jnp.inf)
            m = jnp.maximum(jnp.max(sk, axis=1, keepdims=True), jnp.max(sn, axis=1, keepdims=True))
            pk = jnp.exp(sk - m)
            pn = jnp.exp(sn - m)
            l = jnp.sum(pk, axis=1, keepdims=True) + jnp.sum(pn, axis=1, keepdims=True)
            ps.append((pk, pn, l))
            lses.append(m + jnp.log(l))
        top = functools.reduce(jnp.maximum, lses)
        es = [jnp.exp(x - top) for x in lses]
        tot = functools.reduce(lambda a, b: a + b, es)
        pk_all = None
        pn_all = None
        for (pk, pn, l), e in zip(ps, es):
            coef = e / (tot * l)
            pk_all = pk * coef if pk_all is None else pk_all + pk * coef
            pn_all = pn * coef if pn_all is None else pn_all + pn * coef
        o = _dot(pk_all.astype(BF16), vb) + _dot(pn_all.astype(BF16), vn)
        o_ref[0, :, sl] = o.astype(o_ref.dtype)

    @pl.when(b == nb - 1)
    def _():
        for cp in old_copies(nxt, 1 - slot):
            cp.wait()


def _swa_sample(za_pad, kbuf, vbuf, n_new):
    B, wb = kbuf.shape[0], kbuf.shape[1] // A_HEADS
    big_dil = PATTERNS[-1][1]
    dense = max(w for w, _ in PATTERNS[:-1])
    assert all(w <= PATTERNS[-1][0] and d <= big_dil for w, d in PATTERNS[:-1])
    assert wb % dense == 0 and dense % big_dil == 0 and wb % big_dil == 0 and n_new <= big_dil
    n_old = (wb - dense) // big_dil
    assert n_old % 8 == 0
    grp = n_new * A_HEADS
    old_pos = (jnp.arange(n_old, dtype=I32)[None, :] * big_dil + jnp.arange(n_new, dtype=I32)[:, None]).reshape(-1)
    pos = jnp.concatenate([old_pos, wb - dense + jnp.arange(dense, dtype=I32)]).reshape(1, -1)
    new_blk = (1, SWA_ROWS, MIX_A)
    rec_blk = (1, dense * A_HEADS, A_DH)
    last = wb // dense - 1
    return pl.pallas_call(
        functools.partial(_swa_body, wb=wb, nb=B, n_new=n_new, n_old=n_old, big_dil=big_dil, dense=dense),
        grid=(B,),
        in_specs=[pl.BlockSpec(new_blk, lambda b: (b, 0, 0)),
                  pl.BlockSpec(new_blk, lambda b: (b, 0, 1)),
                  pl.BlockSpec(new_blk, lambda b: (b, 0, 2)),
                  _resident(pos.shape),
                  pl.BlockSpec(rec_blk, lambda b: (b, last, 0)),
                  pl.BlockSpec(rec_blk, lambda b: (b, last, 0)),
                  pl.BlockSpec(memory_space=pl.ANY), pl.BlockSpec(memory_space=pl.ANY)],
        out_specs=pl.BlockSpec(new_blk, lambda b: (b, 0, 0)),
        out_shape=jax.ShapeDtypeStruct((B, SWA_ROWS, MIX_A), BF16),
        scratch_shapes=[pltpu.VMEM((2 * n_old * grp, A_DH), F32), pltpu.VMEM((2 * n_old * grp, A_DH), F32),
                        pltpu.SemaphoreType.DMA((2,))],
        compiler_params=_cparams(1),
        name="swa_sample",
    )(za_pad, za_pad, za_pad, pos, kbuf, vbuf, kbuf, vbuf)


def _mix_body(*refs, dils):
    n_pat = len(dils)
    x_ref, hm_ref = refs[0], refs[1]
    n_lse = n_pat if n_pat > 1 else 0
    o_refs = refs[2:2 + n_pat]
    l_refs = refs[2 + n_pat:2 + n_pat + n_lse]
    wmo_ref, g_ref, b_ref, wq_ref, x1_ref, q_ref = refs[2 + n_pat + n_lse:8 + n_pat + n_lse]
    if n_pat == 1:
        ha = o_refs[0][...]
    else:
        o_nat, l_nat = refs[8 + n_pat + n_lse:]
        tm = x_ref.shape[0]
        for p, dil in enumerate(dils):
            if dil == 1:
                continue
            rows = tm // dil
            for r in range(dil):
                dst = pl.ds(r, rows, stride=dil)
                l_nat[p, dst, :] = l_refs[p][:, r * LANES:(r + 1) * LANES]
                for h in range(A_HEADS):
                    c0 = r * MIX_A + h * A_DH
                    o_nat[p, h, dst, :] = o_refs[p][:, c0:c0 + A_DH].astype(F32)

        def lse_of(p, h):
            return l_refs[p][:, h:h + 1] if dils[p] == 1 else l_nat[p, :, h:h + 1]

        def out_of(p, h):
            return o_refs[p][:, h * A_DH:(h + 1) * A_DH].astype(F32) if dils[p] == 1 else o_nat[p, h]

        cols = []
        for h in range(A_HEADS):
            ls = [lse_of(p, h) for p in range(n_pat)]
            top = functools.reduce(jnp.maximum, ls)
            es = [jnp.exp(x - top) for x in ls]
            inv = 1.0 / functools.reduce(lambda a, b: a + b, es)
            acc = None
            for p, e in enumerate(es):
                term = out_of(p, h) * (e * inv)
                acc = term if acc is None else acc + term
            cols.append(acc.astype(BF16))
        ha = jnp.concatenate(cols, axis=1)
    mix = _dot(hm_ref[...], wmo_ref[0:MIX_V, :]) + _dot(ha, wmo_ref[MIX_V:MIX_V + MIX_A, :])
    x1 = _layer_norm(DN_ALPHA * x_ref[...] + mix, g_ref[...], b_ref[...])
    x1_ref[...] = x1
    q_ref[...] = _dot(x1.astype(BF16), wq_ref[...]).astype(q_ref.dtype)


def _mix_ln1_q(x, hm, outs, lses, dils, wmo, ln_g, ln_b, wq, tm):
    M = x.shape[0]
    n_pat = len(outs)
    assert len(lses) == (n_pat if n_pat > 1 else 0) and len(dils) == n_pat
    assert all(tm % (16 * d) == 0 for d in dils)
    row = lambda w: pl.BlockSpec((tm, w), lambda i: (i, 0))
    view = lambda w, d: pl.BlockSpec((tm // d, d * w), lambda i: (i, 0))
    scratch = []
    if n_pat > 1:
        scratch = [pltpu.VMEM((n_pat, A_HEADS, tm, A_DH), F32), pltpu.VMEM((n_pat, tm, LANES), F32)]
    return pl.pallas_call(
        functools.partial(_mix_body, dils=tuple(dils)),
        grid=(M // tm,),
        in_specs=[row(D_MODEL), row(MIX_V)] + [view(MIX_A, d) for d in dils]
        + [view(LANES, d) for d in dils[:len(lses)]]
        + [_resident(wmo.shape), _resident(ln_g.shape), _resident(ln_b.shape), _resident(wq.shape)],
        out_specs=[row(D_MODEL), row(D_MODEL)],
        out_shape=[jax.ShapeDtypeStruct((M, D_MODEL), F32), jax.ShapeDtypeStruct((M, D_MODEL), BF16)],
        scratch_shapes=scratch,
        compiler_params=_cparams(1),
        name="mix_ln1_q",
    )(x, hm, *outs, *lses, wmo, ln_g, ln_b, wq)


def _xattn_body(q_ref, k_ref, v_ref, o_ref):
    for h in range(X_HEADS):
        sl = slice(h * X_DH, (h + 1) * X_DH)
        o_ref[0, :, sl] = _xattn_head(q_ref[0, :, sl], k_ref[0, :, sl], v_ref[0, :, sl]).astype(o_ref.dtype)


def _xattn_head(q, k, v):
    s = _dot_nt(q, k.astype(BF16)) * (X_DH ** -0.5)
    p = jnp.exp(s - jnp.max(s, axis=1, keepdims=True))
    p = p / jnp.sum(p, axis=1, keepdims=True)
    return _dot(p.astype(BF16), v.astype(BF16))


def _xattn_cache_body(q_ref, k_hbm, v_hbm, o_ref, kbuf, vbuf, sem, *, nb):
    b = pl.program_id(0)
    slot = lax.rem(b, 2)

    def head_copies(bi, sl):
        cps = []
        for h in range(X_HEADS):
            cps.append(pltpu.make_async_copy(k_hbm.at[0, bi, :, h, :], kbuf.at[sl, h], sem.at[sl]))
            cps.append(pltpu.make_async_copy(v_hbm.at[0, bi, :, h, :], vbuf.at[sl, h], sem.at[sl]))
        return cps

    @pl.when(b == 0)
    def _():
        for cp in head_copies(0, 0):
            cp.start()

    for cp in head_copies(b, slot):
        cp.wait()
    nxt = jnp.minimum(b + 1, nb - 1)
    for cp in head_copies(nxt, 1 - slot):
        cp.start()
    for h in range(X_HEADS):
        sl = slice(h * X_DH, (h + 1) * X_DH)
        o_ref[0, :, sl] = _xattn_head(q_ref[0, :, sl], kbuf[slot, h], vbuf[slot, h]).astype(o_ref.dtype)

    @pl.when(b == nb - 1)
    def _():
        for cp in head_copies(nxt, 1 - slot):
            cp.wait()


def _xattn_cache(q, ck, cv):
    B, Tq, _ = q.shape
    nm = ck.shape[2]
    return pl.pallas_call(
        functools.partial(_xattn_cache_body, nb=B),
        grid=(B,),
        in_specs=[pl.BlockSpec((1, Tq, D_MODEL), lambda b: (b, 0, 0)),
                  pl.BlockSpec(memory_space=pl.ANY), pl.BlockSpec(memory_space=pl.ANY)],
        out_specs=pl.BlockSpec((1, Tq, D_MODEL), lambda b: (b, 0, 0)),
        out_shape=jax.ShapeDtypeStruct((B, Tq, D_MODEL), BF16),
        scratch_shapes=[pltpu.VMEM((2, X_HEADS, nm, X_DH), F32), pltpu.VMEM((2, X_HEADS, nm, X_DH), F32),
                        pltpu.SemaphoreType.DMA((2,))],
        compiler_params=_cparams(1),
        name="xattn_cache",
    )(q, ck, cv)


def _xattn(q, mk, mv, tq):
    B, Tq, _ = q.shape
    nm = mk.shape[1]
    mem_blk = (1, nm, D_MODEL)
    return pl.pallas_call(
        _xattn_body,
        grid=(B, Tq // tq),
        in_specs=[pl.BlockSpec((1, tq, D_MODEL), lambda b, i: (b, i, 0)),
                  pl.BlockSpec(mem_blk, lambda b, i: (b, 0, 0)),
                  pl.BlockSpec(mem_blk, lambda b, i: (b, 0, 0))],
        out_specs=pl.BlockSpec((1, tq, D_MODEL), lambda b, i: (b, i, 0)),
        out_shape=jax.ShapeDtypeStruct((B, Tq, D_MODEL), BF16),
        compiler_params=_cparams(2),
        name="xattn",
    )(q, mk, mv)


PACK_SLABS = D_MODEL // (2 * LANES)


def _pack_bf16_pairs(lo, hi):
    lo_bits = pltpu.bitcast(lo.astype(BF16).astype(F32), jnp.uint32)
    hi_bits = pltpu.bitcast(hi.astype(BF16).astype(F32), jnp.uint32)
    return jnp.bitwise_or(jnp.bitwise_and(hi_bits, jnp.uint32(0xFFFF0000)), lax.shift_right_logical(lo_bits, jnp.uint32(16)))


def _unpack_bf16_pairs(w):
    lo = pltpu.bitcast(lax.shift_left(w, jnp.uint32(16)), F32).astype(BF16)
    hi = pltpu.bitcast(jnp.bitwise_and(w, jnp.uint32(0xFFFF0000)), F32).astype(BF16)
    return lo, hi


def _xo_body(x1_ref, o_ref, wo_ref, g_ref, b_ref, wr_ref, br_ref, *rest, tail_rows, nsteps):
    if not tail_rows:
        _xo_rows(x1_ref, o_ref, wo_ref, g_ref, b_ref, wr_ref, br_ref, *rest)
        return
    tx2_ref, txp_ref, x2_ref, xp_ref, ti_ref, tg_ref = rest
    i = pl.program_id(0)

    @pl.when(i < nsteps)
    def _():
        _xo_rows(x1_ref, o_ref, wo_ref, g_ref, b_ref, wr_ref, br_ref, x2_ref, xp_ref, ti_ref, tg_ref)

    @pl.when(i == nsteps)
    def _():
        tm = x2_ref.shape[0]
        x2_ref[0:tail_rows, :] = tx2_ref[...]
        x2_ref[tail_rows:, :] = jnp.zeros((tm - tail_rows, D_MODEL), F32)
        xp_ref[0:tail_rows * PACK_SLABS, :] = txp_ref[...]
        xp_ref[tail_rows * PACK_SLABS:, :] = jnp.zeros(((tm - tail_rows) * PACK_SLABS, LANES), jnp.uint32)


def _xo_rows(x1_ref, o_ref, wo_ref, g_ref, b_ref, wr_ref, br_ref, x2_ref, xp_ref, ti_ref, tg_ref):
    y = _dot(o_ref[...], wo_ref[...])
    x2 = _layer_norm(DN_ALPHA * x1_ref[...] + y, g_ref[...], b_ref[...])
    x2_ref[...] = x2
    tm = x2.shape[0]
    for s in range(PACK_SLABS):
        lo = x2[:, s * LANES:(s + 1) * LANES]
        hi = x2[:, D_MODEL // 2 + s * LANES:D_MODEL // 2 + (s + 1) * LANES]
        xp_ref[pl.ds(s, tm, stride=PACK_SLABS), :] = _pack_bf16_pairs(lo, hi)
    lane = lax.broadcasted_iota(I32, (tm, LANES), 1)
    lanef = lane.astype(F32)
    logits = jnp.where(lane < N_EXPERTS, _dot(x2.astype(BF16), wr_ref[...]) + br_ref[...], -jnp.inf)
    vals, idxs = [], []
    cur = logits
    for _ in range(TOP_K):
        top = jnp.max(cur, axis=1, keepdims=True)
        idx = jnp.min(jnp.where(cur == top, lanef, float(LANES)), axis=1, keepdims=True)
        vals.append(top)
        idxs.append(idx)
        cur = jnp.where(lanef == idx, -jnp.inf, cur)
    es = [jnp.exp(v - vals[0]) for v in vals]
    inv = 1.0 / functools.reduce(lambda a, b: a + b, es)
    ti = jnp.zeros((tm, LANES), F32)
    tg = jnp.zeros((tm, LANES), F32)
    for k in range(TOP_K):
        ti = jnp.where(lane == k, idxs[k], ti)
        tg = jnp.where(lane == k, es[k] * inv, tg)
    ti_ref[...] = ti.astype(I32)
    tg_ref[...] = tg


def _xo_ln2_router(x1, o, wo, ln_g, ln_b, wr, br, tm, tail=None):
    M = x1.shape[0]
    nsteps = M // tm
    tail_rows = 0 if tail is None else tail[0].shape[0]
    assert tail_rows <= tm
    extra = 1 if tail_rows else 0
    last = nsteps - 1
    row_in = lambda w: pl.BlockSpec((tm, w), lambda i: (jnp.minimum(i, last), 0))
    tail_specs = [] if tail is None else [_resident(tail[0].shape), _resident(tail[1].shape)]
    return pl.pallas_call(
        functools.partial(_xo_body, tail_rows=tail_rows, nsteps=nsteps),
        grid=(nsteps + extra,),
        in_specs=[row_in(D_MODEL), row_in(D_MODEL), _resident(wo.shape), _resident(ln_g.shape),
                  _resident(ln_b.shape), _resident(wr.shape), _resident(br.shape)] + tail_specs,
        out_specs=[pl.BlockSpec((tm, D_MODEL), lambda i: (i, 0)),
                   pl.BlockSpec((tm * PACK_SLABS, LANES), lambda i: (i, 0)), row_in(LANES), row_in(LANES)],
        out_shape=[jax.ShapeDtypeStruct((M + extra * tm, D_MODEL), F32),
                   jax.ShapeDtypeStruct(((M + extra * tm) * PACK_SLABS, LANES), jnp.uint32),
                   jax.ShapeDtypeStruct((M, LANES), I32), jax.ShapeDtypeStruct((M, LANES), F32)],
        compiler_params=_cparams(1),
        name="xo_ln2_router",
    )(x1, o, wo, ln_g, ln_b, wr, br, *([] if tail is None else list(tail)))


MOE_SUB_ROWS = 256
MOE_FF_CHUNK = 256


def _ffn_body(te_ref, tr_ref, nu_ref, idc_ref, idn_ref, x_hbm, wg_ref, bg_ref, wu_ref, bu_ref, wd_ref, bd_ref,
              o_ref, stg, sem, xbb, wgb, wub, wdb, *, tm, sb, nch):
    t = pl.program_id(0)
    c = pl.program_id(1)
    n_used = nu_ref[0]
    ps = tm // nch
    gp = ps * PACK_SLABS
    cur = lax.rem(t, 2)
    tp = tm * PACK_SLABS

    def row_copy(src_row, sl, g, r):
        src_row = pl.multiple_of(src_row, PACK_SLABS)
        dst = pl.ds(sl * tp + g * gp + r * PACK_SLABS, PACK_SLABS)
        return pltpu.make_async_copy(x_hbm.at[pl.ds(src_row, PACK_SLABS), :], stg.at[dst, :], sem.at[sl * nch + g])

    def group_wait(sl, g):
        pltpu.make_async_copy(x_hbm.at[pl.ds(0, gp), :], stg.at[pl.ds(sl * tp + g * gp, gp), :],
                              sem.at[sl * nch + g]).wait()

    def unpack_group(sl, g):
        rows_g = slice(g * ps, (g + 1) * ps)
        for s in range(PACK_SLABS):
            lo, hi = _unpack_bf16_pairs(stg[pl.ds(sl * tp + g * gp + s, ps, stride=PACK_SLABS), :])
            xbb[rows_g, s * LANES:(s + 1) * LANES] = lo
            xbb[rows_g, D_MODEL // 2 + s * LANES:D_MODEL // 2 + (s + 1) * LANES] = hi

    def ffn_rows(rs, wg, wu, wd):
        xb = xbb[rs, :]
        g = jnp.minimum(_dot(xb, wg) + bg_ref[0], SWIGLU_LIMIT)
        u = jnp.clip(_dot(xb, wu) + bu_ref[0], -SWIGLU_LIMIT, SWIGLU_LIMIT)
        hid = (u + 1.0) * (g * _sigmoid(SWIGLU_ALPHA * g))
        o_ref[rs, :] += _dot(hid.astype(BF16), wd)

    @pl.when(t >= n_used)
    def _():
        @pl.when(c == 0)
        def _():
            o_ref[...] = jnp.zeros((tm, D_MODEL), F32)

    @pl.when(t < n_used)
    def _():
        rows = tr_ref[t]
        rows_next = jnp.where(t + 1 < n_used, tr_ref[jnp.minimum(t + 1, n_used - 1)], 0)

        @pl.when((t == 0) & (c == 0))
        def _():
            xbb[...] = jnp.zeros((tm, D_MODEL), BF16)
            for g in range(nch):
                @pl.when(g * ps < rows)
                def _():
                    def issue(r, carry):
                        row_copy(idc_ref[0, 0, g * ps + r], 0, g, r).start()
                        return carry
                    lax.fori_loop(0, ps, issue, 0, unroll=8)

        @pl.when(c == 0)
        def _():
            for g in range(nch):
                @pl.when(g * ps < rows)
                def _():
                    group_wait(cur, g)
                    unpack_group(cur, g)

        @pl.when(c * ps < rows_next)
        def _():
            for r in range(ps):
                row_copy(idn_ref[0, 0, c * ps + r], 1 - cur, c, r).start()

        @pl.when(rows > tm - sb)
        def _():
            @pl.when(c == 0)
            def _():
                o_ref[...] = jnp.broadcast_to(bd_ref[0], (tm, D_MODEL))
            ffn_rows(slice(0, tm), wg_ref[0, 0].astype(BF16), wu_ref[0, 0].astype(BF16), wd_ref[0, 0].astype(BF16))

        @pl.when(rows <= tm - sb)
        def _():
            wgb[...] = wg_ref[0, 0].astype(BF16)
            wub[...] = wu_ref[0, 0].astype(BF16)
            wdb[...] = wd_ref[0, 0].astype(BF16)
            for s in range(tm // sb):
                rs = slice(s * sb, (s + 1) * sb)

                @pl.when((s * sb >= rows) & (c == 0))
                def _():
                    o_ref[rs, :] = jnp.zeros((sb, D_MODEL), F32)

                @pl.when(s * sb < rows)
                def _():
                    @pl.when(c == 0)
                    def _():
                        o_ref[rs, :] = jnp.broadcast_to(bd_ref[0], (sb, D_MODEL))
                    ffn_rows(rs, wgb[...], wub[...], wdb[...])


def _moe_ffn(x, row_tok, tile_e, tile_rows, n_used, wg, bg, wu, bu, wd, bd, tm, sb):
    P = row_tok.shape[0]
    n_tiles = P // tm
    nch = D_FF // MOE_FF_CHUNK
    fc = MOE_FF_CHUNK
    assert tm % nch == 0 and (tm // nch) % 16 == 0 and nch >= 3

    def tile(t, nu):
        return jnp.minimum(t, jnp.maximum(nu[0] - 1, 0))

    def chunk(t, c, nu):
        return jnp.where(t < nu[0], c, nch - 1)

    idx_blk = (1, 1, tm)
    return pl.pallas_call(
        functools.partial(_ffn_body, tm=tm, sb=sb, nch=nch),
        grid_spec=pltpu.PrefetchScalarGridSpec(
            num_scalar_prefetch=3,
            grid=(n_tiles, nch),
            in_specs=[
                pl.BlockSpec(idx_blk, lambda t, c, te, tr, nu: (tile(t, nu), 0, 0), memory_space=pltpu.SMEM),
                pl.BlockSpec(idx_blk, lambda t, c, te, tr, nu: (tile(t + 1, nu), 0, 0), memory_space=pltpu.SMEM),
                pl.BlockSpec(memory_space=pl.ANY),
                pl.BlockSpec((1, 1, D_MODEL, fc), lambda t, c, te, tr, nu: (0, te[tile(t, nu)], 0, chunk(t, c, nu))),
                pl.BlockSpec((1, 1, fc), lambda t, c, te, tr, nu: (te[tile(t, nu)], 0, chunk(t, c, nu))),
                pl.BlockSpec((1, 1, D_MODEL, fc), lambda t, c, te, tr, nu: (0, te[tile(t, nu)], 0, chunk(t, c, nu))),
                pl.BlockSpec((1, 1, fc), lambda t, c, te, tr, nu: (te[tile(t, nu)], 0, chunk(t, c, nu))),
                pl.BlockSpec((1, 1, fc, D_MODEL), lambda t, c, te, tr, nu: (0, te[tile(t, nu)], chunk(t, c, nu), 0)),
                pl.BlockSpec((1, 1, D_MODEL), lambda t, c, te, tr, nu: (te[tile(t, nu)], 0, 0)),
            ],
            out_specs=pl.BlockSpec((tm, D_MODEL), lambda t, c, te, tr, nu: (t, 0)),
            scratch_shapes=[pltpu.VMEM((2 * tm * PACK_SLABS, LANES), jnp.uint32),
                            pltpu.SemaphoreType.DMA((2 * nch,)),
                            pltpu.VMEM((tm, D_MODEL), BF16),
                            pltpu.VMEM((D_MODEL, fc), BF16), pltpu.VMEM((D_MODEL, fc), BF16),
                            pltpu.VMEM((fc, D_MODEL), BF16)]),
        out_shape=jax.ShapeDtypeStruct((P, D_MODEL), F32),
        compiler_params=_cparams(2),
        name="moe_ffn",
    )(tile_e, tile_rows, n_used, row_tok.reshape(n_tiles, 1, tm), row_tok.reshape(n_tiles, 1, tm), x,
      wg, bg, wu, bu, wd, bd)


def _combine_body(posc_ref, posn_ref, x2_ref, tg_ref, ys_hbm, g_ref, b_ref, oa_ref, ob_ref, buf, sem, *,
                  tc, nblk, nblk_a):
    i = pl.program_id(0)
    slot = lax.rem(i, 2)

    def row_copy(src_row, sl, k, j):
        return pltpu.make_async_copy(ys_hbm.at[pl.ds(src_row, 1), :], buf.at[sl, k, pl.ds(j, 1), :], sem.at[sl])

    def slot_wait(sl):
        pltpu.make_async_copy(buf.at[sl], buf.at[sl], sem.at[sl]).wait()

    @pl.when(i == 0)
    def _():
        def issue(j, carry):
            for k in range(TOP_K):
                row_copy(posc_ref[0, 0, j * TOP_K + k], 0, k, j).start()
            return carry
        lax.fori_loop(0, tc, issue, 0, unroll=4)

    slot_wait(slot)
    for j in range(tc):
        for k in range(TOP_K):
            row_copy(posn_ref[0, 0, j * TOP_K + k], 1 - slot, k, j).start()
    y = None
    for k in range(TOP_K):
        term = buf[slot, k] * tg_ref[:, k:k + 1]
        y = term if y is None else y + term
    out = _layer_norm(DN_ALPHA * x2_ref[...] + y, g_ref[...], b_ref[...])

    @pl.when(i < nblk_a)
    def _():
        oa_ref[...] = out

    @pl.when(i >= nblk_a)
    def _():
        ob_ref[...] = out

    @pl.when(i == nblk - 1)
    def _():
        slot_wait(1 - slot)


def _combine_ln3(x2, tg, ys, pos, ln_g, ln_b, tc, split):
    M = pos.shape[0] // TOP_K
    nblk = M // tc
    nblk_a = split // tc
    assert split % tc == 0 and 0 < nblk_a < nblk
    pos3 = pos.reshape(nblk, 1, tc * TOP_K)
    return pl.pallas_call(
        functools.partial(_combine_body, tc=tc, nblk=nblk, nblk_a=nblk_a),
        grid=(nblk,),
        in_specs=[pl.BlockSpec((1, 1, tc * TOP_K), lambda i: (i, 0, 0), memory_space=pltpu.SMEM),
                  pl.BlockSpec((1, 1, tc * TOP_K), lambda i: (jnp.minimum(i + 1, nblk - 1), 0, 0),
                               memory_space=pltpu.SMEM),
                  pl.BlockSpec((tc, D_MODEL), lambda i: (i, 0)),
                  pl.BlockSpec((tc, LANES), lambda i: (i, 0)),
                  pl.BlockSpec(memory_space=pl.ANY),
                  _resident(ln_g.shape), _resident(ln_b.shape)],
        out_specs=[pl.BlockSpec((tc, D_MODEL), lambda i: (jnp.minimum(i, nblk_a - 1), 0)),
                   pl.BlockSpec((tc, D_MODEL), lambda i: (jnp.maximum(i - nblk_a, 0), 0))],
        out_shape=[jax.ShapeDtypeStruct((split, D_MODEL), F32), jax.ShapeDtypeStruct((M - split, D_MODEL), F32)],
        scratch_shapes=[pltpu.VMEM((2, TOP_K, tc, D_MODEL), F32), pltpu.SemaphoreType.DMA((2,))],
        compiler_params=_cparams(1),
        name="moe_combine",
    )(pos3, pos3, x2, tg, ys, ln_g, ln_b)


def _moe(x2, x2_packed, topi, gates, wg, bg, wu, bu, wd, bd, ln_g, ln_b, tm, tc, split):
    ntok = topi.shape[0]
    nk = ntok * TOP_K
    sb = min(tm, MOE_SUB_ROWS)
    n_tiles = -(-(nk + N_EXPERTS * (tm - 1)) // tm)
    e_flat = topi[:, :TOP_K].reshape(nk)
    onehot = (e_flat[:, None] == jnp.arange(N_EXPERTS, dtype=I32)[None, :]).astype(I32)
    csum = jnp.cumsum(onehot, axis=0)
    counts = csum[-1]
    rank = jnp.take_along_axis(csum, e_flat[:, None], axis=1)[:, 0] - 1
    tiles_e = (counts + tm - 1) // tm
    tile_end = jnp.cumsum(tiles_e)
    tile_start = tile_end - tiles_e
    dest = tile_start[e_flat] * tm + rank
    n_used = tile_end[-1:].astype(I32)
    t_ids = jnp.arange(n_tiles, dtype=I32)
    tile_e = jnp.minimum(jnp.searchsorted(tile_end, t_ids, side="right"), N_EXPERTS - 1).astype(I32)
    tile_rows = jnp.clip(counts[tile_e] - (t_ids - tile_start[tile_e]) * tm, 0, tm)
    tile_rows = jnp.where(t_ids < n_used[0], tile_rows, 0).astype(I32)
    P = n_tiles * tm
    first_row = (jnp.arange(nk, dtype=I32) // TOP_K) * PACK_SLABS
    row_src = jnp.zeros((P,), I32).at[dest].set(first_row, unique_indices=True)
    ys = _moe_ffn(x2_packed, row_src, tile_e, tile_rows, n_used, wg, bg, wu, bu, wd, bd, tm, sb)
    return _combine_ln3(x2, gates, ys, dest, ln_g, ln_b, tc, split)


def kernel(x_prompt, x_sample, mem_prompt, state_mlstm_C, state_mlstm_n, state_mlstm_m, cache_swa_k, cache_swa_v, cache_mem_k, cache_mem_v, w_in, b_igate, b_fgate, g_mnorm, w_mix_out, ln1_g, ln1_b, w_xq, w_xk, w_xv, w_xo, ln2_g, ln2_b, w_router, b_router, w_gate, b_gate, w_up, b_up, w_down, b_down, ln3_g, ln3_b):
    assert DEPTH == 1
    B, T, _ = x_prompt.shape
    DB, S, _ = x_sample.shape
    assert B == 1
    n_mem = mem_prompt.shape[1]
    wb = cache_swa_k.shape[2]
    row2 = lambda a: a[0].reshape(1, -1)

    wi = w_in[0]
    c0 = 2 * MIX_QK + 2 * MIX_V
    w_m = wi[:, :c0].astype(BF16)
    w_g = jnp.pad(wi[:, c0:c0 + 2 * M_HEADS], ((0, 0), (0, LANES - 2 * M_HEADS))).astype(BF16)
    w_a = wi[:, c0 + 2 * M_HEADS:].astype(BF16)
    gate_bias = jnp.pad(jnp.concatenate([b_igate[0], b_fgate[0]]), (0, LANES - 2 * M_HEADS)).reshape(1, LANES)
    w = dict(
        wmo=w_mix_out[0].astype(BF16), ln1_g=row2(ln1_g), ln1_b=row2(ln1_b), wxq=w_xq[0].astype(BF16),
        wxo=w_xo[0].astype(BF16), ln2_g=row2(ln2_g), ln2_b=row2(ln2_b),
        wr=jnp.pad(w_router[0], ((0, 0), (0, LANES - N_EXPERTS))).astype(BF16),
        br=jnp.pad(b_router[0], (0, LANES - N_EXPERTS)).reshape(1, LANES),
        wg=w_gate, bg=b_gate[0].reshape(N_EXPERTS, 1, D_FF), wu=w_up, bu=b_up[0].reshape(N_EXPERTS, 1, D_FF),
        wd=w_down, bd=b_down[0].reshape(N_EXPERTS, 1, D_MODEL), ln3_g=row2(ln3_g), ln3_b=row2(ln3_b))
    gm = g_mnorm[0].reshape(1, MIX_V)

    xp = x_prompt.reshape(T, D_MODEL)
    zm, zg = _mm_multi(xp, [w_m, w_g], [BF16, F32], 512, "proj_mlstm")
    dils = tuple(dil for _, dil in PATTERNS)
    wbp = min(wb, T)
    *za_views, kv_tail = _proj_attn(xp, w_a, dils, wbp)
    zeros_c = jnp.zeros((1, M_HEADS, M_DQK, M_DV), F32)
    zeros_n = jnp.zeros((1, M_HEADS, M_DQK), F32)
    zeros_m = jnp.zeros((1, M_HEADS, LANES), F32)
    hm, pC, pn, pm = _mlstm(zm, zg, gate_bias, gm, zeros_c, zeros_n, zeros_m, 1, T // 128)
    pats = [_dilated_pattern(zv, window, dil) for zv, (window, dil) in zip(za_views, PATTERNS)]
    mp = mem_prompt.reshape(n_mem, D_MODEL)
    mem_k = _mm(mp, w_xk[0].astype(BF16), F32, n_mem, 1024, "mem_k")
    mem_v = _mm(mp, w_xv[0].astype(BF16), F32, n_mem, 1024, "mem_v")
    mem_k5 = mem_k.reshape(1, B, n_mem, X_HEADS, X_DH)
    mem_v5 = mem_v.reshape(1, B, n_mem, X_HEADS, X_DH)
    x1_p, q_p = _mix_ln1_q(xp, hm, [p[0] for p in pats], [p[1] for p in pats], dils, w["wmo"], w["ln1_g"], w["ln1_b"],
                           w["wxq"], 512)
    o_p = _xattn(q_p.reshape(1, T, D_MODEL), mem_k.astype(BF16).reshape(1, n_mem, D_MODEL),
                 mem_v.astype(BF16).reshape(1, n_mem, D_MODEL), 512).reshape(T, D_MODEL)

    ns = DB * S
    xs_ = x_sample.reshape(ns, D_MODEL)
    zm_s, zg_s, za_s = _mm_multi(xs_, [w_m, w_g, w_a], [BF16, F32, F32], ns, "proj_sample")
    pad_rows = 128 - S
    zm_pad = jnp.pad(zm_s.reshape(DB, S, -1), ((0, 0), (0, pad_rows), (0, 0))).reshape(DB * 128, -1)
    lane = jnp.arange(LANES)
    neutral = jnp.where(lane < M_HEADS, NEG_BIG, jnp.where(lane < 2 * M_HEADS, -NEG_BIG, 0.0)).astype(F32)
    zg_pad = jnp.concatenate([zg_s.reshape(DB, S, LANES), jnp.broadcast_to(neutral, (DB, pad_rows, LANES))],
                             axis=1).reshape(DB * 128, LANES)
    m0 = jnp.broadcast_to(state_mlstm_m[0][:, :, None], (DB, M_HEADS, LANES))
    hm_s, sC, sn, sm = _mlstm(zm_pad, zg_pad, gate_bias, gm, state_mlstm_C[0], state_mlstm_n[0], m0, DB, 1)
    hm_s = hm_s.reshape(DB, 128, MIX_V)[:, :S].reshape(ns, MIX_V)
    za_pad = jnp.pad(za_s.reshape(DB, S, -1), ((0, 0), (0, SWA_ROWS - S), (0, 0))).astype(BF16)
    ha_s = _swa_sample(za_pad, cache_swa_k.reshape(DB, wb * A_HEADS, A_DH),
                       cache_swa_v.reshape(DB, wb * A_HEADS, A_DH), S)[:, :S].reshape(ns, MIX_A)
    x1_s, q_s = _mix_ln1_q(xs_, hm_s, [ha_s], [], (1,), w["wmo"], w["ln1_g"], w["ln1_b"], w["wxq"], ns)
    q_pad = jnp.pad(q_s.reshape(DB, S, D_MODEL), ((0, 0), (0, SWA_ROWS - S), (0, 0)))
    o_s = _xattn_cache(q_pad, cache_mem_k, cache_mem_v)[:, :S].reshape(ns, D_MODEL)
    x2_s, xk_s, topi_s, gates_s = _xo_ln2_router(x1_s, o_s, w["wxo"], w["ln2_g"], w["ln2_b"], w["wr"], w["br"], ns)

    x2_all, xk_all, topi_p, gates_p = _xo_ln2_router(x1_p, o_p, w["wxo"], w["ln2_g"], w["ln2_b"], w["wr"], w["br"],
                                                     512, tail=(x2_s, xk_s))
    yp, ys_ = _moe(x2_all, xk_all, jnp.concatenate([topi_p, topi_s]),
                   jnp.concatenate([gates_p, gates_s]), w["wg"], w["bg"], w["wu"], w["bu"], w["wd"], w["bd"],
                   w["ln3_g"], w["ln3_b"], 1024, 128, T)

    return (yp.reshape(B, T, D_MODEL), ys_.reshape(DB, S, D_MODEL),
            pC[None], pn[None], pm[:, :, 0][None],
            kv_tail[:, :MIX_A].reshape(1, B, wbp, A_HEADS, A_DH), kv_tail[:, MIX_A:].reshape(1, B, wbp, A_HEADS, A_DH),
            mem_k5, mem_v5,
            sC[None], sn[None], sm[:, :, 0][None],
            za_s[:, MIX_A:2 * MIX_A].reshape(1, DB, S, A_HEADS, A_DH),
            za_s[:, 2 * MIX_A:].reshape(1, DB, S, A_HEADS, A_DH))
```

```python
import functools

import jax
import jax.numpy as jnp
from jax import lax
from jax.experimental import pallas as pl
from jax.experimental.pallas import tpu as pltpu

F32, BF16, I32 = jnp.float32, jnp.bfloat16, jnp.int32

D_MODEL = 2048
DEPTH = 1
M_HEADS, M_DQK, M_DV = 4, 128, 256
HEAD_EPS = 1e-6
A_HEADS, A_DH = 8, 128
PATTERNS = ((128, 1), (512, 4), (2048, 16))
BAND_BLOCK = 128
X_HEADS = 4
X_DH = D_MODEL // X_HEADS
N_EXPERTS, TOP_K, D_FF = 32, 4, 2048
SWIGLU_LIMIT, SWIGLU_ALPHA = 7.0, 1.702
DN_ALPHA = (2 * DEPTH) ** 0.25
LN_EPS = 1e-5
MIX_V = M_HEADS * M_DV
MIX_A = A_HEADS * A_DH
MIX_QK = M_HEADS * M_DQK

LANES = 128
VMEM_LIMIT_BYTES = 58 * 1024 * 1024

NEG_BIG = -1e30


def _cparams(n_axes, vmem=VMEM_LIMIT_BYTES):
    return pltpu.CompilerParams(dimension_semantics=("arbitrary",) * n_axes, vmem_limit_bytes=vmem)


def _dot(a, b):
    return jnp.dot(a, b, preferred_element_type=F32)


def _dot_nt(a, b):
    return lax.dot_general(a, b, (((1,), (1,)), ((), ())), preferred_element_type=F32)


def _dot_tn(a, b):
    return lax.dot_general(a, b, (((0,), (0,)), ((), ())), preferred_element_type=F32)


def _log_sigmoid(x):
    return jnp.minimum(x, 0.0) - jnp.log(1.0 + jnp.exp(-jnp.abs(x)))


def _sigmoid(x):
    return 1.0 / (1.0 + jnp.exp(-x))


def _layer_norm(v, g, b):
    mu = jnp.mean(v, axis=-1, keepdims=True)
    d = v - mu
    var = jnp.mean(d * d, axis=-1, keepdims=True)
    return d * lax.rsqrt(var + LN_EPS) * g + b


def _resident(shape):
    nd = len(shape)
    return pl.BlockSpec(shape, lambda *_: (0,) * nd, pipeline_mode=pl.Buffered(1))


def _mm_body(x_ref, w_ref, o_ref):
    o_ref[...] = _dot(x_ref[...].astype(BF16), w_ref[...]).astype(o_ref.dtype)


def _mm(x, w, out_dtype, tm, tn, name):
    M, K = x.shape
    N = w.shape[1]
    assert M % tm == 0 and N % tn == 0
    return pl.pallas_call(
        _mm_body,
        grid=(M // tm, N // tn),
        in_specs=[pl.BlockSpec((tm, K), lambda i, j: (i, 0)), pl.BlockSpec((K, tn), lambda i, j: (0, j))],
        out_specs=pl.BlockSpec((tm, tn), lambda i, j: (i, j)),
        out_shape=jax.ShapeDtypeStruct((M, N), out_dtype),
        compiler_params=_cparams(2),
        name=name,
    )(x, w)


def _mm_multi_body(x_ref, *refs, tn):
    n = len(refs) // 2
    xb = x_ref[...].astype(BF16)
    for w_ref, o_ref in zip(refs[:n], refs[n:]):
        width = w_ref.shape[1]
        step = min(tn, width)
        for j in range(width // step):
            o_ref[:, j * step:(j + 1) * step] = _dot(xb, w_ref[:, j * step:(j + 1) * step]).astype(o_ref.dtype)


def _mm_multi(x, ws, out_dtypes, tm, name, tn=1024):
    M, K = x.shape
    assert M % tm == 0 and all(w.shape[1] % min(tn, w.shape[1]) == 0 for w in ws)
    return pl.pallas_call(
        functools.partial(_mm_multi_body, tn=tn),
        grid=(M // tm,),
        in_specs=[pl.BlockSpec((tm, K), lambda i: (i, 0))] + [_resident(w.shape) for w in ws],
        out_specs=[pl.BlockSpec((tm, w.shape[1]), lambda i: (i, 0)) for w in ws],
        out_shape=[jax.ShapeDtypeStruct((M, w.shape[1]), dt) for w, dt in zip(ws, out_dtypes)],
        compiler_params=_cparams(1),
        name=name,
    )(x, *ws)


def _mlstm_body(q_ref, k_ref, v_ref, zo_ref, zg_ref, gb_ref, gm_ref, c0_ref, n0_ref, m0_ref,
                hm_ref, c_ref, n_ref, m_ref, *, L, sub):
    @pl.when(pl.program_id(1) == 0)
    def _():
        c_ref[...] = c0_ref[...]
        n_ref[...] = n0_ref[...]
        m_ref[...] = m0_ref[...]

    for ci in range(sub):
        _mlstm_chunk(slice(ci * L, (ci + 1) * L), q_ref, k_ref, v_ref, zo_ref, zg_ref, gb_ref, gm_ref,
                     hm_ref, c_ref, n_ref, m_ref, L)


def _mlstm_chunk(rows, q_ref, k_ref, v_ref, zo_ref, zg_ref, gb_ref, gm_ref, hm_ref, c_ref, n_ref, m_ref, L):
    scale = M_DQK ** -0.5
    g = zg_ref[rows, :] + gb_ref[...]
    gt = g.T
    row = lax.broadcasted_iota(I32, (L, L), 0)
    col = lax.broadcasted_iota(I32, (L, L), 1)
    tri = row >= col
    for h in range(M_HEADS):
        i_col = g[:, h:h + 1]
        f_col = _log_sigmoid(g[:, M_HEADS + h:M_HEADS + h + 1])
        i_row = gt[h:h + 1, :]
        f_row = _log_sigmoid(gt[M_HEADS + h:M_HEADS + h + 1, :])
        q = q_ref[rows, h * M_DQK:(h + 1) * M_DQK]
        k = k_ref[rows, h * M_DQK:(h + 1) * M_DQK]
        v = v_ref[rows, h * M_DV:(h + 1) * M_DV]
        zo = zo_ref[rows, h * M_DV:(h + 1) * M_DV].astype(F32)
        C = c_ref[0, h]
        n = n_ref[0, h:h + 1, :]
        m = m_ref[0, h:h + 1, 0:1]
        b_col = jnp.sum(jnp.where(tri, f_row, 0.0), axis=1, keepdims=True)
        b_row = jnp.sum(jnp.where(row <= col, f_col, 0.0), axis=0, keepdims=True)
        logd = jnp.where(tri, b_col - b_row + i_row, -jnp.inf)
        inter = b_col + m
        mt = jnp.maximum(inter, jnp.max(logd, axis=1, keepdims=True))
        sd = _dot_nt(q, k) * scale * jnp.exp(logd - mt)
        sc = jnp.exp(inter - mt)
        num = _dot(sd.astype(BF16), v) + sc * _dot(q, C.astype(BF16))
        qn = _dot_nt(q, jnp.broadcast_to(n, (16, M_DQK)).astype(BF16))[:, 0:1]
        den = jnp.sum(sd, axis=1, keepdims=True) + sc * qn
        hh = num / jnp.maximum(jnp.abs(den), jnp.exp(-mt))
        hn = hh * lax.rsqrt(jnp.mean(hh * hh, axis=1, keepdims=True) + HEAD_EPS)
        out = hn * gm_ref[:, h * M_DV:(h + 1) * M_DV] * _sigmoid(zo)
        hm_ref[rows, h * M_DV:(h + 1) * M_DV] = out.astype(hm_ref.dtype)
        bl = b_col[L - 1:L, :]
        ml = mt[L - 1:L, :]
        w_col = jnp.exp(bl - b_col + i_col - ml)
        scl = jnp.exp(bl + m - ml)
        kw = k.astype(F32) * (w_col * scale)
        c_ref[0, h] = scl * C + _dot_tn(kw.astype(BF16), v)
        n_ref[0, h:h + 1, :] = scl * n + jnp.sum(kw, axis=0, keepdims=True)
        m_ref[0, h:h + 1, :] = jnp.broadcast_to(ml, (1, LANES))


def _mlstm(zm, zg, gate_bias, g_mnorm, c0, n0, m0, B, nc, sub=1, L=128):
    T = B * nc * L
    assert zm.shape == (T, 2 * MIX_QK + 2 * MIX_V) and nc % sub == 0
    nc, L_chunk, L = nc // sub, L, L * sub
    row_blk = lambda b, c: b * nc + c
    state_specs = [pl.BlockSpec((1, M_HEADS, M_DQK, M_DV), lambda b, c: (b, 0, 0, 0)),
                   pl.BlockSpec((1, M_HEADS, M_DQK), lambda b, c: (b, 0, 0)),
                   pl.BlockSpec((1, M_HEADS, LANES), lambda b, c: (b, 0, 0))]
    return pl.pallas_call(
        functools.partial(_mlstm_body, L=L_chunk, sub=sub),
        grid=(B, nc),
        in_specs=[pl.BlockSpec((L, MIX_QK), lambda b, c: (row_blk(b, c), 0)),
                  pl.BlockSpec((L, MIX_QK), lambda b, c: (row_blk(b, c), 1)),
                  pl.BlockSpec((L, MIX_V), lambda b, c: (row_blk(b, c), 1)),
                  pl.BlockSpec((L, MIX_V), lambda b, c: (row_blk(b, c), 2)),
                  pl.BlockSpec((L, LANES), lambda b, c: (row_blk(b, c), 0)),
                  pl.BlockSpec((1, LANES), lambda b, c: (0, 0)),
                  pl.BlockSpec((1, MIX_V), lambda b, c: (0, 0))] + state_specs,
        out_specs=[pl.BlockSpec((L, MIX_V), lambda b, c: (row_blk(b, c), 0))] + state_specs,
        out_shape=[jax.ShapeDtypeStruct((T, MIX_V), BF16),
                   jax.ShapeDtypeStruct((B, M_HEADS, M_DQK, M_DV), F32),
                   jax.ShapeDtypeStruct((B, M_HEADS, M_DQK), F32),
                   jax.ShapeDtypeStruct((B, M_HEADS, LANES), F32)],
        compiler_params=_cparams(2),
        name="mlstm",
    )(zm, zm, zm, zm, zg, gate_bias, g_mnorm, c0, n0, m0)


DIL_BLOCKS_PER_STEP = 4


def _dil_body(q_ref, kc_ref, kp_ref, vc_ref, vp_ref, o_ref, l_ref, *, span):
    scale = A_DH ** -0.5
    n = pl.program_id(1)
    row = lax.broadcasted_iota(I32, (BAND_BLOCK, BAND_BLOCK), 0)
    col = lax.broadcasted_iota(I32, (BAND_BLOCK, BAND_BLOCK), 1)
    first = jnp.where(n > 0, 0, 2 * BAND_BLOCK)
    mask_c = row >= col
    lane = lax.broadcasted_iota(I32, (BAND_BLOCK, LANES), 1)
    ones = jnp.ones((BAND_BLOCK, A_DH), BF16)
    for j in range(DIL_BLOCKS_PER_STEP):
        rows = slice(j * BAND_BLOCK, (j + 1) * BAND_BLOCK)
        prev = slice((j - 1) * BAND_BLOCK, j * BAND_BLOCK)
        mask_p = (BAND_BLOCK + row - col + (first if j == 0 else 0)) <= span
        lse_tile = jnp.zeros((BAND_BLOCK, LANES), F32)
        for h in range(A_HEADS):
            sl = slice(h * A_DH, (h + 1) * A_DH)
            q = q_ref[rows, sl]
            kp = kp_ref[:, sl] if j == 0 else kc_ref[prev, sl]
            vp = vp_ref[:, sl] if j == 0 else vc_ref[prev, sl]
            sp = jnp.where(mask_p, _dot_nt(q, kp) * scale, -jnp.inf)
            sc = jnp.where(mask_c, _dot_nt(q, kc_ref[rows, sl]) * scale, -jnp.inf)
            m = jnp.max(jnp.maximum(sp, sc), axis=1, keepdims=True)
            pp = jnp.exp(sp - m).astype(BF16)
            pc = jnp.exp(sc - m).astype(BF16)
            oa = (_dot(pp, jnp.concatenate([vp, ones], axis=1))
                  + _dot(pc, jnp.concatenate([vc_ref[rows, sl], ones], axis=1)))
            l = oa[:, A_DH:A_DH + 1]
            o_ref[rows, sl] = (oa[:, :A_DH] / l).astype(o_ref.dtype)
            lse_tile = jnp.where(lane == h, m + jnp.log(l), lse_tile)
        l_ref[rows, :] = lse_tile


PROJ_ATTN_ROWS = 512


def _proj_attn_body(x_ref, w_ref, *refs, dils):
    o_refs, tail_ref, acc = refs[:len(dils)], refs[len(dils)], refs[len(dils) + 1]
    tm = x_ref.shape[0]
    nslab = acc.shape[0]
    tn = nslab * LANES
    xb = x_ref[...].astype(BF16)
    for j in range(w_ref.shape[1] // tn):
        z = _dot(xb, w_ref[:, j * tn:(j + 1) * tn])
        if j * tn >= MIX_A:
            tail_ref[:, j * tn - MIX_A:(j + 1) * tn - MIX_A] = z
        for dil, o_ref in zip(dils, o_refs):
            if dil == 1:
                o_ref[:, j * tn:(j + 1) * tn] = z.astype(o_ref.dtype)
        for s in range(nslab):
            acc[s] = z[:, s * LANES:(s + 1) * LANES]
        for dil, o_ref in zip(dils, o_refs):
            if dil == 1:
                continue
            rows = tm // dil
            for r in range(dil):
                for s in range(nslab):
                    c0 = r * w_ref.shape[1] + j * tn + s * LANES
                    o_ref[:, c0:c0 + LANES] = acc[s, pl.ds(r, rows, stride=dil), :].astype(o_ref.dtype)


def _proj_attn(x, w, dils, tail_rows):
    T, K = x.shape
    N = w.shape[1]
    tm, tn = PROJ_ATTN_ROWS, 1024
    assert T % tm == 0 and N % tn == 0 and all(tm % (16 * d) == 0 for d in dils)
    assert tail_rows % tm == 0 and MIX_A % tn == 0 and N == 3 * MIX_A
    first_tail = (T - tail_rows) // tm
    return pl.pallas_call(
        functools.partial(_proj_attn_body, dils=dils),
        grid=(T // tm,),
        in_specs=[pl.BlockSpec((tm, K), lambda i: (i, 0)), _resident(w.shape)],
        out_specs=[pl.BlockSpec((tm // d, d * N), lambda i: (i, 0)) for d in dils]
        + [pl.BlockSpec((tm, 2 * MIX_A), lambda i: (jnp.maximum(i - first_tail, 0), 0))],
        out_shape=[jax.ShapeDtypeStruct((T // d, d * N), BF16) for d in dils]
        + [jax.ShapeDtypeStruct((tail_rows, 2 * MIX_A), F32)],
        scratch_shapes=[pltpu.VMEM((tn // LANES, tm, LANES), F32)],
        compiler_params=_cparams(1),
        name="proj_attn",
    )(x, w)


def _dilated_pattern(zv, window, dil):
    N = zv.shape[0]
    span = window // dil
    step_rows = DIL_BLOCKS_PER_STEP * BAND_BLOCK
    assert zv.shape[1] == dil * 3 * MIX_A and N % step_rows == 0 and BAND_BLOCK - 1 <= span
    nb = N // step_rows
    prev = lambda n: jnp.maximum(n * DIL_BLOCKS_PER_STEP - 1, 0)
    blk = (step_rows, MIX_A)
    pblk = (BAND_BLOCK, MIX_A)
    o, lse = pl.pallas_call(
        functools.partial(_dil_body, span=span),
        grid=(dil, nb),
        in_specs=[pl.BlockSpec(blk, lambda r, n: (n, 3 * r)),
                  pl.BlockSpec(blk, lambda r, n: (n, 3 * r + 1)),
                  pl.BlockSpec(pblk, lambda r, n: (prev(n), 3 * r + 1)),
                  pl.BlockSpec(blk, lambda r, n: (n, 3 * r + 2)),
                  pl.BlockSpec(pblk, lambda r, n: (prev(n), 3 * r + 2))],
        out_specs=[pl.BlockSpec(blk, lambda r, n: (n, r)),
                   pl.BlockSpec((step_rows, LANES), lambda r, n: (n, r))],
        out_shape=[jax.ShapeDtypeStruct((N, dil * MIX_A), BF16),
                   jax.ShapeDtypeStruct((N, dil * LANES), F32)],
        compiler_params=_cparams(2),
        name=f"dilated_d{dil}",
    )(zv, zv, zv, zv, zv)
    return o, lse


SWA_ROWS = 16


def _swa_body(q_ref, kn_ref, vn_ref, pos_ref, kr_ref, vr_ref, k_hbm, v_hbm, o_ref, oldk, oldv, sem, *,
              wb, nb, n_new, n_old, big_dil, dense):
    scale = A_DH ** -0.5
    b = pl.program_id(0)
    slot = lax.rem(b, 2)
    grp = n_new * A_HEADS

    def old_copies(bi, sl):
        cps = []
        for a in range(n_old):
            src = pl.ds(a * big_dil * A_HEADS, grp)
            dst = pl.ds(sl * n_old * grp + a * grp, grp)
            cps.append(pltpu.make_async_copy(k_hbm.at[bi, src, :], oldk.at[dst, :], sem.at[sl]))
            cps.append(pltpu.make_async_copy(v_hbm.at[bi, src, :], oldv.at[dst, :], sem.at[sl]))
        return cps

    @pl.when(b == 0)
    def _():
        for cp in old_copies(0, 0):
            cp.start()

    for cp in old_copies(b, slot):
        cp.wait()
    nxt = jnp.minimum(b + 1, nb - 1)
    for cp in old_copies(nxt, 1 - slot):
        cp.start()

    ncols = n_new * n_old + dense
    s_k = lax.broadcasted_iota(I32, (SWA_ROWS, ncols), 0)
    d_k = wb + s_k - pos_ref[...]
    s_n = lax.broadcasted_iota(I32, (SWA_ROWS, SWA_ROWS), 0)
    p_n = lax.broadcasted_iota(I32, (SWA_ROWS, SWA_ROWS), 1)
    d_n = s_n - p_n
    masks = []
    for window, dil in PATTERNS:
        assert dil & (dil - 1) == 0 and window % dil == 0
        mk = jnp.where(jnp.bitwise_and(d_k, dil - 1) == 0, d_k, window + 1) <= window
        mn = jnp.where(jnp.bitwise_and(d_n, dil - 1) == 0, jnp.where(d_n >= 0, d_n, window + 1), window + 1) <= window
        masks.append((mk, mn))
    for h in range(A_HEADS):
        sl = slice(h * A_DH, (h + 1) * A_DH)
        q = q_ref[0, :, sl]

        def head_rows(old, rec_ref):
            parts = [old[pl.ds(slot * n_old * grp + s * A_HEADS + h, n_old, stride=grp), :] for s in range(n_new)]
            parts.append(rec_ref[0, pl.ds(h, dense, stride=A_HEADS), :])
            return jnp.concatenate(parts, axis=0).astype(BF16)

        kb = head_rows(oldk, kr_ref)
        vb = head_rows(oldv, vr_ref)
        kn = kn_ref[0, :, sl]
        vn = vn_ref[0, :, sl]
        s_cache = _dot_nt(q, kb) * scale
        s_new = _dot_nt(q, kn) * scale
        ps, lses = [], []
        for mk, mn in masks:
            sk = jnp.where(mk, s_cache, -jnp.inf)
            sn = jnp.where(mn, s_new, -jnp.inf)
            m = jnp.maximum(jnp.max(sk, axis=1, keepdims=True), jnp.max(sn, axis=1, keepdims=True))
            pk = jnp.exp(sk - m)
            pn = jnp.exp(sn - m)
            l = jnp.sum(pk, axis=1, keepdims=True) + jnp.sum(pn, axis=1, keepdims=True)
            ps.append((pk, pn, l))
            lses.append(m + jnp.log(l))
        top = functools.reduce(jnp.maximum, lses)
        es = [jnp.exp(x - top) for x in lses]
        tot = functools.reduce(lambda a, b: a + b, es)
        pk_all = None
        pn_all = None
        for (pk, pn, l), e in zip(ps, es):
            coef = e / (tot * l)
            pk_all = pk * coef if pk_all is None else pk_all + pk * coef
            pn_all = pn * coef if pn_all is None else pn_all + pn * coef
        o = _dot(pk_all.astype(BF16), vb) + _dot(pn_all.astype(BF16), vn)
        o_ref[0, :, sl] = o.astype(o_ref.dtype)

    @pl.when(b == nb - 1)
    def _():
        for cp in old_copies(nxt, 1 - slot):
            cp.wait()


def _swa_sample(za_pad, kbuf, vbuf, n_new):
    B, wb = kbuf.shape[0], kbuf.shape[1] // A_HEADS
    big_dil = PATTERNS[-1][1]
    dense = max(w for w, _ in PATTERNS[:-1])
    assert all(w <= PATTERNS[-1][0] and d <= big_dil for w, d in PATTERNS[:-1])
    assert wb % dense == 0 and dense % big_dil == 0 and wb % big_dil == 0 and n_new <= big_dil
    n_old = (wb - dense) // big_dil
    assert n_old % 8 == 0
    grp = n_new * A_HEADS
    old_pos = (jnp.arange(n_old, dtype=I32)[None, :] * big_dil + jnp.arange(n_new, dtype=I32)[:, None]).reshape(-1)
    pos = jnp.concatenate([old_pos, wb - dense + jnp.arange(dense, dtype=I32)]).reshape(1, -1)
    new_blk = (1, SWA_ROWS, MIX_A)
    rec_blk = (1, dense * A_HEADS, A_DH)
    last = wb // dense - 1
    return pl.pallas_call(
        functools.partial(_swa_body, wb=wb, nb=B, n_new=n_new, n_old=n_old, big_dil=big_dil, dense=dense),
        grid=(B,),
        in_specs=[pl.BlockSpec(new_blk, lambda b: (b, 0, 0)),
                  pl.BlockSpec(new_blk, lambda b: (b, 0, 1)),
                  pl.BlockSpec(new_blk, lambda b: (b, 0, 2)),
                  _resident(pos.shape),
                  pl.BlockSpec(rec_blk, lambda b: (b, last, 0)),
                  pl.BlockSpec(rec_blk, lambda b: (b, last, 0)),
                  pl.BlockSpec(memory_space=pl.ANY), pl.BlockSpec(memory_space=pl.ANY)],
        out_specs=pl.BlockSpec(new_blk, lambda b: (b, 0, 0)),
        out_shape=jax.ShapeDtypeStruct((B, SWA_ROWS, MIX_A), BF16),
        scratch_shapes=[pltpu.VMEM((2 * n_old * grp, A_DH), F32), pltpu.VMEM((2 * n_old * grp, A_DH), F32),
                        pltpu.SemaphoreType.DMA((2,))],
        compiler_params=_cparams(1),
        name="swa_sample",
    )(za_pad, za_pad, za_pad, pos, kbuf, vbuf, kbuf, vbuf)


def _mix_body(*refs, dils):
    n_pat = len(dils)
    x_ref, hm_ref = refs[0], refs[1]
    n_lse = n_pat if n_pat > 1 else 0
    o_refs = refs[2:2 + n_pat]
    l_refs = refs[2 + n_pat:2 + n_pat + n_lse]
    wmo_ref, g_ref, b_ref, wq_ref, x1_ref, q_ref = refs[2 + n_pat + n_lse:8 + n_pat + n_lse]
    if n_pat == 1:
        ha = o_refs[0][...]
    else:
        o_nat, l_nat = refs[8 + n_pat + n_lse:]
        tm = x_ref.shape[0]
        for p, dil in enumerate(dils):
            if dil == 1:
                continue
            rows = tm // dil
            for r in range(dil):
                dst = pl.ds(r, rows, stride=dil)
                l_nat[p, dst, :] = l_refs[p][:, r * LANES:(r + 1) * LANES]
                for h in range(A_HEADS):
                    c0 = r * MIX_A + h * A_DH
                    o_nat[p, h, dst, :] = o_refs[p][:, c0:c0 + A_DH].astype(F32)

        def lse_of(p, h):
            return l_refs[p][:, h:h + 1] if dils[p] == 1 else l_nat[p, :, h:h + 1]

        def out_of(p, h):
            return o_refs[p][:, h * A_DH:(h + 1) * A_DH].astype(F32) if dils[p] == 1 else o_nat[p, h]

        cols = []
        for h in range(A_HEADS):
            ls = [lse_of(p, h) for p in range(n_pat)]
            top = functools.reduce(jnp.maximum, ls)
            es = [jnp.exp(x - top) for x in ls]
            inv = 1.0 / functools.reduce(lambda a, b: a + b, es)
            acc = None
            for p, e in enumerate(es):
                term = out_of(p, h) * (e * inv)
                acc = term if acc is None else acc + term
            cols.append(acc.astype(BF16))
        ha = jnp.concatenate(cols, axis=1)
    mix = _dot(hm_ref[...], wmo_ref[0:MIX_V, :]) + _dot(ha, wmo_ref[MIX_V:MIX_V + MIX_A, :])
    x1 = _layer_norm(DN_ALPHA * x_ref[...] + mix, g_ref[...], b_ref[...])
    x1_ref[...] = x1
    q_ref[...] = _dot(x1.astype(BF16), wq_ref[...]).astype(q_ref.dtype)


def _mix_ln1_q(x, hm, outs, lses, dils, wmo, ln_g, ln_b, wq, tm):
    M = x.shape[0]
    n_pat = len(outs)
    assert len(lses) == (n_pat if n_pat > 1 else 0) and len(dils) == n_pat
    assert all(tm % (16 * d) == 0 for d in dils)
    row = lambda w: pl.BlockSpec((tm, w), lambda i: (i, 0))
    view = lambda w, d: pl.BlockSpec((tm // d, d * w), lambda i: (i, 0))
    scratch = []
    if n_pat > 1:
        scratch = [pltpu.VMEM((n_pat, A_HEADS, tm, A_DH), F32), pltpu.VMEM((n_pat, tm, LANES), F32)]
    return pl.pallas_call(
        functools.partial(_mix_body, dils=tuple(dils)),
        grid=(M // tm,),
        in_specs=[row(D_MODEL), row(MIX_V)] + [view(MIX_A, d) for d in dils]
        + [view(LANES, d) for d in dils[:len(lses)]]
        + [_resident(wmo.shape), _resident(ln_g.shape), _resident(ln_b.shape), _resident(wq.shape)],
        out_specs=[row(D_MODEL), row(D_MODEL)],
        out_shape=[jax.ShapeDtypeStruct((M, D_MODEL), F32), jax.ShapeDtypeStruct((M, D_MODEL), BF16)],
        scratch_shapes=scratch,
        compiler_params=_cparams(1),
        name="mix_ln1_q",
    )(x, hm, *outs, *lses, wmo, ln_g, ln_b, wq)


def _xattn_body(q_ref, k_ref, v_ref, o_ref):
    for h in range(X_HEADS):
        sl = slice(h * X_DH, (h + 1) * X_DH)
        o_ref[0, :, sl] = _xattn_head(q_ref[0, :, sl], k_ref[0, :, sl], v_ref[0, :, sl]).astype(o_ref.dtype)


def _xattn_head(q, k, v):
    s = _dot_nt(q, k.astype(BF16)) * (X_DH ** -0.5)
    p = jnp.exp(s - jnp.max(s, axis=1, keepdims=True))
    p = p / jnp.sum(p, axis=1, keepdims=True)
    return _dot(p.astype(BF16), v.astype(BF16))


def _xattn_cache_body(q_ref, k_hbm, v_hbm, o_ref, kbuf, vbuf, sem, *, nb):
    b = pl.program_id(0)
    slot = lax.rem(b, 2)

    def head_copies(bi, sl):
        cps = []
        for h in range(X_HEADS):
            cps.append(pltpu.make_async_copy(k_hbm.at[0, bi, :, h, :], kbuf.at[sl, h], sem.at[sl]))
            cps.append(pltpu.make_async_copy(v_hbm.at[0, bi, :, h, :], vbuf.at[sl, h], sem.at[sl]))
        return cps

    @pl.when(b == 0)
    def _():
        for cp in head_copies(0, 0):
            cp.start()

    for cp in head_copies(b, slot):
        cp.wait()
    nxt = jnp.minimum(b + 1, nb - 1)
    for cp in head_copies(nxt, 1 - slot):
        cp.start()
    for h in range(X_HEADS):
        sl = slice(h * X_DH, (h + 1) * X_DH)
        o_ref[0, :, sl] = _xattn_head(q_ref[0, :, sl], kbuf[slot, h], vbuf[slot, h]).astype(o_ref.dtype)

    @pl.when(b == nb - 1)
    def _():
        for cp in head_copies(nxt, 1 - slot):
            cp.wait()


def _xattn_cache(q, ck, cv):
    B, Tq, _ = q.shape
    nm = ck.shape[2]
    return pl.pallas_call(
        functools.partial(_xattn_cache_body, nb=B),
        grid=(B,),
        in_specs=[pl.BlockSpec((1, Tq, D_MODEL), lambda b: (b, 0, 0)),
                  pl.BlockSpec(memory_space=pl.ANY), pl.BlockSpec(memory_space=pl.ANY)],
        out_specs=pl.BlockSpec((1, Tq, D_MODEL), lambda b: (b, 0, 0)),
        out_shape=jax.ShapeDtypeStruct((B, Tq, D_MODEL), BF16),
        scratch_shapes=[pltpu.VMEM((2, X_HEADS, nm, X_DH), F32), pltpu.VMEM((2, X_HEADS, nm, X_DH), F32),
                        pltpu.SemaphoreType.DMA((2,))],
        compiler_params=_cparams(1),
        name="xattn_cache",
    )(q, ck, cv)


def _xattn(q, mk, mv, tq):
    B, Tq, _ = q.shape
    nm = mk.shape[1]
    mem_blk = (1, nm, D_MODEL)
    return pl.pallas_call(
        _xattn_body,
        grid=(B, Tq // tq),
        in_specs=[pl.BlockSpec((1, tq, D_MODEL), lambda b, i: (b, i, 0)),
                  pl.BlockSpec(mem_blk, lambda b, i: (b, 0, 0)),
                  pl.BlockSpec(mem_blk, lambda b, i: (b, 0, 0))],
        out_specs=pl.BlockSpec((1, tq, D_MODEL), lambda b, i: (b, i, 0)),
        out_shape=jax.ShapeDtypeStruct((B, Tq, D_MODEL), BF16),
        compiler_params=_cparams(2),
        name="xattn",
    )(q, mk, mv)


PACK_SLABS = D_MODEL // (2 * LANES)


def _pack_bf16_pairs(lo, hi):
    lo_bits = pltpu.bitcast(lo.astype(BF16).astype(F32), jnp.uint32)
    hi_bits = pltpu.bitcast(hi.astype(BF16).astype(F32), jnp.uint32)
    return jnp.bitwise_or(jnp.bitwise_and(hi_bits, jnp.uint32(0xFFFF0000)), lax.shift_right_logical(lo_bits, jnp.uint32(16)))


def _unpack_bf16_pairs(w):
    lo = pltpu.bitcast(lax.shift_left(w, jnp.uint32(16)), F32).astype(BF16)
    hi = pltpu.bitcast(jnp.bitwise_and(w, jnp.uint32(0xFFFF0000)), F32).astype(BF16)
    return lo, hi


def _xo_body(x1_ref, o_ref, wo_ref, g_ref, b_ref, wr_ref, br_ref, *rest, tail_rows, nsteps):
    if not tail_rows:
        _xo_rows(x1_ref, o_ref, wo_ref, g_ref, b_ref, wr_ref, br_ref, *rest)
        return
    tx2_ref, txp_ref, x2_ref, xp_ref, ti_ref, tg_ref = rest
    i = pl.program_id(0)

    @pl.when(i < nsteps)
    def _():
        _xo_rows(x1_ref, o_ref, wo_ref, g_ref, b_ref, wr_ref, br_ref, x2_ref, xp_ref, ti_ref, tg_ref)

    @pl.when(i == nsteps)
    def _():
        tm = x2_ref.shape[0]
        x2_ref[0:tail_rows, :] = tx2_ref[...]
        x2_ref[tail_rows:, :] = jnp.zeros((tm - tail_rows, D_MODEL), F32)
        xp_ref[0:tail_rows * PACK_SLABS, :] = txp_ref[...]
        xp_ref[tail_rows * PACK_SLABS:, :] = jnp.zeros(((tm - tail_rows) * PACK_SLABS, LANES), jnp.uint32)


def _xo_rows(x1_ref, o_ref, wo_ref, g_ref, b_ref, wr_ref, br_ref, x2_ref, xp_ref, ti_ref, tg_ref):
    y = _dot(o_ref[...], wo_ref[...])
    x2 = _layer_norm(DN_ALPHA * x1_ref[...] + y, g_ref[...], b_ref[...])
    x2_ref[...] = x2
    tm = x2.shape[0]
    for s in range(PACK_SLABS):
        lo = x2[:, s * LANES:(s + 1) * LANES]
        hi = x2[:, D_MODEL // 2 + s * LANES:D_MODEL // 2 + (s + 1) * LANES]
        xp_ref[pl.ds(s, tm, stride=PACK_SLABS), :] = _pack_bf16_pairs(lo, hi)
    lane = lax.broadcasted_iota(I32, (tm, LANES), 1)
    lanef = lane.astype(F32)
    logits = jnp.where(lane < N_EXPERTS, _dot(x2.astype(BF16), wr_ref[...]) + br_ref[...], -jnp.inf)
    vals, idxs = [], []
    cur = logits
    for _ in range(TOP_K):
        top = jnp.max(cur, axis=1, keepdims=True)
        idx = jnp.min(jnp.where(cur == top, lanef, float(LANES)), axis=1, keepdims=True)
        vals.append(top)
        idxs.append(idx)
        cur = jnp.where(lanef == idx, -jnp.inf, cur)
    es = [jnp.exp(v - vals[0]) for v in vals]
    inv = 1.0 / functools.reduce(lambda a, b: a + b, es)
    ti = jnp.zeros((tm, LANES), F32)
    tg = jnp.zeros((tm, LANES), F32)
    for k in range(TOP_K):
        ti = jnp.where(lane == k, idxs[k], ti)
        tg = jnp.where(lane == k, es[k] * inv, tg)
    ti_ref[...] = ti.astype(I32)
    tg_ref[...] = tg


def _xo_ln2_router(x1, o, wo, ln_g, ln_b, wr, br, tm, tail=None):
    M = x1.shape[0]
    nsteps = M // tm
    tail_rows = 0 if tail is None else tail[0].shape[0]
    assert tail_rows <= tm
    extra = 1 if tail_rows else 0
    last = nsteps - 1
    row_in = lambda w: pl.BlockSpec((tm, w), lambda i: (jnp.minimum(i, last), 0))
    tail_specs = [] if tail is None else [_resident(tail[0].shape), _resident(tail[1].shape)]
    return pl.pallas_call(
        functools.partial(_xo_body, tail_rows=tail_rows, nsteps=nsteps),
        grid=(nsteps + extra,),
        in_specs=[row_in(D_MODEL), row_in(D_MODEL), _resident(wo.shape), _resident(ln_g.shape),
                  _resident(ln_b.shape), _resident(wr.shape), _resident(br.shape)] + tail_specs,
        out_specs=[pl.BlockSpec((tm, D_MODEL), lambda i: (i, 0)),
                   pl.BlockSpec((tm * PACK_SLABS, LANES), lambda i: (i, 0)), row_in(LANES), row_in(LANES)],
        out_shape=[jax.ShapeDtypeStruct((M + extra * tm, D_MODEL), F32),
                   jax.ShapeDtypeStruct(((M + extra * tm) * PACK_SLABS, LANES), jnp.uint32),
                   jax.ShapeDtypeStruct((M, LANES), I32), jax.ShapeDtypeStruct((M, LANES), F32)],
        compiler_params=_cparams(1),
        name="xo_ln2_router",
    )(x1, o, wo, ln_g, ln_b, wr, br, *([] if tail is None else list(tail)))


MOE_SUB_ROWS = 256
MOE_FF_CHUNK = 256


def _ffn_body(te_ref, tr_ref, nu_ref, idc_ref, idn_ref, x_hbm, wg_ref, bg_ref, wu_ref, bu_ref, wd_ref, bd_ref,
              o_ref, stg, sem, xbb, wgb, wub, wdb, *, tm, sb, nch):
    t = pl.program_id(0)
    c = pl.program_id(1)
    n_used = nu_ref[0]
    ps = tm // nch
    gp = ps * PACK_SLABS
    cur = lax.rem(t, 2)
    tp = tm * PACK_SLABS

    def row_copy(src_row, sl, g, r):
        src_row = pl.multiple_of(src_row, PACK_SLABS)
        dst = pl.ds(sl * tp + g * gp + r * PACK_SLABS, PACK_SLABS)
        return pltpu.make_async_copy(x_hbm.at[pl.ds(src_row, PACK_SLABS), :], stg.at[dst, :], sem.at[sl * nch + g])

    def group_wait(sl, g):
        pltpu.make_async_copy(x_hbm.at[pl.ds(0, gp), :], stg.at[pl.ds(sl * tp + g * gp, gp), :],
                              sem.at[sl * nch + g]).wait()

    def unpack_group(sl, g):
        rows_g = slice(g * ps, (g + 1) * ps)
        for s in range(PACK_SLABS):
            lo, hi = _unpack_bf16_pairs(stg[pl.ds(sl * tp + g * gp + s, ps, stride=PACK_SLABS), :])
            xbb[rows_g, s * LANES:(s + 1) * LANES] = lo
            xbb[rows_g, D_MODEL // 2 + s * LANES:D_MODEL // 2 + (s + 1) * LANES] = hi

    def ffn_rows(rs, wg, wu, wd):
        xb = xbb[rs, :]
        g = jnp.minimum(_dot(xb, wg) + bg_ref[0], SWIGLU_LIMIT)
        u = jnp.clip(_dot(xb, wu) + bu_ref[0], -SWIGLU_LIMIT, SWIGLU_LIMIT)
        hid = (u + 1.0) * (g * _sigmoid(SWIGLU_ALPHA * g))
        o_ref[rs, :] += _dot(hid.astype(BF16), wd)

    @pl.when(t >= n_used)
    def _():
        @pl.when(c == 0)
        def _():
            o_ref[...] = jnp.zeros((tm, D_MODEL), F32)

    @pl.when(t < n_used)
    def _():
        rows = tr_ref[t]
        rows_next = jnp.where(t + 1 < n_used, tr_ref[jnp.minimum(t + 1, n_used - 1)], 0)

        @pl.when((t == 0) & (c == 0))
        def _():
            xbb[...] = jnp.zeros((tm, D_MODEL), BF16)
            for g in range(nch):
                @pl.when(g * ps < rows)
                def _():
                    def issue(r, carry):
                        row_copy(idc_ref[0, 0, g * ps + r], 0, g, r).start()
                        return carry
                    lax.fori_loop(0, ps, issue, 0, unroll=8)

        @pl.when(c == 0)
        def _():
            for g in range(nch):
                @pl.when(g * ps < rows)
                def _():
                    group_wait(cur, g)
                    unpack_group(cur, g)

        @pl.when(c * ps < rows_next)
        def _():
            for r in range(ps):
                row_copy(idn_ref[0, 0, c * ps + r], 1 - cur, c, r).start()

        @pl.when(rows > tm - sb)
        def _():
            @pl.when(c == 0)
            def _():
                o_ref[...] = jnp.broadcast_to(bd_ref[0], (tm, D_MODEL))
            ffn_rows(slice(0, tm), wg_ref[0, 0].astype(BF16), wu_ref[0, 0].astype(BF16), wd_ref[0, 0].astype(BF16))

        @pl.when(rows <= tm - sb)
        def _():
            wgb[...] = wg_ref[0, 0].astype(BF16)
            wub[...] = wu_ref[0, 0].astype(BF16)
            wdb[...] = wd_ref[0, 0].astype(BF16)
            for s in range(tm // sb):
                rs = slice(s * sb, (s + 1) * sb)

                @pl.when((s * sb >= rows) & (c == 0))
                def _():
                    o_ref[rs, :] = jnp.zeros((sb, D_MODEL), F32)

                @pl.when(s * sb < rows)
                def _():
                    @pl.when(c == 0)
                    def _():
                        o_ref[rs, :] = jnp.broadcast_to(bd_ref[0], (sb, D_MODEL))
                    ffn_rows(rs, wgb[...], wub[...], wdb[...])


def _moe_ffn(x, row_tok, tile_e, tile_rows, n_used, wg, bg, wu, bu, wd, bd, tm, sb):
    P = row_tok.shape[0]
    n_tiles = P // tm
    nch = D_FF // MOE_FF_CHUNK
    fc = MOE_FF_CHUNK
    assert tm % nch == 0 and (tm // nch) % 16 == 0 and nch >= 3

    def tile(t, nu):
        return jnp.minimum(t, jnp.maximum(nu[0] - 1, 0))

    def chunk(t, c, nu):
        return jnp.where(t < nu[0], c, nch - 1)

    idx_blk = (1, 1, tm)
    return pl.pallas_call(
        functools.partial(_ffn_body, tm=tm, sb=sb, nch=nch),
        grid_spec=pltpu.PrefetchScalarGridSpec(
            num_scalar_prefetch=3,
            grid=(n_tiles, nch),
            in_specs=[
                pl.BlockSpec(idx_blk, lambda t, c, te, tr, nu: (tile(t, nu), 0, 0), memory_space=pltpu.SMEM),
                pl.BlockSpec(idx_blk, lambda t, c, te, tr, nu: (tile(t + 1, nu), 0, 0), memory_space=pltpu.SMEM),
                pl.BlockSpec(memory_space=pl.ANY),
                pl.BlockSpec((1, 1, D_MODEL, fc), lambda t, c, te, tr, nu: (0, te[tile(t, nu)], 0, chunk(t, c, nu))),
                pl.BlockSpec((1, 1, fc), lambda t, c, te, tr, nu: (te[tile(t, nu)], 0, chunk(t, c, nu))),
                pl.BlockSpec((1, 1, D_MODEL, fc), lambda t, c, te, tr, nu: (0, te[tile(t, nu)], 0, chunk(t, c, nu))),
                pl.BlockSpec((1, 1, fc), lambda t, c, te, tr, nu: (te[tile(t, nu)], 0, chunk(t, c, nu))),
                pl.BlockSpec((1, 1, fc, D_MODEL), lambda t, c, te, tr, nu: (0, te[tile(t, nu)], chunk(t, c, nu), 0)),
                pl.BlockSpec((1, 1, D_MODEL), lambda t, c, te, tr, nu: (te[tile(t, nu)], 0, 0)),
            ],
            out_specs=pl.BlockSpec((tm, D_MODEL), lambda t, c, te, tr, nu: (t, 0)),
            scratch_shapes=[pltpu.VMEM((2 * tm * PACK_SLABS, LANES), jnp.uint32),
                            pltpu.SemaphoreType.DMA((2 * nch,)),
                            pltpu.VMEM((tm, D_MODEL), BF16),
                            pltpu.VMEM((D_MODEL, fc), BF16), pltpu.VMEM((D_MODEL, fc), BF16),
                            pltpu.VMEM((fc, D_MODEL), BF16)]),
        out_shape=jax.ShapeDtypeStruct((P, D_MODEL), F32),
        compiler_params=_cparams(2),
        name="moe_ffn",
    )(tile_e, tile_rows, n_used, row_tok.reshape(n_tiles, 1, tm), row_tok.reshape(n_tiles, 1, tm), x,
      wg, bg, wu, bu, wd, bd)


def _combine_body(posc_ref, posn_ref, x2_ref, tg_ref, ys_hbm, g_ref, b_ref, oa_ref, ob_ref, buf, sem, *,
                  tc, nblk, nblk_a):
    i = pl.program_id(0)
    slot = lax.rem(i, 2)

    def row_copy(src_row, sl, k, j):
        return pltpu.make_async_copy(ys_hbm.at[pl.ds(src_row, 1), :], buf.at[sl, k, pl.ds(j, 1), :], sem.at[sl])

    def slot_wait(sl):
        pltpu.make_async_copy(buf.at[sl], buf.at[sl], sem.at[sl]).wait()

    @pl.when(i == 0)
    def _():
        def issue(j, carry):
            for k in range(TOP_K):
                row_copy(posc_ref[0, 0, j * TOP_K + k], 0, k, j).start()
            return carry
        lax.fori_loop(0, tc, issue, 0, unroll=4)

    slot_wait(slot)
    for j in range(tc):
        for k in range(TOP_K):
            row_copy(posn_ref[0, 0, j * TOP_K + k], 1 - slot, k, j).start()
    y = None
    for k in range(TOP_K):
        term = buf[slot, k] * tg_ref[:, k:k + 1]
        y = term if y is None else y + term
    out = _layer_norm(DN_ALPHA * x2_ref[...] + y, g_ref[...], b_ref[...])

    @pl.when(i < nblk_a)
    def _():
        oa_ref[...] = out

    @pl.when(i >= nblk_a)
    def _():
        ob_ref[...] = out

    @pl.when(i == nblk - 1)
    def _():
        slot_wait(1 - slot)


def _combine_ln3(x2, tg, ys, pos, ln_g, ln_b, tc, split):
    M = pos.shape[0] // TOP_K
    nblk = M // tc
    nblk_a = split // tc
    assert split % tc == 0 and 0 < nblk_a < nblk
    pos3 = pos.reshape(nblk, 1, tc * TOP_K)
    return pl.pallas_call(
        functools.partial(_combine_body, tc=tc, nblk=nblk, nblk_a=nblk_a),
        grid=(nblk,),
        in_specs=[pl.BlockSpec((1, 1, tc * TOP_K), lambda i: (i, 0, 0), memory_space=pltpu.SMEM),
                  pl.BlockSpec((1, 1, tc * TOP_K), lambda i: (jnp.minimum(i + 1, nblk - 1), 0, 0),
                               memory_space=pltpu.SMEM),
                  pl.BlockSpec((tc, D_MODEL), lambda i: (i, 0)),
                  pl.BlockSpec((tc, LANES), lambda i: (i, 0)),
                  pl.BlockSpec(memory_space=pl.ANY),
                  _resident(ln_g.shape), _resident(ln_b.shape)],
        out_specs=[pl.BlockSpec((tc, D_MODEL), lambda i: (jnp.minimum(i, nblk_a - 1), 0)),
                   pl.BlockSpec((tc, D_MODEL), lambda i: (jnp.maximum(i - nblk_a, 0), 0))],
        out_shape=[jax.ShapeDtypeStruct((split, D_MODEL), F32), jax.ShapeDtypeStruct((M - split, D_MODEL), F32)],
        scratch_shapes=[pltpu.VMEM((2, TOP_K, tc, D_MODEL), F32), pltpu.SemaphoreType.DMA((2,))],
        compiler_params=_cparams(1),
        name="moe_combine",
    )(pos3, pos3, x2, tg, ys, ln_g, ln_b)


def _moe(x2, x2_packed, topi, gates, wg, bg, wu, bu, wd, bd, ln_g, ln_b, tm, tc, split):
    ntok = topi.shape[0]
    nk = ntok * TOP_K
    sb = min(tm, MOE_SUB_ROWS)
    n_tiles = -(-(nk + N_EXPERTS * (tm - 1)) // tm)
    e_flat = topi[:, :TOP_K].reshape(nk)
    onehot = (e_flat[:, None] == jnp.arange(N_EXPERTS, dtype=I32)[None, :]).astype(I32)
    csum = jnp.cumsum(onehot, axis=0)
    counts = csum[-1]
    rank = jnp.take_along_axis(csum, e_flat[:, None], axis=1)[:, 0] - 1
    tiles_e = (counts + tm - 1) // tm
    tile_end = jnp.cumsum(tiles_e)
    tile_start = tile_end - tiles_e
    dest = tile_start[e_flat] * tm + rank
    n_used = tile_end[-1:].astype(I32)
    t_ids = jnp.arange(n_tiles, dtype=I32)
    tile_e = jnp.minimum(jnp.searchsorted(tile_end, t_ids, side="right"), N_EXPERTS - 1).astype(I32)
    tile_rows = jnp.clip(counts[tile_e] - (t_ids - tile_start[tile_e]) * tm, 0, tm)
    tile_rows = jnp.where(t_ids < n_used[0], tile_rows, 0).astype(I32)
    P = n_tiles * tm
    first_row = (jnp.arange(nk, dtype=I32) // TOP_K) * PACK_SLABS
    row_src = jnp.zeros((P,), I32).at[dest].set(first_row, unique_indices=True)
    ys = _moe_ffn(x2_packed, row_src, tile_e, tile_rows, n_used, wg, bg, wu, bu, wd, bd, tm, sb)
    return _combine_ln3(x2, gates, ys, dest, ln_g, ln_b, tc, split)


def kernel(x_prompt, x_sample, mem_prompt, state_mlstm_C, state_mlstm_n, state_mlstm_m, cache_swa_k, cache_swa_v, cache_mem_k, cache_mem_v, w_in, b_igate, b_fgate, g_mnorm, w_mix_out, ln1_g, ln1_b, w_xq, w_xk, w_xv, w_xo, ln2_g, ln2_b, w_router, b_router, w_gate, b_gate, w_up, b_up, w_down, b_down, ln3_g, ln3_b):
    assert DEPTH == 1
    B, T, _ = x_prompt.shape
    DB, S, _ = x_sample.shape
    assert B == 1
    n_mem = mem_prompt.shape[1]
    wb = cache_swa_k.shape[2]
    row2 = lambda a: a[0].reshape(1, -1)

    wi = w_in[0]
    c0 = 2 * MIX_QK + 2 * MIX_V
    w_m = wi[:, :c0].astype(BF16)
    w_g = jnp.pad(wi[:, c0:c0 + 2 * M_HEADS], ((0, 0), (0, LANES - 2 * M_HEADS))).astype(BF16)
    w_a = wi[:, c0 + 2 * M_HEADS:].astype(BF16)
    gate_bias = jnp.pad(jnp.concatenate([b_igate[0], b_fgate[0]]), (0, LANES - 2 * M_HEADS)).reshape(1, LANES)
    w = dict(
        wmo=w_mix_out[0].astype(BF16), ln1_g=row2(ln1_g), ln1_b=row2(ln1_b), wxq=w_xq[0].astype(BF16),
        wxo=w_xo[0].astype(BF16), ln2_g=row2(ln2_g), ln2_b=row2(ln2_b),
        wr=jnp.pad(w_router[0], ((0, 0), (0, LANES - N_EXPERTS))).astype(BF16),
        br=jnp.pad(b_router[0], (0, LANES - N_EXPERTS)).reshape(1, LANES),
        wg=w_gate, bg=b_gate[0].reshape(N_EXPERTS, 1, D_FF), wu=w_up, bu=b_up[0].reshape(N_EXPERTS, 1, D_FF),
        wd=w_down, bd=b_down[0].reshape(N_EXPERTS, 1, D_MODEL), ln3_g=row2(ln3_g), ln3_b=row2(ln3_b))
    gm = g_mnorm[0].reshape(1, MIX_V)

    xp = x_prompt.reshape(T, D_MODEL)
    zm, zg = _mm_multi(xp, [w_m, w_g], [BF16, F32], 1024, "proj_mlstm")
    dils = tuple(dil for _, dil in PATTERNS)
    wbp = min(wb, T)
    *za_views, kv_tail = _proj_attn(xp, w_a, dils, wbp)
    zeros_c = jnp.zeros((1, M_HEADS, M_DQK, M_DV), F32)
    zeros_n = jnp.zeros((1, M_HEADS, M_DQK), F32)
    zeros_m = jnp.zeros((1, M_HEADS, LANES), F32)
    hm, pC, pn, pm = _mlstm(zm, zg, gate_bias, gm, zeros_c, zeros_n, zeros_m, 1, T // 512, L=512)
    pats = [_dilated_pattern(zv, window, dil) for zv, (window, dil) in zip(za_views, PATTERNS)]
    mp = mem_prompt.reshape(n_mem, D_MODEL)
    mem_k = _mm(mp, w_xk[0].astype(BF16), F32, n_mem, 1024, "mem_k")
    mem_v = _mm(mp, w_xv[0].astype(BF16), F32, n_mem, 1024, "mem_v")
    mem_k5 = mem_k.reshape(1, B, n_mem, X_HEADS, X_DH)
    mem_v5 = mem_v.reshape(1, B, n_mem, X_HEADS, X_DH)
    x1_p, q_p = _mix_ln1_q(xp, hm, [p[0] for p in pats], [p[1] for p in pats], dils, w["wmo"], w["ln1_g"], w["ln1_b"],
                           w["wxq"], 512)
    o_p = _xattn(q_p.reshape(1, T, D_MODEL), mem_k.astype(BF16).reshape(1, n_mem, D_MODEL),
                 mem_v.astype(BF16).reshape(1, n_mem, D_MODEL), 1024).reshape(T, D_MODEL)

    ns = DB * S
    xs_ = x_sample.reshape(ns, D_MODEL)
    zm_s, zg_s, za_s = _mm_multi(xs_, [w_m, w_g, w_a], [BF16, F32, F32], ns, "proj_sample")
    pad_rows = 128 - S
    zm_pad = jnp.pad(zm_s.reshape(DB, S, -1), ((0, 0), (0, pad_rows), (0, 0))).reshape(DB * 128, -1)
    lane = jnp.arange(LANES)
    neutral = jnp.where(lane < M_HEADS, NEG_BIG, jnp.where(lane < 2 * M_HEADS, -NEG_BIG, 0.0)).astype(F32)
    zg_pad = jnp.concatenate([zg_s.reshape(DB, S, LANES), jnp.broadcast_to(neutral, (DB, pad_rows, LANES))],
                             axis=1).reshape(DB * 128, LANES)
    m0 = jnp.broadcast_to(state_mlstm_m[0][:, :, None], (DB, M_HEADS, LANES))
    hm_s, sC, sn, sm = _mlstm(zm_pad, zg_pad, gate_bias, gm, state_mlstm_C[0], state_mlstm_n[0], m0, DB, 1)
    hm_s = hm_s.reshape(DB, 128, MIX_V)[:, :S].reshape(ns, MIX_V)
    za_pad = jnp.pad(za_s.reshape(DB, S, -1), ((0, 0), (0, SWA_ROWS - S), (0, 0))).astype(BF16)
    ha_s = _swa_sample(za_pad, cache_swa_k.reshape(DB, wb * A_HEADS, A_DH),
                       cache_swa_v.reshape(DB, wb * A_HEADS, A_DH), S)[:, :S].reshape(ns, MIX_A)
    x1_s, q_s = _mix_ln1_q(xs_, hm_s, [ha_s], [], (1,), w["wmo"], w["ln1_g"], w["ln1_b"], w["wxq"], ns)
    q_pad = jnp.pad(q_s.reshape(DB, S, D_MODEL), ((0, 0), (0, SWA_ROWS - S), (0, 0)))
    o_s = _xattn_cache(q_pad, cache_mem_k, cache_mem_v)[:, :S].reshape(ns, D_MODEL)
    x2_s, xk_s, topi_s, gates_s = _xo_ln2_router(x1_s, o_s, w["wxo"], w["ln2_g"], w["ln2_b"], w["wr"], w["br"], ns)

    x2_all, xk_all, topi_p, gates_p = _xo_ln2_router(x1_p, o_p, w["wxo"], w["ln2_g"], w["ln2_b"], w["wr"], w["br"],
                                                     512, tail=(x2_s, xk_s))
    yp, ys_ = _moe(x2_all, xk_all, jnp.concatenate([topi_p, topi_s]),
                   jnp.concatenate([gates_p, gates_s]), w["wg"], w["bg"], w["wu"], w["bu"], w["wd"], w["bd"],
                   w["ln3_g"], w["ln3_b"], 1024, 128, T)

    return (yp.reshape(B, T, D_MODEL), ys_.reshape(DB, S, D_MODEL),
            pC[None], pn[None], pm[:, :, 0][None],
            kv_tail[:, :MIX_A].reshape(1, B, wbp, A_HEADS, A_DH), kv_tail[:, MIX_A:].reshape(1, B, wbp, A_HEADS, A_DH),
            mem_k5, mem_v5,
            sC[None], sn[None], sm[:, :, 0][None],
            za_s[:, MIX_A:2 * MIX_A].reshape(1, DB, S, A_HEADS, A_DH),
            za_s[:, 2 * MIX_A:].reshape(1, DB, S, A_HEADS, A_DH))
```

```python
import functools

import jax
import jax.numpy as jnp
from jax import lax
from jax.experimental import pallas as pl
from jax.experimental.pallas import tpu as pltpu

F32, BF16, I32 = jnp.float32, jnp.bfloat16, jnp.int32

D_MODEL = 2048
DEPTH = 1
M_HEADS, M_DQK, M_DV = 4, 128, 256
HEAD_EPS = 1e-6
A_HEADS, A_DH = 8, 128
PATTERNS = ((128, 1), (512, 4), (2048, 16))
BAND_BLOCK = 128
X_HEADS = 4
X_DH = D_MODEL // X_HEADS
N_EXPERTS, TOP_K, D_FF = 32, 4, 2048
SWIGLU_LIMIT, SWIGLU_ALPHA = 7.0, 1.702
DN_ALPHA = (2 * DEPTH) ** 0.25
LN_EPS = 1e-5
MIX_V = M_HEADS * M_DV
MIX_A = A_HEADS * A_DH
MIX_QK = M_HEADS * M_DQK

LANES = 128
VMEM_LIMIT_BYTES = 58 * 1024 * 1024

NEG_BIG = -1e30


def _cparams(n_axes, vmem=VMEM_LIMIT_BYTES):
    return pltpu.CompilerParams(dimension_semantics=("arbitrary",) * n_axes, vmem_limit_bytes=vmem)


def _dot(a, b):
    return jnp.dot(a, b, preferred_element_type=F32)


def _dot_nt(a, b):
    return lax.dot_general(a, b, (((1,), (1,)), ((), ())), preferred_element_type=F32)


def _dot_tn(a, b):
    return lax.dot_general(a, b, (((0,), (0,)), ((), ())), preferred_element_type=F32)


def _log_sigmoid(x):
    return jnp.minimum(x, 0.0) - jnp.log(1.0 + jnp.exp(-jnp.abs(x)))


def _sigmoid(x):
    return 1.0 / (1.0 + jnp.exp(-x))


def _layer_norm(v, g, b):
    mu = jnp.mean(v, axis=-1, keepdims=True)
    d = v - mu
    var = jnp.mean(d * d, axis=-1, keepdims=True)
    return d * lax.rsqrt(var + LN_EPS) * g + b


def _resident(shape):
    nd = len(shape)
    return pl.BlockSpec(shape, lambda *_: (0,) * nd, pipeline_mode=pl.Buffered(1))


def _mm_body(x_ref, w_ref, o_ref):
    o_ref[...] = _dot(x_ref[...].astype(BF16), w_ref[...]).astype(o_ref.dtype)


def _mm(x, w, out_dtype, tm, tn, name):
    M, K = x.shape
    N = w.shape[1]
    assert M % tm == 0 and N % tn == 0
    return pl.pallas_call(
        _mm_body,
        grid=(M // tm, N // tn),
        in_specs=[pl.BlockSpec((tm, K), lambda i, j: (i, 0)), pl.BlockSpec((K, tn), lambda i, j: (0, j))],
        out_specs=pl.BlockSpec((tm, tn), lambda i, j: (i, j)),
        out_shape=jax.ShapeDtypeStruct((M, N), out_dtype),
        compiler_params=_cparams(2),
        name=name,
    )(x, w)


def _mm_multi_body(x_ref, *refs, tn):
    n = len(refs) // 2
    xb = x_ref[...].astype(BF16)
    for w_ref, o_ref in zip(refs[:n], refs[n:]):
        width = w_ref.shape[1]
        step = min(tn, width)
        for j in range(width // step):
            o_ref[:, j * step:(j + 1) * step] = _dot(xb, w_ref[:, j * step:(j + 1) * step]).astype(o_ref.dtype)


def _mm_multi(x, ws, out_dtypes, tm, name, tn=1024):
    M, K = x.shape
    assert M % tm == 0 and all(w.shape[1] % min(tn, w.shape[1]) == 0 for w in ws)
    return pl.pallas_call(
        functools.partial(_mm_multi_body, tn=tn),
        grid=(M // tm,),
        in_specs=[pl.BlockSpec((tm, K), lambda i: (i, 0))] + [_resident(w.shape) for w in ws],
        out_specs=[pl.BlockSpec((tm, w.shape[1]), lambda i: (i, 0)) for w in ws],
        out_shape=[jax.ShapeDtypeStruct((M, w.shape[1]), dt) for w, dt in zip(ws, out_dtypes)],
        compiler_params=_cparams(1),
        name=name,
    )(x, *ws)


def _mlstm_body(q_ref, k_ref, v_ref, zo_ref, zg_ref, gb_ref, gm_ref, c0_ref, n0_ref, m0_ref,
                hm_ref, c_ref, n_ref, m_ref, *, L, sub):
    @pl.when(pl.program_id(1) == 0)
    def _():
        c_ref[...] = c0_ref[...]
        n_ref[...] = n0_ref[...]
        m_ref[...] = m0_ref[...]

    for ci in range(sub):
        _mlstm_chunk(slice(ci * L, (ci + 1) * L), q_ref, k_ref, v_ref, zo_ref, zg_ref, gb_ref, gm_ref,
                     hm_ref, c_ref, n_ref, m_ref, L)


def _mlstm_chunk(rows, q_ref, k_ref, v_ref, zo_ref, zg_ref, gb_ref, gm_ref, hm_ref, c_ref, n_ref, m_ref, L):
    scale = M_DQK ** -0.5
    g = zg_ref[rows, :] + gb_ref[...]
    gt = g.T
    row = lax.broadcasted_iota(I32, (L, L), 0)
    col = lax.broadcasted_iota(I32, (L, L), 1)
    tri = row >= col
    for h in range(M_HEADS):
        i_col = g[:, h:h + 1]
        f_col = _log_sigmoid(g[:, M_HEADS + h:M_HEADS + h + 1])
        i_row = gt[h:h + 1, :]
        f_row = _log_sigmoid(gt[M_HEADS + h:M_HEADS + h + 1, :])
        q = q_ref[rows, h * M_DQK:(h + 1) * M_DQK]
        k = k_ref[rows, h * M_DQK:(h + 1) * M_DQK]
        v = v_ref[rows, h * M_DV:(h + 1) * M_DV]
        zo = zo_ref[rows, h * M_DV:(h + 1) * M_DV].astype(F32)
        C = c_ref[0, h]
        n = n_ref[0, h:h + 1, :]
        m = m_ref[0, h:h + 1, 0:1]
        b_col = jnp.sum(jnp.where(tri, f_row, 0.0), axis=1, keepdims=True)
        b_row = jnp.sum(jnp.where(row <= col, f_col, 0.0), axis=0, keepdims=True)
        logd = jnp.where(tri, b_col - b_row + i_row, -jnp.inf)
        inter = b_col + m
        mt = jnp.maximum(inter, jnp.max(logd, axis=1, keepdims=True))
        sd = _dot_nt(q, k) * scale * jnp.exp(logd - mt)
        sc = jnp.exp(inter - mt)
        num = _dot(sd.astype(BF16), v) + sc * _dot(q, C.astype(BF16))
        qn = _dot_nt(q, jnp.broadcast_to(n, (16, M_DQK)).astype(BF16))[:, 0:1]
        den = jnp.sum(sd, axis=1, keepdims=True) + sc * qn
        hh = num / jnp.maximum(jnp.abs(den), jnp.exp(-mt))
        hn = hh * lax.rsqrt(jnp.mean(hh * hh, axis=1, keepdims=True) + HEAD_EPS)
        out = hn * gm_ref[:, h * M_DV:(h + 1) * M_DV] * _sigmoid(zo)
        hm_ref[rows, h * M_DV:(h + 1) * M_DV] = out.astype(hm_ref.dtype)
        bl = b_col[L - 1:L, :]
        ml = mt[L - 1:L, :]
        w_col = jnp.exp(bl - b_col + i_col - ml)
        scl = jnp.exp(bl + m - ml)
        kw = k.astype(F32) * (w_col * scale)
        c_ref[0, h] = scl * C + _dot_tn(kw.astype(BF16), v)
        n_ref[0, h:h + 1, :] = scl * n + jnp.sum(kw, axis=0, keepdims=True)
        m_ref[0, h:h + 1, :] = jnp.broadcast_to(ml, (1, LANES))


def _mlstm(zm, zg, gate_bias, g_mnorm, c0, n0, m0, B, nc, sub=1, L=128):
    T = B * nc * L
    assert zm.shape == (T, 2 * MIX_QK + 2 * MIX_V) and nc % sub == 0
    nc, L_chunk, L = nc // sub, L, L * sub
    row_blk = lambda b, c: b * nc + c
    state_specs = [pl.BlockSpec((1, M_HEADS, M_DQK, M_DV), lambda b, c: (b, 0, 0, 0)),
                   pl.BlockSpec((1, M_HEADS, M_DQK), lambda b, c: (b, 0, 0)),
                   pl.BlockSpec((1, M_HEADS, LANES), lambda b, c: (b, 0, 0))]
    return pl.pallas_call(
        functools.partial(_mlstm_body, L=L_chunk, sub=sub),
        grid=(B, nc),
        in_specs=[pl.BlockSpec((L, MIX_QK), lambda b, c: (row_blk(b, c), 0)),
                  pl.BlockSpec((L, MIX_QK), lambda b, c: (row_blk(b, c), 1)),
                  pl.BlockSpec((L, MIX_V), lambda b, c: (row_blk(b, c), 1)),
                  pl.BlockSpec((L, MIX_V), lambda b, c: (row_blk(b, c), 2)),
                  pl.BlockSpec((L, LANES), lambda b, c: (row_blk(b, c), 0)),
                  pl.BlockSpec((1, LANES), lambda b, c: (0, 0)),
                  pl.BlockSpec((1, MIX_V), lambda b, c: (0, 0))] + state_specs,
        out_specs=[pl.BlockSpec((L, MIX_V), lambda b, c: (row_blk(b, c), 0))] + state_specs,
        out_shape=[jax.ShapeDtypeStruct((T, MIX_V), BF16),
                   jax.ShapeDtypeStruct((B, M_HEADS, M_DQK, M_DV), F32),
                   jax.ShapeDtypeStruct((B, M_HEADS, M_DQK), F32),
                   jax.ShapeDtypeStruct((B, M_HEADS, LANES), F32)],
        compiler_params=_cparams(2),
        name="mlstm",
    )(zm, zm, zm, zm, zg, gate_bias, g_mnorm, c0, n0, m0)


DIL_BLOCKS_PER_STEP = 4


def _dil_body(q_ref, kc_ref, kp_ref, vc_ref, vp_ref, o_ref, l_ref, *, span):
    scale = A_DH ** -0.5
    n = pl.program_id(1)
    row = lax.broadcasted_iota(I32, (BAND_BLOCK, BAND_BLOCK), 0)
    col = lax.broadcasted_iota(I32, (BAND_BLOCK, BAND_BLOCK), 1)
    first = jnp.where(n > 0, 0, 2 * BAND_BLOCK)
    mask_c = row >= col
    lane = lax.broadcasted_iota(I32, (BAND_BLOCK, LANES), 1)
    ones = jnp.ones((BAND_BLOCK, A_DH), BF16)
    for j in range(DIL_BLOCKS_PER_STEP):
        rows = slice(j * BAND_BLOCK, (j + 1) * BAND_BLOCK)
        prev = slice((j - 1) * BAND_BLOCK, j * BAND_BLOCK)
        mask_p = (BAND_BLOCK + row - col + (first if j == 0 else 0)) <= span
        lse_tile = jnp.zeros((BAND_BLOCK, LANES), F32)
        for h in range(A_HEADS):
            sl = slice(h * A_DH, (h + 1) * A_DH)
            q = q_ref[rows, sl]
            kp = kp_ref[:, sl] if j == 0 else kc_ref[prev, sl]
            vp = vp_ref[:, sl] if j == 0 else vc_ref[prev, sl]
            sp = jnp.where(mask_p, _dot_nt(q, kp) * scale, -jnp.inf)
            sc = jnp.where(mask_c, _dot_nt(q, kc_ref[rows, sl]) * scale, -jnp.inf)
            m = jnp.max(jnp.maximum(sp, sc), axis=1, keepdims=True)
            pp = jnp.exp(sp - m).astype(BF16)
            pc = jnp.exp(sc - m).astype(BF16)
            oa = (_dot(pp, jnp.concatenate([vp, ones], axis=1))
                  + _dot(pc, jnp.concatenate([vc_ref[rows, sl], ones], axis=1)))
            l = oa[:, A_DH:A_DH + 1]
            o_ref[rows, sl] = (oa[:, :A_DH] / l).astype(o_ref.dtype)
            lse_tile = jnp.where(lane == h, m + jnp.log(l), lse_tile)
        l_ref[rows, :] = lse_tile


PROJ_ATTN_ROWS = 512


def _proj_attn_body(x_ref, w_ref, *refs, dils):
    o_refs, tail_ref, acc = refs[:len(dils)], refs[len(dils)], refs[len(dils) + 1]
    tm = x_ref.shape[0]
    nslab = acc.shape[0]
    tn = nslab * LANES
    xb = x_ref[...].astype(BF16)
    for j in range(w_ref.shape[1] // tn):
        z = _dot(xb, w_ref[:, j * tn:(j + 1) * tn])
        if j * tn >= MIX_A:
            tail_ref[:, j * tn - MIX_A:(j + 1) * tn - MIX_A] = z
        for dil, o_ref in zip(dils, o_refs):
            if dil == 1:
                o_ref[:, j * tn:(j + 1) * tn] = z.astype(o_ref.dtype)
        for s in range(nslab):
            acc[s] = z[:, s * LANES:(s + 1) * LANES]
        for dil, o_ref in zip(dils, o_refs):
            if dil == 1:
                continue
            rows = tm // dil
            for r in range(dil):
                for s in range(nslab):
                    c0 = r * w_ref.shape[1] + j * tn + s * LANES
                    o_ref[:, c0:c0 + LANES] = acc[s, pl.ds(r, rows, stride=dil), :].astype(o_ref.dtype)


def _proj_attn(x, w, dils, tail_rows):
    T, K = x.shape
    N = w.shape[1]
    tm, tn = PROJ_ATTN_ROWS, 1024
    assert T % tm == 0 and N % tn == 0 and all(tm % (16 * d) == 0 for d in dils)
    assert tail_rows % tm == 0 and MIX_A % tn == 0 and N == 3 * MIX_A
    first_tail = (T - tail_rows) // tm
    return pl.pallas_call(
        functools.partial(_proj_attn_body, dils=dils),
        grid=(T // tm,),
        in_specs=[pl.BlockSpec((tm, K), lambda i: (i, 0)), _resident(w.shape)],
        out_specs=[pl.BlockSpec((tm // d, d * N), lambda i: (i, 0)) for d in dils]
        + [pl.BlockSpec((tm, 2 * MIX_A), lambda i: (jnp.maximum(i - first_tail, 0), 0))],
        out_shape=[jax.ShapeDtypeStruct((T // d, d * N), BF16) for d in dils]
        + [jax.ShapeDtypeStruct((tail_rows, 2 * MIX_A), F32)],
        scratch_shapes=[pltpu.VMEM((tn // LANES, tm, LANES), F32)],
        compiler_params=_cparams(1),
        name="proj_attn",
    )(x, w)


def _dilated_pattern(zv, window, dil):
    N = zv.shape[0]
    span = window // dil
    step_rows = DIL_BLOCKS_PER_STEP * BAND_BLOCK
    assert zv.shape[1] == dil * 3 * MIX_A and N % step_rows == 0 and BAND_BLOCK - 1 <= span
    nb = N // step_rows
    prev = lambda n: jnp.maximum(n * DIL_BLOCKS_PER_STEP - 1, 0)
    blk = (step_rows, MIX_A)
    pblk = (BAND_BLOCK, MIX_A)
    o, lse = pl.pallas_call(
        functools.partial(_dil_body, span=span),
        grid=(dil, nb),
        in_specs=[pl.BlockSpec(blk, lambda r, n: (n, 3 * r)),
                  pl.BlockSpec(blk, lambda r, n: (n, 3 * r + 1)),
                  pl.BlockSpec(pblk, lambda r, n: (prev(n), 3 * r + 1)),
                  pl.BlockSpec(blk, lambda r, n: (n, 3 * r + 2)),
                  pl.BlockSpec(pblk, lambda r, n: (prev(n), 3 * r + 2))],
        out_specs=[pl.BlockSpec(blk, lambda r, n: (n, r)),
                   pl.BlockSpec((step_rows, LANES), lambda r, n: (n, r))],
        out_shape=[jax.ShapeDtypeStruct((N, dil * MIX_A), BF16),
                   jax.ShapeDtypeStruct((N, dil * LANES), F32)],
        compiler_params=_cparams(2),
        name=f"dilated_d{dil}",
    )(zv, zv, zv, zv, zv)
    return o, lse


SWA_ROWS = 16


def _swa_body(q_ref, kn_ref, vn_ref, pos_ref, kr_ref, vr_ref, k_hbm, v_hbm, o_ref, oldk, oldv, sem, *,
              wb, nb, n_new, n_old, big_dil, dense):
    scale = A_DH ** -0.5
    b = pl.program_id(0)
    slot = lax.rem(b, 2)
    grp = n_new * A_HEADS

    def old_copies(bi, sl):
        cps = []
        for a in range(n_old):
            src = pl.ds(a * big_dil * A_HEADS, grp)
            dst = pl.ds(sl * n_old * grp + a * grp, grp)
            cps.append(pltpu.make_async_copy(k_hbm.at[bi, src, :], oldk.at[dst, :], sem.at[sl]))
            cps.append(pltpu.make_async_copy(v_hbm.at[bi, src, :], oldv.at[dst, :], sem.at[sl]))
        return cps

    @pl.when(b == 0)
    def _():
        for cp in old_copies(0, 0):
            cp.start()

    for cp in old_copies(b, slot):
        cp.wait()
    nxt = jnp.minimum(b + 1, nb - 1)
    for cp in old_copies(nxt, 1 - slot):
        cp.start()

    ncols = n_new * n_old + dense
    s_k = lax.broadcasted_iota(I32, (SWA_ROWS, ncols), 0)
    d_k = wb + s_k - pos_ref[...]
    s_n = lax.broadcasted_iota(I32, (SWA_ROWS, SWA_ROWS), 0)
    p_n = lax.broadcasted_iota(I32, (SWA_ROWS, SWA_ROWS), 1)
    d_n = s_n - p_n
    masks = []
    for window, dil in PATTERNS:
        assert dil & (dil - 1) == 0 and window % dil == 0
        mk = jnp.where(jnp.bitwise_and(d_k, dil - 1) == 0, d_k, window + 1) <= window
        mn = jnp.where(jnp.bitwise_and(d_n, dil - 1) == 0, jnp.where(d_n >= 0, d_n, window + 1), window + 1) <= window
        masks.append((mk, mn))
    for h in range(A_HEADS):
        sl = slice(h * A_DH, (h + 1) * A_DH)
        q = q_ref[0, :, sl]

        def head_rows(old, rec_ref):
            parts = [old[pl.ds(slot * n_old * grp + s * A_HEADS + h, n_old, stride=grp), :] for s in range(n_new)]
            parts.append(rec_ref[0, pl.ds(h, dense, stride=A_HEADS), :])
            return jnp.concatenate(parts, axis=0).astype(BF16)

        kb = head_rows(oldk, kr_ref)
        vb = head_rows(oldv, vr_ref)
        kn = kn_ref[0, :, sl]
        vn = vn_ref[0, :, sl]
        s_cache = _dot_nt(q, kb) * scale
        s_new = _dot_nt(q, kn) * scale
        ps, lses = [], []
        for mk, mn in masks:
            sk = jnp.where(mk, s_cache, -jnp.inf)
            sn = jnp.where(mn, s_new, -jnp.inf)
            m = jnp.maximum(jnp.max(sk, axis=1, keepdims=True), jnp.max(sn, axis=1, keepdims=True))
            pk = jnp.exp(sk - m)
            pn = jnp.exp(sn - m)
            l = jnp.sum(pk, axis=1, keepdims=True) + jnp.sum(pn, axis=1, keepdims=True)
            ps.append((pk, pn, l))
            lses.append(m + jnp.log(l))
        top = functools.reduce(jnp.maximum, lses)
        es = [jnp.exp(x - top) for x in lses]
        tot = functools.reduce(lambda a, b: a + b, es)
        pk_all = None
        pn_all = None
        for (pk, pn, l), e in zip(ps, es):
            coef = e / (tot * l)
            pk_all = pk * coef if pk_all is None else pk_all + pk * coef
            pn_all = pn * coef if pn_all is None else pn_all + pn * coef
        o = _dot(pk_all.astype(BF16), vb) + _dot(pn_all.astype(BF16), vn)
        o_ref[0, :, sl] = o.astype(o_ref.dtype)

    @pl.when(b == nb - 1)
    def _():
        for cp in old_copies(nxt, 1 - slot):
            cp.wait()


def _swa_sample(za_pad, kbuf, vbuf, n_new):
    B, wb = kbuf.shape[0], kbuf.shape[1] // A_HEADS
    big_dil = PATTERNS[-1][1]
    dense = max(w for w, _ in PATTERNS[:-1])
    assert all(w <= PATTERNS[-1][0] and d <= big_dil for w, d in PATTERNS[:-1])
    assert wb % dense == 0 and dense % big_dil == 0 and wb % big_dil == 0 and n_new <= big_dil
    n_old = (wb - dense) // big_dil
    assert n_old % 8 == 0
    grp = n_new * A_HEADS
    old_pos = (jnp.arange(n_old, dtype=I32)[None, :] * big_dil + jnp.arange(n_new, dtype=I32)[:, None]).reshape(-1)
    pos = jnp.concatenate([old_pos, wb - dense + jnp.arange(dense, dtype=I32)]).reshape(1, -1)
    new_blk = (1, SWA_ROWS, MIX_A)
    rec_blk = (1, dense * A_HEADS, A_DH)
    last = wb // dense - 1
    return pl.pallas_call(
        functools.partial(_swa_body, wb=wb, nb=B, n_new=n_new, n_old=n_old, big_dil=big_dil, dense=dense),
        grid=(B,),
        in_specs=[pl.BlockSpec(new_blk, lambda b: (b, 0, 0)),
                  pl.BlockSpec(new_blk, lambda b: (b, 0, 1)),
                  pl.BlockSpec(new_blk, lambda b: (b, 0, 2)),
                  _resident(pos.shape),
                  pl.BlockSpec(rec_blk, lambda b: (b, last, 0)),
                  pl.BlockSpec(rec_blk, lambda b: (b, last, 0)),
                  pl.BlockSpec(memory_space=pl.ANY), pl.BlockSpec(memory_space=pl.ANY)],
        out_specs=pl.BlockSpec(new_blk, lambda b: (b, 0, 0)),
        out_shape=jax.ShapeDtypeStruct((B, SWA_ROWS, MIX_A), BF16),
        scratch_shapes=[pltpu.VMEM((2 * n_old * grp, A_DH), F32), pltpu.VMEM((2 * n_old * grp, A_DH), F32),
                        pltpu.SemaphoreType.DMA((2,))],
        compiler_params=_cparams(1),
        name="swa_sample",
    )(za_pad, za_pad, za_pad, pos, kbuf, vbuf, kbuf, vbuf)


def _mix_body(*refs, dils):
    n_pat = len(dils)
    x_ref, hm_ref = refs[0], refs[1]
    n_lse = n_pat if n_pat > 1 else 0
    o_refs = refs[2:2 + n_pat]
    l_refs = refs[2 + n_pat:2 + n_pat + n_lse]
    wmo_ref, g_ref, b_ref, wq_ref, x1_ref, q_ref = refs[2 + n_pat + n_lse:8 + n_pat + n_lse]
    if n_pat == 1:
        ha = o_refs[0][...]
    else:
        o_nat, l_nat = refs[8 + n_pat + n_lse:]
        tm = x_ref.shape[0]
        for p, dil in enumerate(dils):
            if dil == 1:
                continue
            rows = tm // dil
            for r in range(dil):
                dst = pl.ds(r, rows, stride=dil)
                l_nat[p, dst, :] = l_refs[p][:, r * LANES:(r + 1) * LANES]
                for h in range(A_HEADS):
                    c0 = r * MIX_A + h * A_DH
                    o_nat[p, h, dst, :] = o_refs[p][:, c0:c0 + A_DH].astype(F32)

        def lse_of(p, h):
            return l_refs[p][:, h:h + 1] if dils[p] == 1 else l_nat[p, :, h:h + 1]

        def out_of(p, h):
            return o_refs[p][:, h * A_DH:(h + 1) * A_DH].astype(F32) if dils[p] == 1 else o_nat[p, h]

        cols = []
        for h in range(A_HEADS):
            ls = [lse_of(p, h) for p in range(n_pat)]
            top = functools.reduce(jnp.maximum, ls)
            es = [jnp.exp(x - top) for x in ls]
            inv = 1.0 / functools.reduce(lambda a, b: a + b, es)
            acc = None
            for p, e in enumerate(es):
                term = out_of(p, h) * (e * inv)
                acc = term if acc is None else acc + term
            cols.append(acc.astype(BF16))
        ha = jnp.concatenate(cols, axis=1)
    mix = _dot(hm_ref[...], wmo_ref[0:MIX_V, :]) + _dot(ha, wmo_ref[MIX_V:MIX_V + MIX_A, :])
    x1 = _layer_norm(DN_ALPHA * x_ref[...] + mix, g_ref[...], b_ref[...])
    x1_ref[...] = x1
    q_ref[...] = _dot(x1.astype(BF16), wq_ref[...]).astype(q_ref.dtype)


def _mix_ln1_q(x, hm, outs, lses, dils, wmo, ln_g, ln_b, wq, tm):
    M = x.shape[0]
    n_pat = len(outs)
    assert len(lses) == (n_pat if n_pat > 1 else 0) and len(dils) == n_pat
    assert all(tm % (16 * d) == 0 for d in dils)
    row = lambda w: pl.BlockSpec((tm, w), lambda i: (i, 0))
    view = lambda w, d: pl.BlockSpec((tm // d, d * w), lambda i: (i, 0))
    scratch = []
    if n_pat > 1:
        scratch = [pltpu.VMEM((n_pat, A_HEADS, tm, A_DH), F32), pltpu.VMEM((n_pat, tm, LANES), F32)]
    return pl.pallas_call(
        functools.partial(_mix_body, dils=tuple(dils)),
        grid=(M // tm,),
        in_specs=[row(D_MODEL), row(MIX_V)] + [view(MIX_A, d) for d in dils]
        + [view(LANES, d) for d in dils[:len(lses)]]
        + [_resident(wmo.shape), _resident(ln_g.shape), _resident(ln_b.shape), _resident(wq.shape)],
        out_specs=[row(D_MODEL), row(D_MODEL)],
        out_shape=[jax.ShapeDtypeStruct((M, D_MODEL), F32), jax.ShapeDtypeStruct((M, D_MODEL), BF16)],
        scratch_shapes=scratch,
        compiler_params=_cparams(1),
        name="mix_ln1_q",
    )(x, hm, *outs, *lses, wmo, ln_g, ln_b, wq)


def _xattn_body(q_ref, k_ref, v_ref, o_ref):
    for h in range(X_HEADS):
        sl = slice(h * X_DH, (h + 1) * X_DH)
        o_ref[0, :, sl] = _xattn_head(q_ref[0, :, sl], k_ref[0, :, sl], v_ref[0, :, sl]).astype(o_ref.dtype)


def _xattn_head(q, k, v):
    s = _dot_nt(q, k.astype(BF16)) * (X_DH ** -0.5)
    p = jnp.exp(s - jnp.max(s, axis=1, keepdims=True))
    p = p / jnp.sum(p, axis=1, keepdims=True)
    return _dot(p.astype(BF16), v.astype(BF16))


def _xattn_cache_body(q_ref, k_hbm, v_hbm, o_ref, kbuf, vbuf, sem, *, nb):
    b = pl.program_id(0)
    slot = lax.rem(b, 2)

    def head_copies(bi, sl):
        cps = []
        for h in range(X_HEADS):
            cps.append(pltpu.make_async_copy(k_hbm.at[0, bi, :, h, :], kbuf.at[sl, h], sem.at[sl]))
            cps.append(pltpu.make_async_copy(v_hbm.at[0, bi, :, h, :], vbuf.at[sl, h], sem.at[sl]))
        return cps

    @pl.when(b == 0)
    def _():
        for cp in head_copies(0, 0):
            cp.start()

    for cp in head_copies(b, slot):
        cp.wait()
    nxt = jnp.minimum(b + 1, nb - 1)
    for cp in head_copies(nxt, 1 - slot):
        cp.start()
    for h in range(X_HEADS):
        sl = slice(h * X_DH, (h + 1) * X_DH)
        o_ref[0, :, sl] = _xattn_head(q_ref[0, :, sl], kbuf[slot, h], vbuf[slot, h]).astype(o_ref.dtype)

    @pl.when(b == nb - 1)
    def _():
        for cp in head_copies(nxt, 1 - slot):
            cp.wait()


def _xattn_cache(q, ck, cv):
    B, Tq, _ = q.shape
    nm = ck.shape[2]
    return pl.pallas_call(
        functools.partial(_xattn_cache_body, nb=B),
        grid=(B,),
        in_specs=[pl.BlockSpec((1, Tq, D_MODEL), lambda b: (b, 0, 0)),
                  pl.BlockSpec(memory_space=pl.ANY), pl.BlockSpec(memory_space=pl.ANY)],
        out_specs=pl.BlockSpec((1, Tq, D_MODEL), lambda b: (b, 0, 0)),
        out_shape=jax.ShapeDtypeStruct((B, Tq, D_MODEL), BF16),
        scratch_shapes=[pltpu.VMEM((2, X_HEADS, nm, X_DH), F32), pltpu.VMEM((2, X_HEADS, nm, X_DH), F32),
                        pltpu.SemaphoreType.DMA((2,))],
        compiler_params=_cparams(1),
        name="xattn_cache",
    )(q, ck, cv)


def _xattn(q, mk, mv, tq):
    B, Tq, _ = q.shape
    nm = mk.shape[1]
    mem_blk = (1, nm, D_MODEL)
    return pl.pallas_call(
        _xattn_body,
        grid=(B, Tq // tq),
        in_specs=[pl.BlockSpec((1, tq, D_MODEL), lambda b, i: (b, i, 0)),
                  pl.BlockSpec(mem_blk, lambda b, i: (b, 0, 0)),
                  pl.BlockSpec(mem_blk, lambda b, i: (b, 0, 0))],
        out_specs=pl.BlockSpec((1, tq, D_MODEL), lambda b, i: (b, i, 0)),
        out_shape=jax.ShapeDtypeStruct((B, Tq, D_MODEL), BF16),
        compiler_params=_cparams(2),
        name="xattn",
    )(q, mk, mv)


PACK_SLABS = D_MODEL // (2 * LANES)


def _pack_bf16_pairs(lo, hi):
    lo_bits = pltpu.bitcast(lo.astype(BF16).astype(F32), jnp.uint32)
    hi_bits = pltpu.bitcast(hi.astype(BF16).astype(F32), jnp.uint32)
    return jnp.bitwise_or(jnp.bitwise_and(hi_bits, jnp.uint32(0xFFFF0000)), lax.shift_right_logical(lo_bits, jnp.uint32(16)))


def _unpack_bf16_pairs(w):
    lo = pltpu.bitcast(lax.shift_left(w, jnp.uint32(16)), F32).astype(BF16)
    hi = pltpu.bitcast(jnp.bitwise_and(w, jnp.uint32(0xFFFF0000)), F32).astype(BF16)
    return lo, hi


def _xo_body(x1_ref, o_ref, wo_ref, g_ref, b_ref, wr_ref, br_ref, *rest, tail_rows, nsteps):
    if not tail_rows:
        _xo_rows(x1_ref, o_ref, wo_ref, g_ref, b_ref, wr_ref, br_ref, *rest)
        return
    tx2_ref, txp_ref, x2_ref, xp_ref, ti_ref, tg_ref = rest
    i = pl.program_id(0)

    @pl.when(i < nsteps)
    def _():
        _xo_rows(x1_ref, o_ref, wo_ref, g_ref, b_ref, wr_ref, br_ref, x2_ref, xp_ref, ti_ref, tg_ref)

    @pl.when(i == nsteps)
    def _():
        tm = x2_ref.shape[0]
        x2_ref[0:tail_rows, :] = tx2_ref[...]
        x2_ref[tail_rows:, :] = jnp.zeros((tm - tail_rows, D_MODEL), F32)
        xp_ref[0:tail_rows * PACK_SLABS, :] = txp_ref[...]
        xp_ref[tail_rows * PACK_SLABS:, :] = jnp.zeros(((tm - tail_rows) * PACK_SLABS, LANES), jnp.uint32)


def _xo_rows(x1_ref, o_ref, wo_ref, g_ref, b_ref, wr_ref, br_ref, x2_ref, xp_ref, ti_ref, tg_ref):
    y = _dot(o_ref[...], wo_ref[...])
    x2 = _layer_norm(DN_ALPHA * x1_ref[...] + y, g_ref[...], b_ref[...])
    x2_ref[...] = x2
    tm = x2.shape[0]
    for s in range(PACK_SLABS):
        lo = x2[:, s * LANES:(s + 1) * LANES]
        hi = x2[:, D_MODEL // 2 + s * LANES:D_MODEL // 2 + (s + 1) * LANES]
        xp_ref[pl.ds(s, tm, stride=PACK_SLABS), :] = _pack_bf16_pairs(lo, hi)
    lane = lax.broadcasted_iota(I32, (tm, LANES), 1)
    lanef = lane.astype(F32)
    logits = jnp.where(lane < N_EXPERTS, _dot(x2.astype(BF16), wr_ref[...]) + br_ref[...], -jnp.inf)
    vals, idxs = [], []
    cur = logits
    for _ in range(TOP_K):
        top = jnp.max(cur, axis=1, keepdims=True)
        idx = jnp.min(jnp.where(cur == top, lanef, float(LANES)), axis=1, keepdims=True)
        vals.append(top)
        idxs.append(idx)
        cur = jnp.where(lanef == idx, -jnp.inf, cur)
    es = [jnp.exp(v - vals[0]) for v in vals]
    inv = 1.0 / functools.reduce(lambda a, b: a + b, es)
    ti = jnp.zeros((tm, LANES), F32)
    tg = jnp.zeros((tm, LANES), F32)
    for k in range(TOP_K):
        ti = jnp.where(lane == k, idxs[k], ti)
        tg = jnp.where(lane == k, es[k] * inv, tg)
    ti_ref[...] = ti.astype(I32)
    tg_ref[...] = tg


def _xo_ln2_router(x1, o, wo, ln_g, ln_b, wr, br, tm, tail=None):
    M = x1.shape[0]
    nsteps = M // tm
    tail_rows = 0 if tail is None else tail[0].shape[0]
    assert tail_rows <= tm
    extra = 1 if tail_rows else 0
    last = nsteps - 1
    row_in = lambda w: pl.BlockSpec((tm, w), lambda i: (jnp.minimum(i, last), 0))
    tail_specs = [] if tail is None else [_resident(tail[0].shape), _resident(tail[1].shape)]
    return pl.pallas_call(
        functools.partial(_xo_body, tail_rows=tail_rows, nsteps=nsteps),
        grid=(nsteps + extra,),
        in_specs=[row_in(D_MODEL), row_in(D_MODEL), _resident(wo.shape), _resident(ln_g.shape),
                  _resident(ln_b.shape), _resident(wr.shape), _resident(br.shape)] + tail_specs,
        out_specs=[pl.BlockSpec((tm, D_MODEL), lambda i: (i, 0)),
                   pl.BlockSpec((tm * PACK_SLABS, LANES), lambda i: (i, 0)), row_in(LANES), row_in(LANES)],
        out_shape=[jax.ShapeDtypeStruct((M + extra * tm, D_MODEL), F32),
                   jax.ShapeDtypeStruct(((M + extra * tm) * PACK_SLABS, LANES), jnp.uint32),
                   jax.ShapeDtypeStruct((M, LANES), I32), jax.ShapeDtypeStruct((M, LANES), F32)],
        compiler_params=_cparams(1),
        name="xo_ln2_router",
    )(x1, o, wo, ln_g, ln_b, wr, br, *([] if tail is None else list(tail)))


MOE_SUB_ROWS = 256
MOE_FF_CHUNK = 256


def _ffn_body(te_ref, tr_ref, nu_ref, idc_ref, idn_ref, x_hbm, wg_ref, bg_ref, wu_ref, bu_ref, wd_ref, bd_ref,
              o_ref, stg, sem, xbb, *, tm, sb, nch):
    t = pl.program_id(0)
    c = pl.program_id(1)
    n_used = nu_ref[0]
    ps = tm // nch
    gp = ps * PACK_SLABS
    cur = lax.rem(t, 2)
    tp = tm * PACK_SLABS

    def row_copy(src_row, sl, g, r):
        src_row = pl.multiple_of(src_row, PACK_SLABS)
        dst = pl.ds(sl * tp + g * gp + r * PACK_SLABS, PACK_SLABS)
        return pltpu.make_async_copy(x_hbm.at[pl.ds(src_row, PACK_SLABS), :], stg.at[dst, :], sem.at[sl * nch + g])

    def group_wait(sl, g):
        pltpu.make_async_copy(x_hbm.at[pl.ds(0, gp), :], stg.at[pl.ds(sl * tp + g * gp, gp), :],
                              sem.at[sl * nch + g]).wait()

    def unpack_group(sl, g):
        rows_g = slice(g * ps, (g + 1) * ps)
        for s in range(PACK_SLABS):
            lo, hi = _unpack_bf16_pairs(stg[pl.ds(sl * tp + g * gp + s, ps, stride=PACK_SLABS), :])
            xbb[rows_g, s * LANES:(s + 1) * LANES] = lo
            xbb[rows_g, D_MODEL // 2 + s * LANES:D_MODEL // 2 + (s + 1) * LANES] = hi

    def ffn_rows(rs, wg, wu, wd):
        xb = xbb[rs, :]
        g = jnp.minimum(_dot(xb, wg) + bg_ref[0], SWIGLU_LIMIT)
        u = jnp.clip(_dot(xb, wu) + bu_ref[0], -SWIGLU_LIMIT, SWIGLU_LIMIT)
        hid = (u + 1.0) * (g * _sigmoid(SWIGLU_ALPHA * g))
        o_ref[rs, :] += _dot(hid.astype(BF16), wd)

    @pl.when(t >= n_used)
    def _():
        @pl.when(c == 0)
        def _():
            o_ref[...] = jnp.zeros((tm, D_MODEL), F32)

    @pl.when(t < n_used)
    def _():
        rows = tr_ref[t]
        rows_next = jnp.where(t + 1 < n_used, tr_ref[jnp.minimum(t + 1, n_used - 1)], 0)

        @pl.when((t == 0) & (c == 0))
        def _():
            xbb[...] = jnp.zeros((tm, D_MODEL), BF16)
            for g in range(nch):
                @pl.when(g * ps < rows)
                def _():
                    def issue(r, carry):
                        row_copy(idc_ref[0, 0, g * ps + r], 0, g, r).start()
                        return carry
                    lax.fori_loop(0, ps, issue, 0, unroll=8)

        @pl.when(c == 0)
        def _():
            for g in range(nch):
                @pl.when(g * ps < rows)
                def _():
                    group_wait(cur, g)
                    unpack_group(cur, g)

        @pl.when(c * ps < rows_next)
        def _():
            for r in range(ps):
                row_copy(idn_ref[0, 0, c * ps + r], 1 - cur, c, r).start()

        for nb in range(1, tm // sb + 1):
            @pl.when((rows > (nb - 1) * sb) & (rows <= nb * sb))
            def _():
                @pl.when(c == 0)
                def _():
                    o_ref[0:nb * sb, :] = jnp.broadcast_to(bd_ref[0], (nb * sb, D_MODEL))
                    if nb * sb < tm:
                        o_ref[nb * sb:, :] = jnp.zeros((tm - nb * sb, D_MODEL), F32)
                ffn_rows(slice(0, nb * sb), wg_ref[0, 0].astype(BF16), wu_ref[0, 0].astype(BF16),
                         wd_ref[0, 0].astype(BF16))


def _moe_ffn(x, row_tok, tile_e, tile_rows, n_used, wg, bg, wu, bu, wd, bd, tm, sb):
    P = row_tok.shape[0]
    n_tiles = P // tm
    nch = D_FF // MOE_FF_CHUNK
    fc = MOE_FF_CHUNK
    assert tm % nch == 0 and (tm // nch) % 16 == 0 and nch >= 3

    def tile(t, nu):
        return jnp.minimum(t, jnp.maximum(nu[0] - 1, 0))

    def chunk(t, c, nu):
        return jnp.where(t < nu[0], c, nch - 1)

    idx_blk = (1, 1, tm)
    return pl.pallas_call(
        functools.partial(_ffn_body, tm=tm, sb=sb, nch=nch),
        grid_spec=pltpu.PrefetchScalarGridSpec(
            num_scalar_prefetch=3,
            grid=(n_tiles, nch),
            in_specs=[
                pl.BlockSpec(idx_blk, lambda t, c, te, tr, nu: (tile(t, nu), 0, 0), memory_space=pltpu.SMEM),
                pl.BlockSpec(idx_blk, lambda t, c, te, tr, nu: (tile(t + 1, nu), 0, 0), memory_space=pltpu.SMEM),
                pl.BlockSpec(memory_space=pl.ANY),
                pl.BlockSpec((1, 1, D_MODEL, fc), lambda t, c, te, tr, nu: (0, te[tile(t, nu)], 0, chunk(t, c, nu))),
                pl.BlockSpec((1, 1, fc), lambda t, c, te, tr, nu: (te[tile(t, nu)], 0, chunk(t, c, nu))),
                pl.BlockSpec((1, 1, D_MODEL, fc), lambda t, c, te, tr, nu: (0, te[tile(t, nu)], 0, chunk(t, c, nu))),
                pl.BlockSpec((1, 1, fc), lambda t, c, te, tr, nu: (te[tile(t, nu)], 0, chunk(t, c, nu))),
                pl.BlockSpec((1, 1, fc, D_MODEL), lambda t, c, te, tr, nu: (0, te[tile(t, nu)], chunk(t, c, nu), 0)),
                pl.BlockSpec((1, 1, D_MODEL), lambda t, c, te, tr, nu: (te[tile(t, nu)], 0, 0)),
            ],
            out_specs=pl.BlockSpec((tm, D_MODEL), lambda t, c, te, tr, nu: (t, 0)),
            scratch_shapes=[pltpu.VMEM((2 * tm * PACK_SLABS, LANES), jnp.uint32),
                            pltpu.SemaphoreType.DMA((2 * nch,)),
                            pltpu.VMEM((tm, D_MODEL), BF16)]),
        out_shape=jax.ShapeDtypeStruct((P, D_MODEL), F32),
        compiler_params=_cparams(2),
        name="moe_ffn",
    )(tile_e, tile_rows, n_used, row_tok.reshape(n_tiles, 1, tm), row_tok.reshape(n_tiles, 1, tm), x,
      wg, bg, wu, bu, wd, bd)


def _combine_body(posc_ref, posn_ref, x2_ref, tg_ref, ys_hbm, g_ref, b_ref, oa_ref, ob_ref, buf, sem, *,
                  tc, nblk, nblk_a):
    i = pl.program_id(0)
    slot = lax.rem(i, 2)

    def row_copy(src_row, sl, k, j):
        return pltpu.make_async_copy(ys_hbm.at[pl.ds(src_row, 1), :], buf.at[sl, k, pl.ds(j, 1), :], sem.at[sl])

    def slot_wait(sl):
        pltpu.make_async_copy(buf.at[sl], buf.at[sl], sem.at[sl]).wait()

    @pl.when(i == 0)
    def _():
        def issue(j, carry):
            for k in range(TOP_K):
                row_copy(posc_ref[0, 0, j * TOP_K + k], 0, k, j).start()
            return carry
        lax.fori_loop(0, tc, issue, 0, unroll=4)

    slot_wait(slot)
    for j in range(tc):
        for k in range(TOP_K):
            row_copy(posn_ref[0, 0, j * TOP_K + k], 1 - slot, k, j).start()
    y = None
    for k in range(TOP_K):
        term = buf[slot, k] * tg_ref[:, k:k + 1]
        y = term if y is None else y + term
    out = _layer_norm(DN_ALPHA * x2_ref[...] + y, g_ref[...], b_ref[...])

    @pl.when(i < nblk_a)
    def _():
        oa_ref[...] = out

    @pl.when(i >= nblk_a)
    def _():
        ob_ref[...] = out

    @pl.when(i == nblk - 1)
    def _():
        slot_wait(1 - slot)


def _combine_ln3(x2, tg, ys, pos, ln_g, ln_b, tc, split):
    M = pos.shape[0] // TOP_K
    nblk = M // tc
    nblk_a = split // tc
    assert split % tc == 0 and 0 < nblk_a < nblk
    pos3 = pos.reshape(nblk, 1, tc * TOP_K)
    return pl.pallas_call(
        functools.partial(_combine_body, tc=tc, nblk=nblk, nblk_a=nblk_a),
        grid=(nblk,),
        in_specs=[pl.BlockSpec((1, 1, tc * TOP_K), lambda i: (i, 0, 0), memory_space=pltpu.SMEM),
                  pl.BlockSpec((1, 1, tc * TOP_K), lambda i: (jnp.minimum(i + 1, nblk - 1), 0, 0),
                               memory_space=pltpu.SMEM),
                  pl.BlockSpec((tc, D_MODEL), lambda i: (i, 0)),
                  pl.BlockSpec((tc, LANES), lambda i: (i, 0)),
                  pl.BlockSpec(memory_space=pl.ANY),
                  _resident(ln_g.shape), _resident(ln_b.shape)],
        out_specs=[pl.BlockSpec((tc, D_MODEL), lambda i: (jnp.minimum(i, nblk_a - 1), 0)),
                   pl.BlockSpec((tc, D_MODEL), lambda i: (jnp.maximum(i - nblk_a, 0), 0))],
        out_shape=[jax.ShapeDtypeStruct((split, D_MODEL), F32), jax.ShapeDtypeStruct((M - split, D_MODEL), F32)],
        scratch_shapes=[pltpu.VMEM((2, TOP_K, tc, D_MODEL), F32), pltpu.SemaphoreType.DMA((2,))],
        compiler_params=_cparams(1),
        name="moe_combine",
    )(pos3, pos3, x2, tg, ys, ln_g, ln_b)


def _moe(x2, x2_packed, topi, gates, wg, bg, wu, bu, wd, bd, ln_g, ln_b, tm, tc, split):
    ntok = topi.shape[0]
    nk = ntok * TOP_K
    sb = min(tm, MOE_SUB_ROWS)
    n_tiles = -(-(nk + N_EXPERTS * (tm - 1)) // tm)
    e_flat = topi[:, :TOP_K].reshape(nk)
    onehot = (e_flat[:, None] == jnp.arange(N_EXPERTS, dtype=I32)[None, :]).astype(I32)
    csum = jnp.cumsum(onehot, axis=0)
    counts = csum[-1]
    rank = jnp.take_along_axis(csum, e_flat[:, None], axis=1)[:, 0] - 1
    tiles_e = (counts + tm - 1) // tm
    tile_end = jnp.cumsum(tiles_e)
    tile_start = tile_end - tiles_e
    dest = tile_start[e_flat] * tm + rank
    n_used = tile_end[-1:].astype(I32)
    t_ids = jnp.arange(n_tiles, dtype=I32)
    tile_e = jnp.minimum(jnp.searchsorted(tile_end, t_ids, side="right"), N_EXPERTS - 1).astype(I32)
    tile_rows = jnp.clip(counts[tile_e] - (t_ids - tile_start[tile_e]) * tm, 0, tm)
    tile_rows = jnp.where(t_ids < n_used[0], tile_rows, 0).astype(I32)
    P = n_tiles * tm
    first_row = (jnp.arange(nk, dtype=I32) // TOP_K) * PACK_SLABS
    row_src = jnp.zeros((P,), I32).at[dest].set(first_row, unique_indices=True)
    ys = _moe_ffn(x2_packed, row_src, tile_e, tile_rows, n_used, wg, bg, wu, bu, wd, bd, tm, sb)
    return _combine_ln3(x2, gates, ys, dest, ln_g, ln_b, tc, split)


def kernel(x_prompt, x_sample, mem_prompt, state_mlstm_C, state_mlstm_n, state_mlstm_m, cache_swa_k, cache_swa_v, cache_mem_k, cache_mem_v, w_in, b_igate, b_fgate, g_mnorm, w_mix_out, ln1_g, ln1_b, w_xq, w_xk, w_xv, w_xo, ln2_g, ln2_b, w_router, b_router, w_gate, b_gate, w_up, b_up, w_down, b_down, ln3_g, ln3_b):
    assert DEPTH == 1
    B, T, _ = x_prompt.shape
    DB, S, _ = x_sample.shape
    assert B == 1
    n_mem = mem_prompt.shape[1]
    wb = cache_swa_k.shape[2]
    row2 = lambda a: a[0].reshape(1, -1)

    wi = w_in[0]
    c0 = 2 * MIX_QK + 2 * MIX_V
    w_m = wi[:, :c0].astype(BF16)
    w_g = jnp.pad(wi[:, c0:c0 + 2 * M_HEADS], ((0, 0), (0, LANES - 2 * M_HEADS))).astype(BF16)
    w_a = wi[:, c0 + 2 * M_HEADS:].astype(BF16)
    gate_bias = jnp.pad(jnp.concatenate([b_igate[0], b_fgate[0]]), (0, LANES - 2 * M_HEADS)).reshape(1, LANES)
    w = dict(
        wmo=w_mix_out[0].astype(BF16), ln1_g=row2(ln1_g), ln1_b=row2(ln1_b), wxq=w_xq[0].astype(BF16),
        wxo=w_xo[0].astype(BF16), ln2_g=row2(ln2_g), ln2_b=row2(ln2_b),
        wr=jnp.pad(w_router[0], ((0, 0), (0, LANES - N_EXPERTS))).astype(BF16),
        br=jnp.pad(b_router[0], (0, LANES - N_EXPERTS)).reshape(1, LANES),
        wg=w_gate, bg=b_gate[0].reshape(N_EXPERTS, 1, D_FF), wu=w_up, bu=b_up[0].reshape(N_EXPERTS, 1, D_FF),
        wd=w_down, bd=b_down[0].reshape(N_EXPERTS, 1, D_MODEL), ln3_g=row2(ln3_g), ln3_b=row2(ln3_b))
    gm = g_mnorm[0].reshape(1, MIX_V)

    xp = x_prompt.reshape(T, D_MODEL)
    zm, zg = _mm_multi(xp, [w_m, w_g], [BF16, F32], 1024, "proj_mlstm")
    dils = tuple(dil for _, dil in PATTERNS)
    wbp = min(wb, T)
    *za_views, kv_tail = _proj_attn(xp, w_a, dils, wbp)
    zeros_c = jnp.zeros((1, M_HEADS, M_DQK, M_DV), F32)
    zeros_n = jnp.zeros((1, M_HEADS, M_DQK), F32)
    zeros_m = jnp.zeros((1, M_HEADS, LANES), F32)
    hm, pC, pn, pm = _mlstm(zm, zg, gate_bias, gm, zeros_c, zeros_n, zeros_m, 1, T // 512, L=512)
    pats = [_dilated_pattern(zv, window, dil) for zv, (window, dil) in zip(za_views, PATTERNS)]
    mp = mem_prompt.reshape(n_mem, D_MODEL)
    mem_k = _mm(mp, w_xk[0].astype(BF16), F32, n_mem, 1024, "mem_k")
    mem_v = _mm(mp, w_xv[0].astype(BF16), F32, n_mem, 1024, "mem_v")
    mem_k5 = mem_k.reshape(1, B, n_mem, X_HEADS, X_DH)
    mem_v5 = mem_v.reshape(1, B, n_mem, X_HEADS, X_DH)
    x1_p, q_p = _mix_ln1_q(xp, hm, [p[0] for p in pats], [p[1] for p in pats], dils, w["wmo"], w["ln1_g"], w["ln1_b"],
                           w["wxq"], 512)
    o_p = _xattn(q_p.reshape(1, T, D_MODEL), mem_k.astype(BF16).reshape(1, n_mem, D_MODEL),
                 mem_v.astype(BF16).reshape(1, n_mem, D_MODEL), 1024).reshape(T, D_MODEL)

    ns = DB * S
    xs_ = x_sample.reshape(ns, D_MODEL)
    zm_s, zg_s, za_s = _mm_multi(xs_, [w_m, w_g, w_a], [BF16, F32, F32], ns, "proj_sample")
    pad_rows = 128 - S
    zm_pad = jnp.pad(zm_s.reshape(DB, S, -1), ((0, 0), (0, pad_rows), (0, 0))).reshape(DB * 128, -1)
    lane = jnp.arange(LANES)
    neutral = jnp.where(lane < M_HEADS, NEG_BIG, jnp.where(lane < 2 * M_HEADS, -NEG_BIG, 0.0)).astype(F32)
    zg_pad = jnp.concatenate([zg_s.reshape(DB, S, LANES), jnp.broadcast_to(neutral, (DB, pad_rows, LANES))],
                             axis=1).reshape(DB * 128, LANES)
    m0 = jnp.broadcast_to(state_mlstm_m[0][:, :, None], (DB, M_HEADS, LANES))
    hm_s, sC, sn, sm = _mlstm(zm_pad, zg_pad, gate_bias, gm, state_mlstm_C[0], state_mlstm_n[0], m0, DB, 1)
    hm_s = hm_s.reshape(DB, 128, MIX_V)[:, :S].reshape(ns, MIX_V)
    za_pad = jnp.pad(za_s.reshape(DB, S, -1), ((0, 0), (0, SWA_ROWS - S), (0, 0))).astype(BF16)
    ha_s = _swa_sample(za_pad, cache_swa_k.reshape(DB, wb * A_HEADS, A_DH),
                       cache_swa_v.reshape(DB, wb * A_HEADS, A_DH), S)[:, :S].reshape(ns, MIX_A)
    x1_s, q_s = _mix_ln1_q(xs_, hm_s, [ha_s], [], (1,), w["wmo"], w["ln1_g"], w["ln1_b"], w["wxq"], ns)
    q_pad = jnp.pad(q_s.reshape(DB, S, D_MODEL), ((0, 0), (0, SWA_ROWS - S), (0, 0)))
    o_s = _xattn_cache(q_pad, cache_mem_k, cache_mem_v)[:, :S].reshape(ns, D_MODEL)
    x2_s, xk_s, topi_s, gates_s = _xo_ln2_router(x1_s, o_s, w["wxo"], w["ln2_g"], w["ln2_b"], w["wr"], w["br"], ns)

    x2_all, xk_all, topi_p, gates_p = _xo_ln2_router(x1_p, o_p, w["wxo"], w["ln2_g"], w["ln2_b"], w["wr"], w["br"],
                                                     512, tail=(x2_s, xk_s))
    yp, ys_ = _moe(x2_all, xk_all, jnp.concatenate([topi_p, topi_s]),
                   jnp.concatenate([gates_p, gates_s]), w["wg"], w["bg"], w["wu"], w["bu"], w["wd"], w["bd"],
                   w["ln3_g"], w["ln3_b"], 1024, 128, T)

    return (yp.reshape(B, T, D_MODEL), ys_.reshape(DB, S, D_MODEL),
            pC[None], pn[None], pm[:, :, 0][None],
            kv_tail[:, :MIX_A].reshape(1, B, wbp, A_HEADS, A_DH), kv_tail[:, MIX_A:].reshape(1, B, wbp, A_HEADS, A_DH),
            mem_k5, mem_v5,
            sC[None], sn[None], sm[:, :, 0][None],
            za_s[:, MIX_A:2 * MIX_A].reshape(1, DB, S, A_HEADS, A_DH),
            za_s[:, 2 * MIX_A:].reshape(1, DB, S, A_HEADS, A_DH))
```

```python
import functools

import jax
import jax.numpy as jnp
from jax import lax
from jax.experimental import pallas as pl
from jax.experimental.pallas import tpu as pltpu

F32, BF16, I32 = jnp.float32, jnp.bfloat16, jnp.int32

D_MODEL = 2048
DEPTH = 1
M_HEADS, M_DQK, M_DV = 4, 128, 256
HEAD_EPS = 1e-6
A_HEADS, A_DH = 8, 128
PATTERNS = ((128, 1), (512, 4), (2048, 16))
BAND_BLOCK = 128
X_HEADS = 4
X_DH = D_MODEL // X_HEADS
N_EXPERTS, TOP_K, D_FF = 32, 4, 2048
SWIGLU_LIMIT, SWIGLU_ALPHA = 7.0, 1.702
DN_ALPHA = (2 * DEPTH) ** 0.25
LN_EPS = 1e-5
MIX_V = M_HEADS * M_DV
MIX_A = A_HEADS * A_DH
MIX_QK = M_HEADS * M_DQK

LANES = 128
VMEM_LIMIT_BYTES = 58 * 1024 * 1024

NEG_BIG = -1e30


def _cparams(n_axes, vmem=VMEM_LIMIT_BYTES):
    return pltpu.CompilerParams(dimension_semantics=("arbitrary",) * n_axes, vmem_limit_bytes=vmem)


def _dot(a, b):
    return jnp.dot(a, b, preferred_element_type=F32)


def _dot_nt(a, b):
    return lax.dot_general(a, b, (((1,), (1,)), ((), ())), preferred_element_type=F32)


def _dot_tn(a, b):
    return lax.dot_general(a, b, (((0,), (0,)), ((), ())), preferred_element_type=F32)


def _log_sigmoid(x):
    return jnp.minimum(x, 0.0) - jnp.log(1.0 + jnp.exp(-jnp.abs(x)))


def _sigmoid(x):
    return 1.0 / (1.0 + jnp.exp(-x))


def _layer_norm(v, g, b):
    mu = jnp.mean(v, axis=-1, keepdims=True)
    d = v - mu
    var = jnp.mean(d * d, axis=-1, keepdims=True)
    return d * lax.rsqrt(var + LN_EPS) * g + b


def _resident(shape):
    nd = len(shape)
    return pl.BlockSpec(shape, lambda *_: (0,) * nd, pipeline_mode=pl.Buffered(1))


def _mm_body(x_ref, w_ref, o_ref):
    o_ref[...] = _dot(x_ref[...].astype(BF16), w_ref[...]).astype(o_ref.dtype)


def _mm(x, w, out_dtype, tm, tn, name):
    M, K = x.shape
    N = w.shape[1]
    assert M % tm == 0 and N % tn == 0
    return pl.pallas_call(
        _mm_body,
        grid=(M // tm, N // tn),
        in_specs=[pl.BlockSpec((tm, K), lambda i, j: (i, 0)), pl.BlockSpec((K, tn), lambda i, j: (0, j))],
        out_specs=pl.BlockSpec((tm, tn), lambda i, j: (i, j)),
        out_shape=jax.ShapeDtypeStruct((M, N), out_dtype),
        compiler_params=_cparams(2),
        name=name,
    )(x, w)


def _mm_multi_body(x_ref, *refs, tn):
    n = len(refs) // 2
    xb = x_ref[...].astype(BF16)
    for w_ref, o_ref in zip(refs[:n], refs[n:]):
        width = w_ref.shape[1]
        step = min(tn, width)
        for j in range(width // step):
            o_ref[:, j * step:(j + 1) * step] = _dot(xb, w_ref[:, j * step:(j + 1) * step]).astype(o_ref.dtype)


def _mm_multi(x, ws, out_dtypes, tm, name, tn=1024):
    M, K = x.shape
    assert M % tm == 0 and all(w.shape[1] % min(tn, w.shape[1]) == 0 for w in ws)
    return pl.pallas_call(
        functools.partial(_mm_multi_body, tn=tn),
        grid=(M // tm,),
        in_specs=[pl.BlockSpec((tm, K), lambda i: (i, 0))] + [_resident(w.shape) for w in ws],
        out_specs=[pl.BlockSpec((tm, w.shape[1]), lambda i: (i, 0)) for w in ws],
        out_shape=[jax.ShapeDtypeStruct((M, w.shape[1]), dt) for w, dt in zip(ws, out_dtypes)],
        compiler_params=_cparams(1),
        name=name,
    )(x, *ws)


def _mlstm_body(q_ref, k_ref, v_ref, zo_ref, zg_ref, gb_ref, gm_ref, c0_ref, n0_ref, m0_ref,
                hm_ref, c_ref, n_ref, m_ref, *, L, sub):
    @pl.when(pl.program_id(1) == 0)
    def _():
        c_ref[...] = c0_ref[...]
        n_ref[...] = n0_ref[...]
        m_ref[...] = m0_ref[...]

    for ci in range(sub):
        _mlstm_chunk(slice(ci * L, (ci + 1) * L), q_ref, k_ref, v_ref, zo_ref, zg_ref, gb_ref, gm_ref,
                     hm_ref, c_ref, n_ref, m_ref, L)


def _mlstm_chunk(rows, q_ref, k_ref, v_ref, zo_ref, zg_ref, gb_ref, gm_ref, hm_ref, c_ref, n_ref, m_ref, L):
    scale = M_DQK ** -0.5
    g = zg_ref[rows, :] + gb_ref[...]
    gt = g.T
    row = lax.broadcasted_iota(I32, (L, L), 0)
    col = lax.broadcasted_iota(I32, (L, L), 1)
    tri = row >= col
    for h in range(M_HEADS):
        i_col = g[:, h:h + 1]
        f_col = _log_sigmoid(g[:, M_HEADS + h:M_HEADS + h + 1])
        i_row = gt[h:h + 1, :]
        f_row = _log_sigmoid(gt[M_HEADS + h:M_HEADS + h + 1, :])
        q = q_ref[rows, h * M_DQK:(h + 1) * M_DQK]
        k = k_ref[rows, h * M_DQK:(h + 1) * M_DQK]
        v = v_ref[rows, h * M_DV:(h + 1) * M_DV]
        zo = zo_ref[rows, h * M_DV:(h + 1) * M_DV].astype(F32)
        C = c_ref[0, h]
        n = n_ref[0, h:h + 1, :]
        m = m_ref[0, h:h + 1, 0:1]
        b_col = jnp.sum(jnp.where(tri, f_row, 0.0), axis=1, keepdims=True)
        b_row = jnp.sum(jnp.where(row <= col, f_col, 0.0), axis=0, keepdims=True)
        logd = jnp.where(tri, b_col - b_row + i_row, -jnp.inf)
        inter = b_col + m
        mt = jnp.maximum(inter, jnp.max(logd, axis=1, keepdims=True))
        sd = _dot_nt(q, k) * scale * jnp.exp(logd - mt)
        sc = jnp.exp(inter - mt)
        num = _dot(sd.astype(BF16), v) + sc * _dot(q, C.astype(BF16))
        qn = _dot_nt(q, jnp.broadcast_to(n, (16, M_DQK)).astype(BF16))[:, 0:1]
        den = jnp.sum(sd, axis=1, keepdims=True) + sc * qn
        hh = num / jnp.maximum(jnp.abs(den), jnp.exp(-mt))
        hn = hh * lax.rsqrt(jnp.mean(hh * hh, axis=1, keepdims=True) + HEAD_EPS)
        out = hn * gm_ref[:, h * M_DV:(h + 1) * M_DV] * _sigmoid(zo)
        hm_ref[rows, h * M_DV:(h + 1) * M_DV] = out.astype(hm_ref.dtype)
        bl = b_col[L - 1:L, :]
        ml = mt[L - 1:L, :]
        w_col = jnp.exp(bl - b_col + i_col - ml)
        scl = jnp.exp(bl + m - ml)
        kw = k.astype(F32) * (w_col * scale)
        c_ref[0, h] = scl * C + _dot_tn(kw.astype(BF16), v)
        n_ref[0, h:h + 1, :] = scl * n + jnp.sum(kw, axis=0, keepdims=True)
        m_ref[0, h:h + 1, :] = jnp.broadcast_to(ml, (1, LANES))


def _mlstm(zm, zg, gate_bias, g_mnorm, c0, n0, m0, B, nc, sub=1, L=128):
    T = B * nc * L
    assert zm.shape == (T, 2 * MIX_QK + 2 * MIX_V) and nc % sub == 0
    nc, L_chunk, L = nc // sub, L, L * sub
    row_blk = lambda b, c: b * nc + c
    state_specs = [pl.BlockSpec((1, M_HEADS, M_DQK, M_DV), lambda b, c: (b, 0, 0, 0)),
                   pl.BlockSpec((1, M_HEADS, M_DQK), lambda b, c: (b, 0, 0)),
                   pl.BlockSpec((1, M_HEADS, LANES), lambda b, c: (b, 0, 0))]
    return pl.pallas_call(
        functools.partial(_mlstm_body, L=L_chunk, sub=sub),
        grid=(B, nc),
        in_specs=[pl.BlockSpec((L, MIX_QK), lambda b, c: (row_blk(b, c), 0)),
                  pl.BlockSpec((L, MIX_QK), lambda b, c: (row_blk(b, c), 1)),
                  pl.BlockSpec((L, MIX_V), lambda b, c: (row_blk(b, c), 1)),
                  pl.BlockSpec((L, MIX_V), lambda b, c: (row_blk(b, c), 2)),
                  pl.BlockSpec((L, LANES), lambda b, c: (row_blk(b, c), 0)),
                  pl.BlockSpec((1, LANES), lambda b, c: (0, 0)),
                  pl.BlockSpec((1, MIX_V), lambda b, c: (0, 0))] + state_specs,
        out_specs=[pl.BlockSpec((L, MIX_V), lambda b, c: (row_blk(b, c), 0))] + state_specs,
        out_shape=[jax.ShapeDtypeStruct((T, MIX_V), BF16),
                   jax.ShapeDtypeStruct((B, M_HEADS, M_DQK, M_DV), F32),
                   jax.ShapeDtypeStruct((B, M_HEADS, M_DQK), F32),
                   jax.ShapeDtypeStruct((B, M_HEADS, LANES), F32)],
        compiler_params=_cparams(2),
        name="mlstm",
    )(zm, zm, zm, zm, zg, gate_bias, g_mnorm, c0, n0, m0)


DIL_BLOCKS_PER_STEP = 4


def _dil_body(q_ref, kc_ref, kp_ref, vc_ref, vp_ref, o_ref, l_ref, *, span):
    scale = A_DH ** -0.5
    n = pl.program_id(1)
    row = lax.broadcasted_iota(I32, (BAND_BLOCK, BAND_BLOCK), 0)
    col = lax.broadcasted_iota(I32, (BAND_BLOCK, BAND_BLOCK), 1)
    first = jnp.where(n > 0, 0, 2 * BAND_BLOCK)
    mask_c = row >= col
    lane = lax.broadcasted_iota(I32, (BAND_BLOCK, LANES), 1)
    ones = jnp.ones((BAND_BLOCK, A_DH), BF16)
    for j in range(DIL_BLOCKS_PER_STEP):
        rows = slice(j * BAND_BLOCK, (j + 1) * BAND_BLOCK)
        prev = slice((j - 1) * BAND_BLOCK, j * BAND_BLOCK)
        mask_p = (BAND_BLOCK + row - col + (first if j == 0 else 0)) <= span
        lse_tile = jnp.zeros((BAND_BLOCK, LANES), F32)
        for h in range(A_HEADS):
            sl = slice(h * A_DH, (h + 1) * A_DH)
            q = q_ref[rows, sl]
            kp = kp_ref[:, sl] if j == 0 else kc_ref[prev, sl]
            vp = vp_ref[:, sl] if j == 0 else vc_ref[prev, sl]
            sp = jnp.where(mask_p, _dot_nt(q, kp) * scale, -jnp.inf)
            sc = jnp.where(mask_c, _dot_nt(q, kc_ref[rows, sl]) * scale, -jnp.inf)
            m = jnp.max(jnp.maximum(sp, sc), axis=1, keepdims=True)
            pp = jnp.exp(sp - m).astype(BF16)
            pc = jnp.exp(sc - m).astype(BF16)
            oa = (_dot(pp, jnp.concatenate([vp, ones], axis=1))
                  + _dot(pc, jnp.concatenate([vc_ref[rows, sl], ones], axis=1)))
            l = oa[:, A_DH:A_DH + 1]
            o_ref[rows, sl] = (oa[:, :A_DH] / l).astype(o_ref.dtype)
            lse_tile = jnp.where(lane == h, m + jnp.log(l), lse_tile)
        l_ref[rows, :] = lse_tile


PROJ_ATTN_ROWS = 512


def _proj_attn_body(x_ref, w_ref, *refs, dils):
    o_refs, tail_ref, acc = refs[:len(dils)], refs[len(dils)], refs[len(dils) + 1]
    tm = x_ref.shape[0]
    nslab = acc.shape[0]
    tn = nslab * LANES
    xb = x_ref[...].astype(BF16)
    for j in range(w_ref.shape[1] // tn):
        z = _dot(xb, w_ref[:, j * tn:(j + 1) * tn])
        if j * tn >= MIX_A:
            tail_ref[:, j * tn - MIX_A:(j + 1) * tn - MIX_A] = z
        for dil, o_ref in zip(dils, o_refs):
            if dil == 1:
                o_ref[:, j * tn:(j + 1) * tn] = z.astype(o_ref.dtype)
        for s in range(nslab):
            acc[s] = z[:, s * LANES:(s + 1) * LANES]
        for dil, o_ref in zip(dils, o_refs):
            if dil == 1:
                continue
            rows = tm // dil
            for r in range(dil):
                for s in range(nslab):
                    c0 = r * w_ref.shape[1] + j * tn + s * LANES
                    o_ref[:, c0:c0 + LANES] = acc[s, pl.ds(r, rows, stride=dil), :].astype(o_ref.dtype)


def _proj_attn(x, w, dils, tail_rows):
    T, K = x.shape
    N = w.shape[1]
    tm, tn = PROJ_ATTN_ROWS, 1024
    assert T % tm == 0 and N % tn == 0 and all(tm % (16 * d) == 0 for d in dils)
    assert tail_rows % tm == 0 and MIX_A % tn == 0 and N == 3 * MIX_A
    first_tail = (T - tail_rows) // tm
    return pl.pallas_call(
        functools.partial(_proj_attn_body, dils=dils),
        grid=(T // tm,),
        in_specs=[pl.BlockSpec((tm, K), lambda i: (i, 0)), _resident(w.shape)],
        out_specs=[pl.BlockSpec((tm // d, d * N), lambda i: (i, 0)) for d in dils]
        + [pl.BlockSpec((tm, 2 * MIX_A), lambda i: (jnp.maximum(i - first_tail, 0), 0))],
        out_shape=[jax.ShapeDtypeStruct((T // d, d * N), BF16) for d in dils]
        + [jax.ShapeDtypeStruct((tail_rows, 2 * MIX_A), F32)],
        scratch_shapes=[pltpu.VMEM((tn // LANES, tm, LANES), F32)],
        compiler_params=_cparams(1),
        name="proj_attn",
    )(x, w)


def _dilated_pattern(zv, window, dil):
    N = zv.shape[0]
    span = window // dil
    step_rows = DIL_BLOCKS_PER_STEP * BAND_BLOCK
    assert zv.shape[1] == dil * 3 * MIX_A and N % step_rows == 0 and BAND_BLOCK - 1 <= span
    nb = N // step_rows
    prev = lambda n: jnp.maximum(n * DIL_BLOCKS_PER_STEP - 1, 0)
    blk = (step_rows, MIX_A)
    pblk = (BAND_BLOCK, MIX_A)
    o, lse = pl.pallas_call(
        functools.partial(_dil_body, span=span),
        grid=(dil, nb),
        in_specs=[pl.BlockSpec(blk, lambda r, n: (n, 3 * r)),
                  pl.BlockSpec(blk, lambda r, n: (n, 3 * r + 1)),
                  pl.BlockSpec(pblk, lambda r, n: (prev(n), 3 * r + 1)),
                  pl.BlockSpec(blk, lambda r, n: (n, 3 * r + 2)),
                  pl.BlockSpec(pblk, lambda r, n: (prev(n), 3 * r + 2))],
        out_specs=[pl.BlockSpec(blk, lambda r, n: (n, r)),
                   pl.BlockSpec((step_rows, LANES), lambda r, n: (n, r))],
        out_shape=[jax.ShapeDtypeStruct((N, dil * MIX_A), BF16),
                   jax.ShapeDtypeStruct((N, dil * LANES), F32)],
        compiler_params=_cparams(2),
        name=f"dilated_d{dil}",
    )(zv, zv, zv, zv, zv)
    return o, lse


SWA_ROWS = 16


def _swa_body(q_ref, kn_ref, vn_ref, pos_ref, kr_ref, vr_ref, k_hbm, v_hbm, o_ref, oldk, oldv, sem, *,
              wb, nb, n_new, n_old, big_dil, dense):
    scale = A_DH ** -0.5
    b = pl.program_id(0)
    slot = lax.rem(b, 2)
    grp = n_new * A_HEADS

    def old_copies(bi, sl):
        cps = []
        for a in range(n_old):
            src = pl.ds(a * big_dil * A_HEADS, grp)
            dst = pl.ds(sl * n_old * grp + a * grp, grp)
            cps.append(pltpu.make_async_copy(k_hbm.at[bi, src, :], oldk.at[dst, :], sem.at[sl]))
            cps.append(pltpu.make_async_copy(v_hbm.at[bi, src, :], oldv.at[dst, :], sem.at[sl]))
        return cps

    @pl.when(b == 0)
    def _():
        for cp in old_copies(0, 0):
            cp.start()

    for cp in old_copies(b, slot):
        cp.wait()
    nxt = jnp.minimum(b + 1, nb - 1)
    for cp in old_copies(nxt, 1 - slot):
        cp.start()

    ncols = n_new * n_old + dense
    s_k = lax.broadcasted_iota(I32, (SWA_ROWS, ncols), 0)
    d_k = wb + s_k - pos_ref[...]
    s_n = lax.broadcasted_iota(I32, (SWA_ROWS, SWA_ROWS), 0)
    p_n = lax.broadcasted_iota(I32, (SWA_ROWS, SWA_ROWS), 1)
    d_n = s_n - p_n
    masks = []
    for window, dil in PATTERNS:
        assert dil & (dil - 1) == 0 and window % dil == 0
        mk = jnp.where(jnp.bitwise_and(d_k, dil - 1) == 0, d_k, window + 1) <= window
        mn = jnp.where(jnp.bitwise_and(d_n, dil - 1) == 0, jnp.where(d_n >= 0, d_n, window + 1), window + 1) <= window
        masks.append((mk, mn))
    for h in range(A_HEADS):
        sl = slice(h * A_DH, (h + 1) * A_DH)
        q = q_ref[0, :, sl]

        def head_rows(old, rec_ref):
            parts = [old[pl.ds(slot * n_old * grp + s * A_HEADS + h, n_old, stride=grp), :] for s in range(n_new)]
            parts.append(rec_ref[0, pl.ds(h, dense, stride=A_HEADS), :])
            return jnp.concatenate(parts, axis=0).astype(BF16)

        kb = head_rows(oldk, kr_ref)
        vb = head_rows(oldv, vr_ref)
        kn = kn_ref[0, :, sl]
        vn = vn_ref[0, :, sl]
        s_cache = _dot_nt(q, kb) * scale
        s_new = _dot_nt(q, kn) * scale
        ps, lses = [], []
        for mk, mn in masks:
            sk = jnp.where(mk, s_cache, -jnp.inf)
            sn = jnp.where(mn, s_new, -jnp.inf)
            m = jnp.maximum(jnp.max(sk, axis=1, keepdims=True), jnp.max(sn, axis=1, keepdims=True))
            pk = jnp.exp(sk - m)
            pn = jnp.exp(sn - m)
            l = jnp.sum(pk, axis=1, keepdims=True) + jnp.sum(pn, axis=1, keepdims=True)
            ps.append((pk, pn, l))
            lses.append(m + jnp.log(l))
        top = functools.reduce(jnp.maximum, lses)
        es = [jnp.exp(x - top) for x in lses]
        tot = functools.reduce(lambda a, b: a + b, es)
        pk_all = None
        pn_all = None
        for (pk, pn, l), e in zip(ps, es):
            coef = e / (tot * l)
            pk_all = pk * coef if pk_all is None else pk_all + pk * coef
            pn_all = pn * coef if pn_all is None else pn_all + pn * coef
        o = _dot(pk_all.astype(BF16), vb) + _dot(pn_all.astype(BF16), vn)
        o_ref[0, :, sl] = o.astype(o_ref.dtype)

    @pl.when(b == nb - 1)
    def _():
        for cp in old_copies(nxt, 1 - slot):
            cp.wait()


def _swa_sample(za_pad, kbuf, vbuf, n_new):
    B, wb = kbuf.shape[0], kbuf.shape[1] // A_HEADS
    big_dil = PATTERNS[-1][1]
    dense = max(w for w, _ in PATTERNS[:-1])
    assert all(w <= PATTERNS[-1][0] and d <= big_dil for w, d in PATTERNS[:-1])
    assert wb % dense == 0 and dense % big_dil == 0 and wb % big_dil == 0 and n_new <= big_dil
    n_old = (wb - dense) // big_dil
    assert n_old % 8 == 0
    grp = n_new * A_HEADS
    old_pos = (jnp.arange(n_old, dtype=I32)[None, :] * big_dil + jnp.arange(n_new, dtype=I32)[:, None]).reshape(-1)
    pos = jnp.concatenate([old_pos, wb - dense + jnp.arange(dense, dtype=I32)]).reshape(1, -1)
    new_blk = (1, SWA_ROWS, MIX_A)
    rec_blk = (1, dense * A_HEADS, A_DH)
    last = wb // dense - 1
    return pl.pallas_call(
        functools.partial(_swa_body, wb=wb, nb=B, n_new=n_new, n_old=n_old, big_dil=big_dil, dense=dense),
        grid=(B,),
        in_specs=[pl.BlockSpec(new_blk, lambda b: (b, 0, 0)),
                  pl.BlockSpec(new_blk, lambda b: (b, 0, 1)),
                  pl.BlockSpec(new_blk, lambda b: (b, 0, 2)),
                  _resident(pos.shape),
                  pl.BlockSpec(rec_blk, lambda b: (b, last, 0)),
                  pl.BlockSpec(rec_blk, lambda b: (b, last, 0)),
                  pl.BlockSpec(memory_space=pl.ANY), pl.BlockSpec(memory_space=pl.ANY)],
        out_specs=pl.BlockSpec(new_blk, lambda b: (b, 0, 0)),
        out_shape=jax.ShapeDtypeStruct((B, SWA_ROWS, MIX_A), BF16),
        scratch_shapes=[pltpu.VMEM((2 * n_old * grp, A_DH), F32), pltpu.VMEM((2 * n_old * grp, A_DH), F32),
                        pltpu.SemaphoreType.DMA((2,))],
        compiler_params=_cparams(1),
        name="swa_sample",
    )(za_pad, za_pad, za_pad, pos, kbuf, vbuf, kbuf, vbuf)


def _mix_body(*refs, dils):
    n_pat = len(dils)
    x_ref, hm_ref = refs[0], refs[1]
    n_lse = n_pat if n_pat > 1 else 0
    o_refs = refs[2:2 + n_pat]
    l_refs = refs[2 + n_pat:2 + n_pat + n_lse]
    wmo_ref, g_ref, b_ref, wq_ref, x1_ref, q_ref = refs[2 + n_pat + n_lse:8 + n_pat + n_lse]
    if n_pat == 1:
        ha = o_refs[0][...]
    else:
        o_nat, l_nat = refs[8 + n_pat + n_lse:]
        tm = x_ref.shape[0]
        for p, dil in enumerate(dils):
            if dil == 1:
                continue
            rows = tm // dil
            for r in range(dil):
                dst = pl.ds(r, rows, stride=dil)
                l_nat[p, dst, :] = l_refs[p][:, r * LANES:(r + 1) * LANES]
                for h in range(A_HEADS):
                    c0 = r * MIX_A + h * A_DH
                    o_nat[p, h, dst, :] = o_refs[p][:, c0:c0 + A_DH].astype(F32)

        def lse_of(p, h):
            return l_refs[p][:, h:h + 1] if dils[p] == 1 else l_nat[p, :, h:h + 1]

        def out_of(p, h):
            return o_refs[p][:, h * A_DH:(h + 1) * A_DH].astype(F32) if dils[p] == 1 else o_nat[p, h]

        cols = []
        for h in range(A_HEADS):
            ls = [lse_of(p, h) for p in range(n_pat)]
            top = functools.reduce(jnp.maximum, ls)
            es = [jnp.exp(x - top) for x in ls]
            inv = 1.0 / functools.reduce(lambda a, b: a + b, es)
            acc = None
            for p, e in enumerate(es):
                term = out_of(p, h) * (e * inv)
                acc = term if acc is None else acc + term
            cols.append(acc.astype(BF16))
        ha = jnp.concatenate(cols, axis=1)
    mix = _dot(hm_ref[...], wmo_ref[0:MIX_V, :]) + _dot(ha, wmo_ref[MIX_V:MIX_V + MIX_A, :])
    x1 = _layer_norm(DN_ALPHA * x_ref[...] + mix, g_ref[...], b_ref[...])
    x1_ref[...] = x1
    q_ref[...] = _dot(x1.astype(BF16), wq_ref[...]).astype(q_ref.dtype)


def _mix_ln1_q(x, hm, outs, lses, dils, wmo, ln_g, ln_b, wq, tm):
    M = x.shape[0]
    n_pat = len(outs)
    assert len(lses) == (n_pat if n_pat > 1 else 0) and len(dils) == n_pat
    assert all(tm % (16 * d) == 0 for d in dils)
    row = lambda w: pl.BlockSpec((tm, w), lambda i: (i, 0))
    view = lambda w, d: pl.BlockSpec((tm // d, d * w), lambda i: (i, 0))
    scratch = []
    if n_pat > 1:
        scratch = [pltpu.VMEM((n_pat, A_HEADS, tm, A_DH), F32), pltpu.VMEM((n_pat, tm, LANES), F32)]
    return pl.pallas_call(
        functools.partial(_mix_body, dils=tuple(dils)),
        grid=(M // tm,),
        in_specs=[row(D_MODEL), row(MIX_V)] + [view(MIX_A, d) for d in dils]
        + [view(LANES, d) for d in dils[:len(lses)]]
        + [_resident(wmo.shape), _resident(ln_g.shape), _resident(ln_b.shape), _resident(wq.shape)],
        out_specs=[row(D_MODEL), row(D_MODEL)],
        out_shape=[jax.ShapeDtypeStruct((M, D_MODEL), F32), jax.ShapeDtypeStruct((M, D_MODEL), BF16)],
        scratch_shapes=scratch,
        compiler_params=_cparams(1),
        name="mix_ln1_q",
    )(x, hm, *outs, *lses, wmo, ln_g, ln_b, wq)


def _xattn_body(q_ref, k_ref, v_ref, o_ref):
    for h in range(X_HEADS):
        sl = slice(h * X_DH, (h + 1) * X_DH)
        o_ref[0, :, sl] = _xattn_head(q_ref[0, :, sl], k_ref[0, :, sl], v_ref[0, :, sl]).astype(o_ref.dtype)


def _xattn_head(q, k, v):
    s = _dot_nt(q, k.astype(BF16)) * (X_DH ** -0.5)
    p = jnp.exp(s - jnp.max(s, axis=1, keepdims=True))
    p = p / jnp.sum(p, axis=1, keepdims=True)
    return _dot(p.astype(BF16), v.astype(BF16))


def _xattn_cache_body(q_ref, k_hbm, v_hbm, o_ref, kbuf, vbuf, sem, *, nb):
    b = pl.program_id(0)
    slot = lax.rem(b, 2)

    def head_copies(bi, sl):
        cps = []
        for h in range(X_HEADS):
            cps.append(pltpu.make_async_copy(k_hbm.at[0, bi, :, h, :], kbuf.at[sl, h], sem.at[sl]))
            cps.append(pltpu.make_async_copy(v_hbm.at[0, bi, :, h, :], vbuf.at[sl, h], sem.at[sl]))
        return cps

    @pl.when(b == 0)
    def _():
        for cp in head_copies(0, 0):
            cp.start()

    for cp in head_copies(b, slot):
        cp.wait()
    nxt = jnp.minimum(b + 1, nb - 1)
    for cp in head_copies(nxt, 1 - slot):
        cp.start()
    for h in range(X_HEADS):
        sl = slice(h * X_DH, (h + 1) * X_DH)
        o_ref[0, :, sl] = _xattn_head(q_ref[0, :, sl], kbuf[slot, h], vbuf[slot, h]).astype(o_ref.dtype)

    @pl.when(b == nb - 1)
    def _():
        for cp in head_copies(nxt, 1 - slot):
            cp.wait()


def _xattn_cache(q, ck, cv):
    B, Tq, _ = q.shape
    nm = ck.shape[2]
    return pl.pallas_call(
        functools.partial(_xattn_cache_body, nb=B),
        grid=(B,),
        in_specs=[pl.BlockSpec((1, Tq, D_MODEL), lambda b: (b, 0, 0)),
                  pl.BlockSpec(memory_space=pl.ANY), pl.BlockSpec(memory_space=pl.ANY)],
        out_specs=pl.BlockSpec((1, Tq, D_MODEL), lambda b: (b, 0, 0)),
        out_shape=jax.ShapeDtypeStruct((B, Tq, D_MODEL), BF16),
        scratch_shapes=[pltpu.VMEM((2, X_HEADS, nm, X_DH), F32), pltpu.VMEM((2, X_HEADS, nm, X_DH), F32),
                        pltpu.SemaphoreType.DMA((2,))],
        compiler_params=_cparams(1),
        name="xattn_cache",
    )(q, ck, cv)


def _xattn(q, mk, mv, tq):
    B, Tq, _ = q.shape
    nm = mk.shape[1]
    mem_blk = (1, nm, D_MODEL)
    return pl.pallas_call(
        _xattn_body,
        grid=(B, Tq // tq),
        in_specs=[pl.BlockSpec((1, tq, D_MODEL), lambda b, i: (b, i, 0)),
                  pl.BlockSpec(mem_blk, lambda b, i: (b, 0, 0)),
                  pl.BlockSpec(mem_blk, lambda b, i: (b, 0, 0))],
        out_specs=pl.BlockSpec((1, tq, D_MODEL), lambda b, i: (b, i, 0)),
        out_shape=jax.ShapeDtypeStruct((B, Tq, D_MODEL), BF16),
        compiler_params=_cparams(2),
        name="xattn",
    )(q, mk, mv)


PACK_SLABS = D_MODEL // (2 * LANES)


def _pack_bf16_pairs(lo, hi):
    lo_bits = pltpu.bitcast(lo.astype(BF16).astype(F32), jnp.uint32)
    hi_bits = pltpu.bitcast(hi.astype(BF16).astype(F32), jnp.uint32)
    return jnp.bitwise_or(jnp.bitwise_and(hi_bits, jnp.uint32(0xFFFF0000)), lax.shift_right_logical(lo_bits, jnp.uint32(16)))


def _unpack_bf16_pairs(w):
    lo = pltpu.bitcast(lax.shift_left(w, jnp.uint32(16)), F32).astype(BF16)
    hi = pltpu.bitcast(jnp.bitwise_and(w, jnp.uint32(0xFFFF0000)), F32).astype(BF16)
    return lo, hi


def _xo_body(x1_ref, o_ref, wo_ref, g_ref, b_ref, wr_ref, br_ref, *rest, tail_rows, nsteps):
    if not tail_rows:
        _xo_rows(x1_ref, o_ref, wo_ref, g_ref, b_ref, wr_ref, br_ref, *rest)
        return
    tx2_ref, txp_ref, x2_ref, xp_ref, ti_ref, tg_ref = rest
    i = pl.program_id(0)

    @pl.when(i < nsteps)
    def _():
        _xo_rows(x1_ref, o_ref, wo_ref, g_ref, b_ref, wr_ref, br_ref, x2_ref, xp_ref, ti_ref, tg_ref)

    @pl.when(i == nsteps)
    def _():
        tm = x2_ref.shape[0]
        x2_ref[0:tail_rows, :] = tx2_ref[...]
        x2_ref[tail_rows:, :] = jnp.zeros((tm - tail_rows, D_MODEL), F32)
        xp_ref[0:tail_rows * PACK_SLABS, :] = txp_ref[...]
        xp_ref[tail_rows * PACK_SLABS:, :] = jnp.zeros(((tm - tail_rows) * PACK_SLABS, LANES), jnp.uint32)


def _xo_rows(x1_ref, o_ref, wo_ref, g_ref, b_ref, wr_ref, br_ref, x2_ref, xp_ref, ti_ref, tg_ref):
    y = _dot(o_ref[...], wo_ref[...])
    x2 = _layer_norm(DN_ALPHA * x1_ref[...] + y, g_ref[...], b_ref[...])
    x2_ref[...] = x2
    tm = x2.shape[0]
    for s in range(PACK_SLABS):
        lo = x2[:, s * LANES:(s + 1) * LANES]
        hi = x2[:, D_MODEL // 2 + s * LANES:D_MODEL // 2 + (s + 1) * LANES]
        xp_ref[pl.ds(s, tm, stride=PACK_SLABS), :] = _pack_bf16_pairs(lo, hi)
    lane = lax.broadcasted_iota(I32, (tm, LANES), 1)
    lanef = lane.astype(F32)
    logits = jnp.where(lane < N_EXPERTS, _dot(x2.astype(BF16), wr_ref[...]) + br_ref[...], -jnp.inf)
    vals, idxs = [], []
    cur = logits
    for _ in range(TOP_K):
        top = jnp.max(cur, axis=1, keepdims=True)
        idx = jnp.min(jnp.where(cur == top, lanef, float(LANES)), axis=1, keepdims=True)
        vals.append(top)
        idxs.append(idx)
        cur = jnp.where(lanef == idx, -jnp.inf, cur)
    es = [jnp.exp(v - vals[0]) for v in vals]
    inv = 1.0 / functools.reduce(lambda a, b: a + b, es)
    ti = jnp.zeros((tm, LANES), F32)
    tg = jnp.zeros((tm, LANES), F32)
    for k in range(TOP_K):
        ti = jnp.where(lane == k, idxs[k], ti)
        tg = jnp.where(lane == k, es[k] * inv, tg)
    ti_ref[...] = ti.astype(I32)
    tg_ref[...] = tg


def _xo_ln2_router(x1, o, wo, ln_g, ln_b, wr, br, tm, tail=None):
    M = x1.shape[0]
    nsteps = M // tm
    tail_rows = 0 if tail is None else tail[0].shape[0]
    assert tail_rows <= tm
    extra = 1 if tail_rows else 0
    last = nsteps - 1
    row_in = lambda w: pl.BlockSpec((tm, w), lambda i: (jnp.minimum(i, last), 0))
    tail_specs = [] if tail is None else [_resident(tail[0].shape), _resident(tail[1].shape)]
    return pl.pallas_call(
        functools.partial(_xo_body, tail_rows=tail_rows, nsteps=nsteps),
        grid=(nsteps + extra,),
        in_specs=[row_in(D_MODEL), row_in(D_MODEL), _resident(wo.shape), _resident(ln_g.shape),
                  _resident(ln_b.shape), _resident(wr.shape), _resident(br.shape)] + tail_specs,
        out_specs=[pl.BlockSpec((tm, D_MODEL), lambda i: (i, 0)),
                   pl.BlockSpec((tm * PACK_SLABS, LANES), lambda i: (i, 0)), row_in(LANES), row_in(LANES)],
        out_shape=[jax.ShapeDtypeStruct((M + extra * tm, D_MODEL), F32),
                   jax.ShapeDtypeStruct(((M + extra * tm) * PACK_SLABS, LANES), jnp.uint32),
                   jax.ShapeDtypeStruct((M, LANES), I32), jax.ShapeDtypeStruct((M, LANES), F32)],
        compiler_params=_cparams(1),
        name="xo_ln2_router",
    )(x1, o, wo, ln_g, ln_b, wr, br, *([] if tail is None else list(tail)))


MOE_SUB_ROWS = 256
MOE_FF_CHUNK = 512


def _ffn_body(te_ref, tr_ref, nu_ref, idc_ref, idn_ref, x_hbm, wg_ref, bg_ref, wu_ref, bu_ref, wd_ref, bd_ref,
              o_ref, stg, sem, xbb, *, tm, sb, nch):
    t = pl.program_id(0)
    c = pl.program_id(1)
    n_used = nu_ref[0]
    ps = tm // nch
    gp = ps * PACK_SLABS
    cur = lax.rem(t, 2)
    tp = tm * PACK_SLABS

    def row_copy(src_row, sl, g, r):
        src_row = pl.multiple_of(src_row, PACK_SLABS)
        dst = pl.ds(sl * tp + g * gp + r * PACK_SLABS, PACK_SLABS)
        return pltpu.make_async_copy(x_hbm.at[pl.ds(src_row, PACK_SLABS), :], stg.at[dst, :], sem.at[sl * nch + g])

    def group_wait(sl, g):
        pltpu.make_async_copy(x_hbm.at[pl.ds(0, gp), :], stg.at[pl.ds(sl * tp + g * gp, gp), :],
                              sem.at[sl * nch + g]).wait()

    def unpack_group(sl, g):
        rows_g = slice(g * ps, (g + 1) * ps)
        for s in range(PACK_SLABS):
            lo, hi = _unpack_bf16_pairs(stg[pl.ds(sl * tp + g * gp + s, ps, stride=PACK_SLABS), :])
            xbb[rows_g, s * LANES:(s + 1) * LANES] = lo
            xbb[rows_g, D_MODEL // 2 + s * LANES:D_MODEL // 2 + (s + 1) * LANES] = hi

    def ffn_rows(rs, wg, wu, wd):
        xb = xbb[rs, :]
        g = jnp.minimum(_dot(xb, wg) + bg_ref[0], SWIGLU_LIMIT)
        u = jnp.clip(_dot(xb, wu) + bu_ref[0], -SWIGLU_LIMIT, SWIGLU_LIMIT)
        hid = (u + 1.0) * (g * _sigmoid(SWIGLU_ALPHA * g))
        o_ref[rs, :] += _dot(hid.astype(BF16), wd)

    @pl.when(t >= n_used)
    def _():
        @pl.when(c == 0)
        def _():
            o_ref[...] = jnp.zeros((tm, D_MODEL), F32)

    @pl.when(t < n_used)
    def _():
        rows = tr_ref[t]
        rows_next = jnp.where(t + 1 < n_used, tr_ref[jnp.minimum(t + 1, n_used - 1)], 0)

        @pl.when((t == 0) & (c == 0))
        def _():
            xbb[...] = jnp.zeros((tm, D_MODEL), BF16)
            for g in range(nch):
                @pl.when(g * ps < rows)
                def _():
                    def issue(r, carry):
                        row_copy(idc_ref[0, 0, g * ps + r], 0, g, r).start()
                        return carry
                    lax.fori_loop(0, ps, issue, 0, unroll=8)

        @pl.when(c == 0)
        def _():
            for g in range(nch):
                @pl.when(g * ps < rows)
                def _():
                    group_wait(cur, g)
                    unpack_group(cur, g)

        @pl.when(c * ps < rows_next)
        def _():
            for r in range(ps):
                row_copy(idn_ref[0, 0, c * ps + r], 1 - cur, c, r).start()

        for nb in range(1, tm // sb + 1):
            @pl.when((rows > (nb - 1) * sb) & (rows <= nb * sb))
            def _():
                @pl.when(c == 0)
                def _():
                    o_ref[0:nb * sb, :] = jnp.broadcast_to(bd_ref[0], (nb * sb, D_MODEL))
                    if nb * sb < tm:
                        o_ref[nb * sb:, :] = jnp.zeros((tm - nb * sb, D_MODEL), F32)
                ffn_rows(slice(0, nb * sb), wg_ref[0, 0].astype(BF16), wu_ref[0, 0].astype(BF16),
                         wd_ref[0, 0].astype(BF16))


def _moe_ffn(x, row_tok, tile_e, tile_rows, n_used, wg, bg, wu, bu, wd, bd, tm, sb):
    P = row_tok.shape[0]
    n_tiles = P // tm
    nch = D_FF // MOE_FF_CHUNK
    fc = MOE_FF_CHUNK
    assert tm % nch == 0 and (tm // nch) % 16 == 0 and nch >= 3

    def tile(t, nu):
        return jnp.minimum(t, jnp.maximum(nu[0] - 1, 0))

    def chunk(t, c, nu):
        return jnp.where(t < nu[0], c, nch - 1)

    idx_blk = (1, 1, tm)
    return pl.pallas_call(
        functools.partial(_ffn_body, tm=tm, sb=sb, nch=nch),
        grid_spec=pltpu.PrefetchScalarGridSpec(
            num_scalar_prefetch=3,
            grid=(n_tiles, nch),
            in_specs=[
                pl.BlockSpec(idx_blk, lambda t, c, te, tr, nu: (tile(t, nu), 0, 0), memory_space=pltpu.SMEM),
                pl.BlockSpec(idx_blk, lambda t, c, te, tr, nu: (tile(t + 1, nu), 0, 0), memory_space=pltpu.SMEM),
                pl.BlockSpec(memory_space=pl.ANY),
                pl.BlockSpec((1, 1, D_MODEL, fc), lambda t, c, te, tr, nu: (0, te[tile(t, nu)], 0, chunk(t, c, nu))),
                pl.BlockSpec((1, 1, fc), lambda t, c, te, tr, nu: (te[tile(t, nu)], 0, chunk(t, c, nu))),
                pl.BlockSpec((1, 1, D_MODEL, fc), lambda t, c, te, tr, nu: (0, te[tile(t, nu)], 0, chunk(t, c, nu))),
                pl.BlockSpec((1, 1, fc), lambda t, c, te, tr, nu: (te[tile(t, nu)], 0, chunk(t, c, nu))),
                pl.BlockSpec((1, 1, fc, D_MODEL), lambda t, c, te, tr, nu: (0, te[tile(t, nu)], chunk(t, c, nu), 0)),
                pl.BlockSpec((1, 1, D_MODEL), lambda t, c, te, tr, nu: (te[tile(t, nu)], 0, 0)),
            ],
            out_specs=pl.BlockSpec((tm, D_MODEL), lambda t, c, te, tr, nu: (t, 0)),
            scratch_shapes=[pltpu.VMEM((2 * tm * PACK_SLABS, LANES), jnp.uint32),
                            pltpu.SemaphoreType.DMA((2 * nch,)),
                            pltpu.VMEM((tm, D_MODEL), BF16)]),
        out_shape=jax.ShapeDtypeStruct((P, D_MODEL), F32),
        compiler_params=_cparams(2),
        name="moe_ffn",
    )(tile_e, tile_rows, n_used, row_tok.reshape(n_tiles, 1, tm), row_tok.reshape(n_tiles, 1, tm), x,
      wg, bg, wu, bu, wd, bd)


def _combine_body(posc_ref, posn_ref, x2_ref, tg_ref, ys_hbm, g_ref, b_ref, oa_ref, ob_ref, buf, sem, *,
                  tc, nblk, nblk_a):
    i = pl.program_id(0)
    slot = lax.rem(i, 2)

    def row_copy(src_row, sl, k, j):
        return pltpu.make_async_copy(ys_hbm.at[pl.ds(src_row, 1), :], buf.at[sl, k, pl.ds(j, 1), :], sem.at[sl])

    def slot_wait(sl):
        pltpu.make_async_copy(buf.at[sl], buf.at[sl], sem.at[sl]).wait()

    @pl.when(i == 0)
    def _():
        def issue(j, carry):
            for k in range(TOP_K):
                row_copy(posc_ref[0, 0, j * TOP_K + k], 0, k, j).start()
            return carry
        lax.fori_loop(0, tc, issue, 0, unroll=4)

    slot_wait(slot)
    for j in range(tc):
        for k in range(TOP_K):
            row_copy(posn_ref[0, 0, j * TOP_K + k], 1 - slot, k, j).start()
    y = None
    for k in range(TOP_K):
        term = buf[slot, k] * tg_ref[:, k:k + 1]
        y = term if y is None else y + term
    out = _layer_norm(DN_ALPHA * x2_ref[...] + y, g_ref[...], b_ref[...])

    @pl.when(i < nblk_a)
    def _():
        oa_ref[...] = out

    @pl.when(i >= nblk_a)
    def _():
        ob_ref[...] = out

    @pl.when(i == nblk - 1)
    def _():
        slot_wait(1 - slot)


def _combine_ln3(x2, tg, ys, pos, ln_g, ln_b, tc, split):
    M = pos.shape[0] // TOP_K
    nblk = M // tc
    nblk_a = split // tc
    assert split % tc == 0 and 0 < nblk_a < nblk
    pos3 = pos.reshape(nblk, 1, tc * TOP_K)
    return pl.pallas_call(
        functools.partial(_combine_body, tc=tc, nblk=nblk, nblk_a=nblk_a),
        grid=(nblk,),
        in_specs=[pl.BlockSpec((1, 1, tc * TOP_K), lambda i: (i, 0, 0), memory_space=pltpu.SMEM),
                  pl.BlockSpec((1, 1, tc * TOP_K), lambda i: (jnp.minimum(i + 1, nblk - 1), 0, 0),
                               memory_space=pltpu.SMEM),
                  pl.BlockSpec((tc, D_MODEL), lambda i: (i, 0)),
                  pl.BlockSpec((tc, LANES), lambda i: (i, 0)),
                  pl.BlockSpec(memory_space=pl.ANY),
                  _resident(ln_g.shape), _resident(ln_b.shape)],
        out_specs=[pl.BlockSpec((tc, D_MODEL), lambda i: (jnp.minimum(i, nblk_a - 1), 0)),
                   pl.BlockSpec((tc, D_MODEL), lambda i: (jnp.maximum(i - nblk_a, 0), 0))],
        out_shape=[jax.ShapeDtypeStruct((split, D_MODEL), F32), jax.ShapeDtypeStruct((M - split, D_MODEL), F32)],
        scratch_shapes=[pltpu.VMEM((2, TOP_K, tc, D_MODEL), F32), pltpu.SemaphoreType.DMA((2,))],
        compiler_params=_cparams(1),
        name="moe_combine",
    )(pos3, pos3, x2, tg, ys, ln_g, ln_b)


def _moe(x2, x2_packed, topi, gates, wg, bg, wu, bu, wd, bd, ln_g, ln_b, tm, tc, split):
    ntok = topi.shape[0]
    nk = ntok * TOP_K
    sb = min(tm, MOE_SUB_ROWS)
    n_tiles = -(-(nk + N_EXPERTS * (tm - 1)) // tm)
    e_flat = topi[:, :TOP_K].reshape(nk)
    onehot = (e_flat[:, None] == jnp.arange(N_EXPERTS, dtype=I32)[None, :]).astype(I32)
    csum = jnp.cumsum(onehot, axis=0)
    counts = csum[-1]
    rank = jnp.take_along_axis(csum, e_flat[:, None], axis=1)[:, 0] - 1
    tiles_e = (counts + tm - 1) // tm
    tile_end = jnp.cumsum(tiles_e)
    tile_start = tile_end - tiles_e
    dest = tile_start[e_flat] * tm + rank
    n_used = tile_end[-1:].astype(I32)
    t_ids = jnp.arange(n_tiles, dtype=I32)
    tile_e = jnp.minimum(jnp.searchsorted(tile_end, t_ids, side="right"), N_EXPERTS - 1).astype(I32)
    tile_rows = jnp.clip(counts[tile_e] - (t_ids - tile_start[tile_e]) * tm, 0, tm)
    tile_rows = jnp.where(t_ids < n_used[0], tile_rows, 0).astype(I32)
    P = n_tiles * tm
    first_row = (jnp.arange(nk, dtype=I32) // TOP_K) * PACK_SLABS
    row_src = jnp.zeros((P,), I32).at[dest].set(first_row, unique_indices=True)
    ys = _moe_ffn(x2_packed, row_src, tile_e, tile_rows, n_used, wg, bg, wu, bu, wd, bd, tm, sb)
    return _combine_ln3(x2, gates, ys, dest, ln_g, ln_b, tc, split)


def kernel(x_prompt, x_sample, mem_prompt, state_mlstm_C, state_mlstm_n, state_mlstm_m, cache_swa_k, cache_swa_v, cache_mem_k, cache_mem_v, w_in, b_igate, b_fgate, g_mnorm, w_mix_out, ln1_g, ln1_b, w_xq, w_xk, w_xv, w_xo, ln2_g, ln2_b, w_router, b_router, w_gate, b_gate, w_up, b_up, w_down, b_down, ln3_g, ln3_b):
    assert DEPTH == 1
    B, T, _ = x_prompt.shape
    DB, S, _ = x_sample.shape
    assert B == 1
    n_mem = mem_prompt.shape[1]
    wb = cache_swa_k.shape[2]
    row2 = lambda a: a[0].reshape(1, -1)

    wi = w_in[0]
    c0 = 2 * MIX_QK + 2 * MIX_V
    w_m = wi[:, :c0].astype(BF16)
    w_g = jnp.pad(wi[:, c0:c0 + 2 * M_HEADS], ((0, 0), (0, LANES - 2 * M_HEADS))).astype(BF16)
    w_a = wi[:, c0 + 2 * M_HEADS:].astype(BF16)
    gate_bias = jnp.pad(jnp.concatenate([b_igate[0], b_fgate[0]]), (0, LANES - 2 * M_HEADS)).reshape(1, LANES)
    w = dict(
        wmo=w_mix_out[0].astype(BF16), ln1_g=row2(ln1_g), ln1_b=row2(ln1_b), wxq=w_xq[0].astype(BF16),
        wxo=w_xo[0].astype(BF16), ln2_g=row2(ln2_g), ln2_b=row2(ln2_b),
        wr=jnp.pad(w_router[0], ((0, 0), (0, LANES - N_EXPERTS))).astype(BF16),
        br=jnp.pad(b_router[0], (0, LANES - N_EXPERTS)).reshape(1, LANES),
        wg=w_gate, bg=b_gate[0].reshape(N_EXPERTS, 1, D_FF), wu=w_up, bu=b_up[0].reshape(N_EXPERTS, 1, D_FF),
        wd=w_down, bd=b_down[0].reshape(N_EXPERTS, 1, D_MODEL), ln3_g=row2(ln3_g), ln3_b=row2(ln3_b))
    gm = g_mnorm[0].reshape(1, MIX_V)

    xp = x_prompt.reshape(T, D_MODEL)
    zm, zg = _mm_multi(xp, [w_m, w_g], [BF16, F32], 1024, "proj_mlstm")
    dils = tuple(dil for _, dil in PATTERNS)
    wbp = min(wb, T)
    *za_views, kv_tail = _proj_attn(xp, w_a, dils, wbp)
    zeros_c = jnp.zeros((1, M_HEADS, M_DQK, M_DV), F32)
    zeros_n = jnp.zeros((1, M_HEADS, M_DQK), F32)
    zeros_m = jnp.zeros((1, M_HEADS, LANES), F32)
    hm, pC, pn, pm = _mlstm(zm, zg, gate_bias, gm, zeros_c, zeros_n, zeros_m, 1, T // 512, L=512)
    pats = [_dilated_pattern(zv, window, dil) for zv, (window, dil) in zip(za_views, PATTERNS)]
    mp = mem_prompt.reshape(n_mem, D_MODEL)
    mem_k = _mm(mp, w_xk[0].astype(BF16), F32, n_mem, 1024, "mem_k")
    mem_v = _mm(mp, w_xv[0].astype(BF16), F32, n_mem, 1024, "mem_v")
    mem_k5 = mem_k.reshape(1, B, n_mem, X_HEADS, X_DH)
    mem_v5 = mem_v.reshape(1, B, n_mem, X_HEADS, X_DH)
    x1_p, q_p = _mix_ln1_q(xp, hm, [p[0] for p in pats], [p[1] for p in pats], dils, w["wmo"], w["ln1_g"], w["ln1_b"],
                           w["wxq"], 512)
    o_p = _xattn(q_p.reshape(1, T, D_MODEL), mem_k.astype(BF16).reshape(1, n_mem, D_MODEL),
                 mem_v.astype(BF16).reshape(1, n_mem, D_MODEL), 1024).reshape(T, D_MODEL)

    ns = DB * S
    xs_ = x_sample.reshape(ns, D_MODEL)
    zm_s, zg_s, za_s = _mm_multi(xs_, [w_m, w_g, w_a], [BF16, F32, F32], ns, "proj_sample")
    pad_rows = 128 - S
    zm_pad = jnp.pad(zm_s.reshape(DB, S, -1), ((0, 0), (0, pad_rows), (0, 0))).reshape(DB * 128, -1)
    lane = jnp.arange(LANES)
    neutral = jnp.where(lane < M_HEADS, NEG_BIG, jnp.where(lane < 2 * M_HEADS, -NEG_BIG, 0.0)).astype(F32)
    zg_pad = jnp.concatenate([zg_s.reshape(DB, S, LANES), jnp.broadcast_to(neutral, (DB, pad_rows, LANES))],
                             axis=1).reshape(DB * 128, LANES)
    m0 = jnp.broadcast_to(state_mlstm_m[0][:, :, None], (DB, M_HEADS, LANES))
    hm_s, sC, sn, sm = _mlstm(zm_pad, zg_pad, gate_bias, gm, state_mlstm_C[0], state_mlstm_n[0], m0, DB, 1)
    hm_s = hm_s.reshape(DB, 128, MIX_V)[:, :S].reshape(ns, MIX_V)
    za_pad = jnp.pad(za_s.reshape(DB, S, -1), ((0, 0), (0, SWA_ROWS - S), (0, 0))).astype(BF16)
    ha_s = _swa_sample(za_pad, cache_swa_k.reshape(DB, wb * A_HEADS, A_DH),
                       cache_swa_v.reshape(DB, wb * A_HEADS, A_DH), S)[:, :S].reshape(ns, MIX_A)
    x1_s, q_s = _mix_ln1_q(xs_, hm_s, [ha_s], [], (1,), w["wmo"], w["ln1_g"], w["ln1_b"], w["wxq"], ns)
    q_pad = jnp.pad(q_s.reshape(DB, S, D_MODEL), ((0, 0), (0, SWA_ROWS - S), (0, 0)))
    o_s = _xattn_cache(q_pad, cache_mem_k, cache_mem_v)[:, :S].reshape(ns, D_MODEL)
    x2_s, xk_s, topi_s, gates_s = _xo_ln2_router(x1_s, o_s, w["wxo"], w["ln2_g"], w["ln2_b"], w["wr"], w["br"], ns)

    x2_all, xk_all, topi_p, gates_p = _xo_ln2_router(x1_p, o_p, w["wxo"], w["ln2_g"], w["ln2_b"], w["wr"], w["br"],
                                                     512, tail=(x2_s, xk_s))
    yp, ys_ = _moe(x2_all, xk_all, jnp.concatenate([topi_p, topi_s]),
                   jnp.concatenate([gates_p, gates_s]), w["wg"], w["bg"], w["wu"], w["bu"], w["wd"], w["bd"],
                   w["ln3_g"], w["ln3_b"], 1024, 128, T)

    return (yp.reshape(B, T, D_MODEL), ys_.reshape(DB, S, D_MODEL),
            pC[None], pn[None], pm[:, :, 0][None],
            kv_tail[:, :MIX_A].reshape(1, B, wbp, A_HEADS, A_DH), kv_tail[:, MIX_A:].reshape(1, B, wbp, A_HEADS, A_DH),
            mem_k5, mem_v5,
            sC[None], sn[None], sm[:, :, 0][None],
            za_s[:, MIX_A:2 * MIX_A].reshape(1, DB, S, A_HEADS, A_DH),
            za_s[:, 2 * MIX_A:].reshape(1, DB, S, A_HEADS, A_DH))
```
